```python
import jax, jax.numpy as jnp
from jax import lax
import numpy as np


D_MODEL = 1024
BATCH = 16
SEQ = 2048
DEPTH = 1
DEC_BATCH = 128
DEC_SEQ = 4
PAST_LEN = 16384
PAGE_SIZE = 128

H_A = 4
DV_A = D_MODEL // H_A
DK_A = DV_A // 2
CHUNK = 64
F_BIAS_LO = 3.0
F_BIAS_HI = 6.0
V_DIM = 128
H_B = D_MODEL // V_DIM
NOPE_DIM = 128
ROPE_DIM = 64
Q_LORA = 256
KV_LORA = 128
ROPE_THETA = 10000.0
Q_BLOCK = 128
SM_SCALE = (NOPE_DIM + ROPE_DIM) ** -0.5
D_FF = 2816
EPS = 1e-6
POOL_NUM = 5
POOL_DEN = 4
IN_SPLITS = (H_A * DK_A, H_A * DK_A, H_A * DV_A, H_A, H_A, H_A * DV_A, Q_LORA, KV_LORA, ROPE_DIM, D_MODEL, D_MODEL)
IN_COLS = 2 * H_A * DK_A + 2 * H_A * DV_A + 2 * H_A + Q_LORA + KV_LORA + ROPE_DIM + 2 * D_MODEL

kernel_name = "mlstm_mla_macaron_sandwich_step"


def rmsnorm(x, g):
    xf = x.astype(jnp.float32)
    y = xf * lax.rsqrt(jnp.mean(xf * xf, axis=-1, keepdims=True) + EPS)
    return (y * g.astype(jnp.float32)).astype(x.dtype)


def swiglu(x, w_gu, w_down):
    g, u = jnp.split(x @ w_gu, 2, axis=-1)
    return (jax.nn.silu(g) * u) @ w_down


def rope(x, pos):
    half = ROPE_DIM // 2
    inv_freq = ROPE_THETA ** (-jnp.arange(half, dtype=jnp.float32) / half)
    ang = pos.astype(jnp.float32)[:, None] * inv_freq[None, :]
    cos = jnp.cos(ang)[None, :, None, :]
    sin = jnp.sin(ang)[None, :, None, :]
    xf = x.astype(jnp.float32)
    x1, x2 = xf[..., :half], xf[..., half:]
    return jnp.concatenate([x1 * cos - x2 * sin, x2 * cos + x1 * sin], axis=-1).astype(x.dtype)


def mlstm_chunkwise(q, k, v, i_pre, logf, C0, n0, m0):
    B, T, H, _ = q.shape
    L = CHUNK if T % CHUNK == 0 else T
    nc = T // L

    def to_chunks(a):
        a = a.astype(jnp.float32).reshape((B, nc, L) + a.shape[2:])
        return jnp.moveaxis(a, (1, 3), (0, 2))

    causal = jnp.tril(jnp.ones((L, L), dtype=bool))

    def step(carry, inp):
        C, n, m = carry
        qc, kc, vc, ic, fc = inp
        b = jnp.cumsum(fc, axis=-1)
        D = b[..., :, None] - b[..., None, :] + ic[..., None, :]
        D = jnp.where(causal, D, -jnp.inf)
        m_inter = b + m[..., None]
        m_t = jnp.maximum(m_inter, jnp.max(D, axis=-1))
        w = jnp.exp(D - m_t[..., None])
        inter = jnp.exp(m_inter - m_t)
        s = jnp.einsum('bhtd,bhsd->bhts', qc, kc) * w
        num = jnp.einsum('bhts,bhsv->bhtv', s, vc) + inter[..., None] * jnp.einsum('bhvd,bhtd->bhtv', C, qc)
        den = jnp.sum(s, axis=-1) + inter * jnp.einsum('bhd,bhtd->bht', n, qc)
        h = num / jnp.maximum(jnp.abs(den), jnp.exp(-m_t))[..., None]
        bL = b[..., -1]
        g = bL[..., None] - b + ic
        m_new = jnp.maximum(bL + m, jnp.max(g, axis=-1))
        a = jnp.exp(g - m_new[..., None])
        decay = jnp.exp(bL + m - m_new)
        C_new = decay[..., None, None] * C + jnp.einsum('bhs,bhsv,bhsd->bhvd', a, vc, kc)
        n_new = decay[..., None] * n + jnp.einsum('bhs,bhsd->bhd', a, kc)
        return (C_new, n_new, m_new), h

    carry0 = (C0.astype(jnp.float32), n0.astype(jnp.float32), m0.astype(jnp.float32))
    (C, n, m), hs = lax.scan(step, carry0, (to_chunks(q), to_chunks(k), to_chunks(v), to_chunks(i_pre), to_chunks(logf)))
    hs = jnp.moveaxis(hs, (0, 2), (1, 3)).reshape(B, T, H, v.shape[-1])
    return hs.astype(v.dtype), (C, n, m)


def mla_scores(q_lat, q_rope, ckv, krope):
    s = jnp.einsum('bthc,bsc->bhts', q_lat, ckv) + jnp.einsum('bthr,bsr->bhts', q_rope, krope)
    return s.astype(jnp.float32) * SM_SCALE


def mla_prompt(q_lat, q_rope, ckv, krope):
    B, T, H, C = q_lat.shape
    nb = T // Q_BLOCK
    kpos = jnp.arange(T)

    def blk(i):
        start = i * Q_BLOCK
        ql = lax.dynamic_slice_in_dim(q_lat, start, Q_BLOCK, axis=1)
        qr = lax.dynamic_slice_in_dim(q_rope, start, Q_BLOCK, axis=1)
        s = mla_scores(ql, qr, ckv, krope)
        qpos = start + jnp.arange(Q_BLOCK)
        s = jnp.where(kpos[None, :] <= qpos[:, None], s, -jnp.inf)
        p = jax.nn.softmax(s, axis=-1).astype(ckv.dtype)
        return jnp.einsum('bhts,bsc->bthc', p, ckv)

    o = lax.map(blk, jnp.arange(nb))
    return jnp.moveaxis(o, 0, 1).reshape(B, T, H, C)


def mla_sample(q_lat, q_rope, ckv_past, kr_past, ckv_new, kr_new):
    T = q_lat.shape[1]
    P = ckv_past.shape[1]
    s_past = mla_scores(q_lat, q_rope, ckv_past, kr_past)
    s_new = mla_scores(q_lat, q_rope, ckv_new, kr_new)
    s_new = jnp.where(jnp.tril(jnp.ones((T, T), dtype=bool)), s_new, -jnp.inf)
    p = jax.nn.softmax(jnp.concatenate([s_past, s_new], axis=-1), axis=-1).astype(ckv_new.dtype)
    return jnp.einsum('bhts,bsc->bthc', p[..., :P], ckv_past) + jnp.einsum('bhts,bsc->bthc', p[..., P:], ckv_new)


def mixer(h, pos, C0, n0, m0, ckv_past, kr_past, lw):
    B, T, _ = h.shape
    proj = h @ lw['w_in']
    parts = []
    off = 0
    for width in IN_SPLITS:
        parts.append(proj[..., off:off + width])
        off += width
    qa, ka, va, ip, fp, oa, cq, ckv, kr, ga, gb = parts
    qa = qa.reshape(B, T, H_A, DK_A)
    ka = ka.reshape(B, T, H_A, DK_A) * (DK_A ** -0.5)
    va = va.reshape(B, T, H_A, DV_A)
    gates = jnp.concatenate([ip, fp], axis=-1).astype(jnp.float32) + lw['b_gates'].astype(jnp.float32)
    i_pre = gates[..., :H_A]
    logf = jax.nn.log_sigmoid(gates[..., H_A:])
    ha, (C, n, m) = mlstm_chunkwise(qa, ka, va, i_pre, logf, C0, n0, m0)
    ha = rmsnorm(ha, lw['norm_mlstm_h'].reshape(H_A, DV_A)).reshape(B, T, D_MODEL)
    y_a = jax.nn.sigmoid(oa) * ha
    cq = rmsnorm(cq, lw['norm_q_lat'])
    q = (cq @ lw['w_uq']).reshape(B, T, H_B, NOPE_DIM + ROPE_DIM)
    q_nope = q[..., :NOPE_DIM]
    q_rope = rope(q[..., NOPE_DIM:], pos)
    ckv = rmsnorm(ckv, lw['norm_kv_lat'])
    kr = rope(kr[:, :, None, :], pos)[:, :, 0, :]
    q_lat = jnp.einsum('bthd,chd->bthc', q_nope, lw['w_uk'])
    if ckv_past is None:
        o_lat = mla_prompt(q_lat, q_rope, ckv, kr)
    else:
        o_lat = mla_sample(q_lat, q_rope, ckv_past, kr_past, ckv, kr)
    y_b = jnp.einsum('bthc,chd->bthd', o_lat, lw['w_uv']).reshape(B, T, D_MODEL)
    merged = jax.nn.sigmoid(ga) * y_a + jax.nn.sigmoid(gb) * y_b
    return merged @ lw['w_out'], (ckv, kr, C, n, m)


def trunk_layer(x, pos, C0, n0, m0, ckv_past, kr_past, lw):
    f1 = swiglu(rmsnorm(x, lw['norm_ffn1_pre']), lw['w_ffn1_gu'], lw['w_ffn1_down'])
    x = x + 0.5 * rmsnorm(f1, lw['norm_ffn1_post'])
    mix, state = mixer(rmsnorm(x, lw['norm_mix_pre']), pos, C0, n0, m0, ckv_past, kr_past, lw)
    x = x + rmsnorm(mix, lw['norm_mix_post'])
    f2 = swiglu(rmsnorm(x, lw['norm_ffn2_pre']), lw['w_ffn2_gu'], lw['w_ffn2_down'])
    x = x + 0.5 * rmsnorm(f2, lw['norm_ffn2_post'])
    return x, state


def setup_inputs(seed: int = 0) -> dict:
    key = jax.random.key(seed)
    ks = jax.random.split(key, 40)
    n_pages = PAST_LEN // PAGE_SIZE
    n_pool = (DEC_BATCH * n_pages * POOL_NUM) // POOL_DEN

    def nrm(k, shape, scale):
        return jax.random.normal(k, shape, jnp.float32) * scale

    def gain(k, shape):
        return 1.0 + 0.05 * jax.random.normal(k, shape, jnp.float32)

    f_bias = jnp.linspace(F_BIAS_LO, F_BIAS_HI, H_A, dtype=jnp.float32)[None, :] + nrm(ks[30], (DEPTH, H_A), 0.1)
    i_bias = nrm(ks[31], (DEPTH, H_A), 0.1)
    page_table = jax.random.permutation(ks[7], n_pool)[: DEC_BATCH * n_pages].reshape(DEC_BATCH, n_pages).astype(jnp.int32)
    return {
        'x_prompt': nrm(ks[0], (BATCH, SEQ, D_MODEL), 1.0),
        'x_sample': nrm(ks[1], (DEC_BATCH, DEC_SEQ, D_MODEL), 1.0),
        'cache_kv_latent': nrm(ks[2], (DEPTH, n_pool, PAGE_SIZE, KV_LORA), 1.0),
        'cache_k_rope': nrm(ks[3], (DEPTH, n_pool, PAGE_SIZE, ROPE_DIM), 1.0),
        'state_mlstm_C': nrm(ks[4], (DEPTH, DEC_BATCH, H_A, DV_A, DK_A), 0.1),
        'state_mlstm_n': nrm(ks[5], (DEPTH, DEC_BATCH, H_A, DK_A), 0.5),
        'state_mlstm_m': nrm(ks[6], (DEPTH, DEC_BATCH, H_A), 1.0),
        'page_table': page_table,
        'norm_ffn1_pre': gain(ks[8], (DEPTH, D_MODEL)),
        'norm_ffn1_post': gain(ks[9], (DEPTH, D_MODEL)),
        'w_ffn1_gu': nrm(ks[10], (DEPTH, D_MODEL, 2 * D_FF), D_MODEL ** -0.5),
        'w_ffn1_down': nrm(ks[11], (DEPTH, D_FF, D_MODEL), D_FF ** -0.5),
        'norm_mix_pre': gain(ks[12], (DEPTH, D_MODEL)),
        'norm_mix_post': gain(ks[13], (DEPTH, D_MODEL)),
        'w_in': nrm(ks[14], (DEPTH, D_MODEL, IN_COLS), D_MODEL ** -0.5),
        'b_gates': jnp.concatenate([i_bias, f_bias], axis=-1),
        'w_uq': nrm(ks[15], (DEPTH, Q_LORA, H_B * (NOPE_DIM + ROPE_DIM)), Q_LORA ** -0.5),
        'norm_q_lat': gain(ks[16], (DEPTH, Q_LORA)),
        'norm_kv_lat': gain(ks[17], (DEPTH, KV_LORA)),
        'w_uk': nrm(ks[18], (DEPTH, KV_LORA, H_B, NOPE_DIM), KV_LORA ** -0.5),
        'w_uv': nrm(ks[19], (DEPTH, KV_LORA, H_B, V_DIM), KV_LORA ** -0.5),
        'norm_mlstm_h': gain(ks[20], (DEPTH, H_A * DV_A)),
        'w_out': nrm(ks[21], (DEPTH, D_MODEL, D_MODEL), D_MODEL ** -0.5),
        'norm_ffn2_pre': gain(ks[22], (DEPTH, D_MODEL)),
        'norm_ffn2_post': gain(ks[23], (DEPTH, D_MODEL)),
        'w_ffn2_gu': nrm(ks[24], (DEPTH, D_MODEL, 2 * D_FF), D_MODEL ** -0.5),
        'w_ffn2_down': nrm(ks[25], (DEPTH, D_FF, D_MODEL), D_FF ** -0.5),
    }


def reference(x_prompt, x_sample, cache_kv_latent, cache_k_rope, state_mlstm_C, state_mlstm_n, state_mlstm_m,
              page_table, norm_ffn1_pre, norm_ffn1_post, w_ffn1_gu, w_ffn1_down, norm_mix_pre, norm_mix_post,
              w_in, b_gates, w_uq, norm_q_lat, norm_kv_lat, w_uk, w_uv, norm_mlstm_h, w_out,
              norm_ffn2_pre, norm_ffn2_post, w_ffn2_gu, w_ffn2_down):
    B_p, T_p, _ = x_prompt.shape
    B_s, T_s, _ = x_sample.shape
    past_len = page_table.shape[1] * PAGE_SIZE
    pos_p = jnp.arange(T_p, dtype=jnp.int32)
    pos_s = past_len + jnp.arange(T_s, dtype=jnp.int32)
    zC = jnp.zeros((B_p, H_A, DV_A, DK_A), jnp.float32)
    zn = jnp.zeros((B_p, H_A, DK_A), jnp.float32)
    zm = jnp.zeros((B_p, H_A), jnp.float32)
    yp, ys = x_prompt, x_sample
    p_states, s_states = [], []
    for l in range(DEPTH):
        lw = dict(norm_ffn1_pre=norm_ffn1_pre[l], norm_ffn1_post=norm_ffn1_post[l], w_ffn1_gu=w_ffn1_gu[l],
                  w_ffn1_down=w_ffn1_down[l], norm_mix_pre=norm_mix_pre[l], norm_mix_post=norm_mix_post[l],
                  w_in=w_in[l], b_gates=b_gates[l], w_uq=w_uq[l], norm_q_lat=norm_q_lat[l],
                  norm_kv_lat=norm_kv_lat[l], w_uk=w_uk[l], w_uv=w_uv[l], norm_mlstm_h=norm_mlstm_h[l],
                  w_out=w_out[l], norm_ffn2_pre=norm_ffn2_pre[l], norm_ffn2_post=norm_ffn2_post[l],
                  w_ffn2_gu=w_ffn2_gu[l], w_ffn2_down=w_ffn2_down[l])
        yp, st_p = trunk_layer(yp, pos_p, zC, zn, zm, None, None, lw)
        p_states.append(st_p)
        ckv_past = cache_kv_latent[l][page_table].reshape(B_s, past_len, KV_LORA)
        kr_past = cache_k_rope[l][page_table].reshape(B_s, past_len, ROPE_DIM)
        ys, st_s = trunk_layer(ys, pos_s, state_mlstm_C[l], state_mlstm_n[l], state_mlstm_m[l], ckv_past, kr_past, lw)
        s_states.append(st_s)
    p_kv = jnp.stack([s[0] for s in p_states])
    p_kr = jnp.stack([s[1] for s in p_states])
    p_C = jnp.stack([s[2] for s in p_states])
    p_n = jnp.stack([s[3] for s in p_states])
    p_m = jnp.stack([s[4] for s in p_states])
    s_kv = jnp.stack([s[0] for s in s_states])
    s_kr = jnp.stack([s[1] for s in s_states])
    s_C = jnp.stack([s[2] for s in s_states])
    s_n = jnp.stack([s[3] for s in s_states])
    s_m = jnp.stack([s[4] for s in s_states])
    return (yp, ys, p_kv, p_kr, p_C, p_n, p_m, s_kv, s_kr, s_C, s_n, s_m)
```

```python
import functools

import jax
import jax.numpy as jnp
from jax import lax
from jax.experimental import pallas as pl
from jax.experimental.pallas import tpu as pltpu

F32 = jnp.float32
BF16 = jnp.bfloat16

D_MODEL = 1024
H_A = 4
DV_A = D_MODEL // H_A
DK_A = DV_A // 2
V_DIM = 128
H_B = D_MODEL // V_DIM
NOPE_DIM = 128
ROPE_DIM = 64
Q_LORA = 256
KV_LORA = 128
ROPE_THETA = 10000.0
SM_SCALE = (NOPE_DIM + ROPE_DIM) ** -0.5
D_FF = 2816
EPS = 1e-6
PAGE_SIZE = 128
QK_PAD = 256

LANE = 128
VMEM_LIMIT = 56 * 1024 * 1024
ROW_TILE = 512
FF_CHUNK = 256
MLSTM_CHUNK = 256
ATTN_TILE = 256
PAGES_PER_STEP = 32
NEG_BIG = -1e30


def _params(*sem):
    return pltpu.CompilerParams(dimension_semantics=sem, vmem_limit_bytes=VMEM_LIMIT)


def _resident(shape):
    return pl.BlockSpec(shape, lambda *_: (0,) * len(shape), pipeline_mode=pl.Buffered(1))


def _rms(x, g):
    return x * lax.rsqrt(jnp.mean(x * x, axis=-1, keepdims=True) + EPS) * g


def _dot(a, b):
    return jnp.dot(a, b, preferred_element_type=F32)


def _dot_nt(a, b):
    return lax.dot_general(a, b, (((1,), (1,)), ((), ())), preferred_element_type=F32)


def _dot_tn(a, b):
    return lax.dot_general(a, b, (((0,), (0,)), ((), ())), preferred_element_type=F32)


def _ffn_body(x, gpre, gpost, wgu_ref, wdn_ref, act_ref):
    h = _rms(x, gpre).astype(BF16)
    for lo in range(0, D_FF, FF_CHUNK):
        g = _dot(h, wgu_ref[:, lo:lo + FF_CHUNK])
        u = _dot(h, wgu_ref[:, D_FF + lo:D_FF + lo + FF_CHUNK])
        act_ref[:, lo:lo + FF_CHUNK] = (g * jax.nn.sigmoid(g) * u).astype(BF16)
    f = _dot(act_ref[...], wdn_ref[...])
    return x + 0.5 * _rms(f, gpost)


def _ffn_kernel(x_ref, gpre_ref, gpost_ref, wgu_ref, wdn_ref, o_ref, act_ref):
    o_ref[...] = _ffn_body(x_ref[...], gpre_ref[...], gpost_ref[...], wgu_ref, wdn_ref, act_ref)


def _ffn(x, gpre, gpost, wgu, wdn):
    m = x.shape[0]
    tm = min(ROW_TILE, m)
    row = pl.BlockSpec((tm, D_MODEL), lambda i: (i, 0))
    return pl.pallas_call(
        _ffn_kernel,
        grid=(m // tm,),
        in_specs=[row, _resident((1, D_MODEL)), _resident((1, D_MODEL)),
                  _resident((D_MODEL, 2 * D_FF)), _resident((D_FF, D_MODEL))],
        out_specs=row,
        out_shape=jax.ShapeDtypeStruct((m, D_MODEL), F32),
        scratch_shapes=[pltpu.VMEM((tm, D_FF), BF16)],
        compiler_params=_params("parallel"),
        name="ffn",
    )(x, gpre, gpost, wgu, wdn)


_O_QA, _O_KA, _O_VA, _O_OA, _O_GA, _O_GB = 0, 512, 1024, 2048, 3072, 4096
_O_CQ, _O_CKV, _O_KR, _O_KRS, _O_GT = 5120, 5376, 5504, 5632, 5760
IN_COLS_PAD = 5888


def _proj_kernel(x_ref, gpre_ref, win_ref, bg_ref, gq_ref, gkv_ref, wuq_ref, wuk_ref, cos_ref, sin_ref,
                 qa_ref, ka_ref, va_ref, ga_ref, gb_ref, gt_ref, ckv_ref, kr_ref, kcat_ref, qcat_ref):
    h = _rms(x_ref[...], gpre_ref[...]).astype(BF16)

    def seg(lo, n):
        return _dot(h, win_ref[:, lo:lo + n])

    qa_ref[...] = seg(_O_QA, H_A * DK_A).astype(BF16)
    ka_ref[...] = (seg(_O_KA, H_A * DK_A) * (DK_A ** -0.5)).astype(BF16)
    va_ref[...] = seg(_O_VA, D_MODEL).astype(BF16)
    ga_ref[...] = (jax.nn.sigmoid(seg(_O_GA, D_MODEL)) * jax.nn.sigmoid(seg(_O_OA, D_MODEL))).astype(BF16)
    gb_ref[...] = jax.nn.sigmoid(seg(_O_GB, D_MODEL)).astype(BF16)

    gt = seg(_O_GT, LANE) + bg_ref[...]
    logsig = jnp.minimum(gt, 0.0) - jnp.log1p(jnp.exp(-jnp.abs(gt)))
    lane = lax.broadcasted_iota(jnp.int32, gt.shape, 1)
    gt_ref[...] = jnp.where(lane < H_A, gt, logsig)

    cos = cos_ref[...]
    sin = sin_ref[...]
    ckv = _rms(seg(_O_CKV, KV_LORA), gkv_ref[...])
    kr = seg(_O_KR, LANE) * cos + seg(_O_KRS, LANE) * sin
    ckv_ref[...] = ckv
    kr_ref[...] = kr[:, :ROPE_DIM]
    kcat_ref[:, :KV_LORA] = ckv.astype(BF16)
    kcat_ref[:, KV_LORA:] = kr.astype(BF16)

    cq = _rms(seg(_O_CQ, Q_LORA), gq_ref[...]).astype(BF16)
    q = _dot(cq, wuq_ref[...])
    for hh in range(H_B):
        q_nope = q[:, hh * LANE:(hh + 1) * LANE].astype(BF16)
        q_lat = _dot(q_nope, wuk_ref[hh])
        lo = H_B * LANE + hh * LANE
        q_rope = q[:, lo:lo + LANE] * cos + q[:, lo + H_B * LANE:lo + (H_B + 1) * LANE] * sin
        qcat_ref[:, hh * QK_PAD:hh * QK_PAD + LANE] = q_lat.astype(BF16)
        qcat_ref[:, hh * QK_PAD + LANE:(hh + 1) * QK_PAD] = q_rope.astype(BF16)


def _proj(x, w, cos, sin):
    m = x.shape[0]
    tm = min(ROW_TILE, m)
    ntab = cos.shape[0] // tm

    def row(n):
        return pl.BlockSpec((tm, n), lambda i: (i, 0))

    tab = pl.BlockSpec((tm, LANE), lambda i: (i % ntab, 0))
    outs = [(H_A * DK_A, BF16), (H_A * DK_A, BF16), (D_MODEL, BF16), (D_MODEL, BF16), (D_MODEL, BF16),
            (LANE, F32), (KV_LORA, F32), (ROPE_DIM, F32), (QK_PAD, BF16), (H_B * QK_PAD, BF16)]
    return pl.pallas_call(
        _proj_kernel,
        grid=(m // tm,),
        in_specs=[row(D_MODEL), _resident((1, D_MODEL)), _resident((D_MODEL, IN_COLS_PAD)),
                  _resident((1, LANE)), _resident((1, Q_LORA)), _resident((1, KV_LORA)),
                  _resident((Q_LORA, 3 * H_B * LANE)), _resident((H_B, NOPE_DIM, KV_LORA)), tab, tab],
        out_specs=[row(n) for n, _ in outs],
        out_shape=[jax.ShapeDtypeStruct((m, n), dt) for n, dt in outs],
        compiler_params=_params("parallel"),
        name="proj",
    )(x, w["g_mix_pre"], w["w_in"], w["b_gates"], w["g_q"], w["g_kv"], w["w_uq"], w["w_uk"], cos, sin)


def _mlstm_chunk(q, k, v, i_row, f_row, c_st, n_row, m, tri, eye):
    ln = q.shape[0]
    b_col = jnp.sum(jnp.where(tri, f_row, 0.0), axis=1, keepdims=True)
    b_row = jnp.sum(jnp.where(eye, b_col, 0.0), axis=0, keepdims=True)
    d = jnp.where(tri, b_col - b_row + i_row, -jnp.inf)
    m_inter = b_col + m
    m_t = jnp.maximum(m_inter, jnp.max(d, axis=1, keepdims=True))
    wgt = jnp.exp(d - m_t)
    inter = jnp.exp(m_inter - m_t)
    s = _dot_nt(q, k) * wgt
    num = _dot(s.astype(BF16), v) + inter * _dot_nt(q, c_st.astype(BF16))
    qn = jnp.sum(q.astype(F32) * n_row, axis=1, keepdims=True)
    den = jnp.sum(s, axis=1, keepdims=True) + inter * qn
    h = num / jnp.maximum(jnp.abs(den), jnp.exp(-m_t))

    b_last = b_row[:, ln - 1:ln]
    g_row = b_last - b_row + i_row
    m_new = jnp.maximum(b_last + m, jnp.max(g_row, axis=1, keepdims=True))
    a_row = jnp.exp(g_row - m_new)
    decay = jnp.exp(b_last + m - m_new)
    a_col = jnp.sum(jnp.where(eye, a_row, 0.0), axis=1, keepdims=True)
    va = (v.astype(F32) * a_col).astype(BF16)
    c_new = decay * c_st + _dot_tn(va, k)
    n_new = decay * n_row + jnp.sum(a_col * k.astype(F32), axis=0, keepdims=True)
    return h, c_new, n_new, m_new


def _mlstm_kernel(*refs, chunk, n_chunks, has_init):
    if has_init:
        q_ref, k_ref, v_ref, gt_ref, gh_ref, c0_ref, n0_ref, m0_ref, h_ref, c_ref, n_ref, m_ref = refs
        init = (c0_ref[0, 0], n0_ref[0, 0], m0_ref[0, 0][:, :1])
    else:
        q_ref, k_ref, v_ref, gt_ref, gh_ref, h_ref, c_ref, n_ref, m_ref = refs
        init = (jnp.zeros((DV_A, DK_A), F32), jnp.zeros((1, DK_A), F32), jnp.zeros((1, 1), F32))
    t_idx = lax.broadcasted_iota(jnp.int32, (chunk, chunk), 0)
    s_idx = lax.broadcasted_iota(jnp.int32, (chunk, chunk), 1)
    tri = s_idx <= t_idx
    eye = s_idx == t_idx
    gh = gh_ref[...]

    def step(c, carry):
        c_st, n_row, m = carry
        rows = pl.ds(pl.multiple_of(c * chunk, chunk), chunk)
        i_row = gt_ref[0, 0, c, 0:1, :]
        f_row = gt_ref[0, 0, c, 1:2, :]
        h, c_st, n_row, m = _mlstm_chunk(q_ref[rows, :], k_ref[rows, :], v_ref[rows, :],
                                         i_row, f_row, c_st, n_row, m, tri, eye)
        h_ref[rows, :] = _rms(h, gh).astype(h_ref.dtype)
        return c_st, n_row, m

    if n_chunks == 1:
        c_st, n_row, m = step(0, init)
    else:
        c_st, n_row, m = lax.fori_loop(0, n_chunks, step, init)
    c_ref[0, 0] = c_st
    n_ref[0, 0] = n_row
    m_ref[0, 0] = jnp.broadcast_to(m, (1, LANE))


def _mlstm(qa, ka, va, gates_t, g_h, init, n_batch, t_len, chunk):
    has_init = init is not None
    n_chunks = t_len // chunk
    st4 = lambda n: pl.BlockSpec((1, 1, 1, n), lambda b, h: (b, h, 0, 0))
    c_spec = pl.BlockSpec((1, 1, DV_A, DK_A), lambda b, h: (b, h, 0, 0))
    in_specs = [pl.BlockSpec((t_len, DK_A), lambda b, h: (b, h)),
                pl.BlockSpec((t_len, DK_A), lambda b, h: (b, h)),
                pl.BlockSpec((t_len, DV_A), lambda b, h: (b, h)),
                pl.BlockSpec((1, 1, n_chunks, 2, chunk), lambda b, h: (b, h, 0, 0, 0)),
                pl.BlockSpec((1, DV_A), lambda b, h: (0, h))]
    args = [qa, ka, va, gates_t, g_h]
    if has_init:
        in_specs += [c_spec, st4(DK_A), st4(LANE)]
        args += list(init)
    return pl.pallas_call(
        functools.partial(_mlstm_kernel, chunk=chunk, n_chunks=n_chunks, has_init=has_init),
        grid=(n_batch, H_A),
        in_specs=in_specs,
        out_specs=[pl.BlockSpec((t_len, DV_A), lambda b, h: (b, h)), c_spec, st4(DK_A), st4(LANE)],
        out_shape=[jax.ShapeDtypeStruct((n_batch * t_len, D_MODEL), BF16),
                   jax.ShapeDtypeStruct((n_batch, H_A, DV_A, DK_A), F32),
                   jax.ShapeDtypeStruct((n_batch, H_A, 1, DK_A), F32),
                   jax.ShapeDtypeStruct((n_batch, H_A, 1, LANE), F32)],
        compiler_params=_params("parallel", "parallel"),
        name="mlstm_init" if has_init else "mlstm",
    )(*args)


def _attn_kernel(q_ref, k_ref, o_ref):
    tq = q_ref.shape[0]
    qi = pl.program_id(1)
    q = jnp.concatenate([q_ref[:, hh * QK_PAD:(hh + 1) * QK_PAD] for hh in range(H_B)], axis=0)
    rows = H_B * tq

    def block(j, carry, masked):
        m, l, acc = carry
        kj = k_ref[pl.ds(pl.multiple_of(j * tq, tq), tq), :]
        s = _dot_nt(q, kj) * SM_SCALE
        if masked:
            r_idx = lax.broadcasted_iota(jnp.int32, (H_B, tq, tq), 1).reshape(rows, tq)
            c_idx = lax.broadcasted_iota(jnp.int32, (rows, tq), 1)
            s = jnp.where(c_idx <= r_idx, s, -jnp.inf)
        m_new = jnp.maximum(m, jnp.max(s, axis=1, keepdims=True))
        p = jnp.exp(s - m_new)
        alpha = jnp.exp(m - m_new)
        l = alpha * l + jnp.sum(p, axis=1, keepdims=True)
        acc = alpha * acc + _dot(p.astype(BF16), kj[:, :KV_LORA])
        return m_new, l, acc

    init = (jnp.full((rows, 1), -jnp.inf, F32), jnp.zeros((rows, 1), F32), jnp.zeros((rows, KV_LORA), F32))
    carry = lax.fori_loop(0, qi, lambda j, c: block(j, c, False), init)
    _, l, acc = block(qi, carry, True)
    o = acc / l
    for hh in range(H_B):
        o_ref[:, hh * KV_LORA:(hh + 1) * KV_LORA] = o[hh * tq:(hh + 1) * tq].astype(o_ref.dtype)


def _attn_prompt(qcat, kcat, n_batch, t_len):
    tq = min(ATTN_TILE, t_len)
    nq = t_len // tq
    return pl.pallas_call(
        _attn_kernel,
        grid=(n_batch, nq),
        in_specs=[pl.BlockSpec((tq, H_B * QK_PAD), lambda b, i: (b * nq + i, 0)),
                  pl.BlockSpec((t_len, QK_PAD), lambda b, i: (b, 0))],
        out_specs=pl.BlockSpec((tq, H_B * KV_LORA), lambda b, i: (b * nq + i, 0)),
        out_shape=jax.ShapeDtypeStruct((n_batch * t_len, H_B * KV_LORA), BF16),
        compiler_params=_params("parallel", "parallel"),
        name="attn_prompt",
    )(qcat, kcat)


def _attn_sample_kernel(pt_ref, q_ref, kvn_ref, krn_ref, kv_hbm, kr_hbm, o_ref,
                        kv_buf, kr_buf, sem, m_ref, l_ref, acc_ref, *, n_steps, t_new):
    b = pl.program_id(0)
    c = pl.program_id(1)
    n_b = pl.num_programs(0)
    g = b * n_steps + c
    slot = g % 2

    def page_copies(bb, cc, sl):
        copies = []
        for p in range(PAGES_PER_STEP):
            page = pt_ref[bb, cc * PAGES_PER_STEP + p]
            copies.append(pltpu.make_async_copy(kv_hbm.at[page], kv_buf.at[sl, p], sem.at[sl, 0]))
            copies.append(pltpu.make_async_copy(kr_hbm.at[page], kr_buf.at[sl, p], sem.at[sl, 1]))
        return copies

    @pl.when(g == 0)
    def _():
        for cp in page_copies(b, c, slot):
            cp.start()

    @pl.when(g + 1 < n_b * n_steps)
    def _():
        nxt = g + 1
        for cp in page_copies(nxt // n_steps, nxt % n_steps, 1 - slot):
            cp.start()

    @pl.when(c == 0)
    def _():
        m_ref[...] = jnp.full(m_ref.shape, -jnp.inf, F32)
        l_ref[...] = jnp.zeros(l_ref.shape, F32)
        acc_ref[...] = jnp.zeros(acc_ref.shape, F32)

    q = q_ref[0]
    q_lat = q[:, :KV_LORA]
    q_rope = q[:, KV_LORA:KV_LORA + ROPE_DIM]

    def update(s, v):
        m = m_ref[...]
        m_new = jnp.maximum(m, jnp.max(s, axis=1, keepdims=True))
        p = jnp.exp(s - m_new)
        alpha = jnp.exp(m - m_new)
        m_ref[...] = m_new
        l_ref[...] = alpha * l_ref[...] + jnp.sum(p, axis=1, keepdims=True)
        acc_ref[...] = alpha * acc_ref[...] + _dot(p.astype(BF16), v)

    for cp in page_copies(b, c, slot):
        cp.wait()
    n_rows = PAGES_PER_STEP * PAGE_SIZE
    kv = kv_buf[slot].reshape(n_rows, KV_LORA).astype(BF16)
    kr = kr_buf[slot].reshape(n_rows, ROPE_DIM).astype(BF16)
    update((_dot_nt(q_lat, kv) + _dot_nt(q_rope, kr)) * SM_SCALE, kv)

    @pl.when(c == n_steps - 1)
    def _():
        kvn = kvn_ref[0].astype(BF16)
        krn = krn_ref[0].astype(BF16)
        s = (_dot_nt(q_lat, kvn) + _dot_nt(q_rope, krn)) * SM_SCALE
        r_tok = lax.broadcasted_iota(jnp.int32, s.shape, 0) % t_new
        c_tok = lax.broadcasted_iota(jnp.int32, s.shape, 1)
        update(jnp.where(c_tok <= r_tok, s, -jnp.inf), kvn)
        o_ref[0] = (acc_ref[...] / l_ref[...]).astype(o_ref.dtype)


def _attn_sample(page_table, q, kv_new, kr_new, cache_kv, cache_kr, t_new):
    n_b, n_pages = page_table.shape
    n_steps = n_pages // PAGES_PER_STEP
    rows = q.shape[1]
    pad = kv_new.shape[1]
    grid_spec = pltpu.PrefetchScalarGridSpec(
        num_scalar_prefetch=1,
        grid=(n_b, n_steps),
        in_specs=[pl.BlockSpec((1, rows, QK_PAD), lambda b, c, pt: (b, 0, 0)),
                  pl.BlockSpec((1, pad, KV_LORA), lambda b, c, pt: (b, 0, 0)),
                  pl.BlockSpec((1, pad, ROPE_DIM), lambda b, c, pt: (b, 0, 0)),
                  pl.BlockSpec(memory_space=pl.ANY),
                  pl.BlockSpec(memory_space=pl.ANY)],
        out_specs=pl.BlockSpec((1, rows, KV_LORA), lambda b, c, pt: (b, 0, 0)),
        scratch_shapes=[pltpu.VMEM((2, PAGES_PER_STEP, PAGE_SIZE, KV_LORA), F32),
                        pltpu.VMEM((2, PAGES_PER_STEP, PAGE_SIZE, ROPE_DIM), F32),
                        pltpu.SemaphoreType.DMA((2, 2)),
                        pltpu.VMEM((rows, 1), F32), pltpu.VMEM((rows, 1), F32),
                        pltpu.VMEM((rows, KV_LORA), F32)])
    return pl.pallas_call(
        functools.partial(_attn_sample_kernel, n_steps=n_steps, t_new=t_new),
        grid_spec=grid_spec,
        out_shape=jax.ShapeDtypeStruct((n_b, rows, KV_LORA), BF16),
        compiler_params=_params("arbitrary", "arbitrary"),
        name="attn_sample",
    )(page_table, q, kv_new, kr_new, cache_kv, cache_kr)


def _out_kernel(x_ref, ha_ref, ga_ref, gb_ref, ol_ref, wuv_ref, wout_ref, gpost_ref, o_ref):
    merged = []
    for hh in range(H_B):
        cols = slice(hh * V_DIM, (hh + 1) * V_DIM)
        y_b = _dot(ol_ref[:, cols], wuv_ref[hh])
        y = ga_ref[:, cols].astype(F32) * ha_ref[:, cols].astype(F32) + gb_ref[:, cols].astype(F32) * y_b
        merged.append(y.astype(BF16))
    mix = _dot(jnp.concatenate(merged, axis=1), wout_ref[...])
    o_ref[...] = x_ref[...] + _rms(mix, gpost_ref[...])


def _out(x, ha, ga, gb, o_lat, w):
    m = x.shape[0]
    tm = min(ROW_TILE, m)
    row = pl.BlockSpec((tm, D_MODEL), lambda i: (i, 0))
    return pl.pallas_call(
        _out_kernel,
        grid=(m // tm,),
        in_specs=[row, row, row, row, row, _resident((H_B, KV_LORA, V_DIM)),
                  _resident((D_MODEL, D_MODEL)), _resident((1, D_MODEL))],
        out_specs=row,
        out_shape=jax.ShapeDtypeStruct((m, D_MODEL), F32),
        compiler_params=_params("parallel"),
        name="out_proj",
    )(x, ha, ga, gb, o_lat, w["w_uv"], w["w_out"], w["g_mix_post"])


def _prep_weights(norm_ffn1_pre, norm_ffn1_post, w_ffn1_gu, w_ffn1_down, norm_mix_pre, norm_mix_post,
                  w_in, b_gates, w_uq, norm_q_lat, norm_kv_lat, w_uk, w_uv, norm_mlstm_h, w_out,
                  norm_ffn2_pre, norm_ffn2_post, w_ffn2_gu, w_ffn2_down):
    half = ROPE_DIM // 2
    row = lambda g: g.reshape(1, -1).astype(F32)

    def swap_pad(wr):
        z = jnp.zeros(wr.shape[:-1] + (LANE - ROPE_DIM,), wr.dtype)
        return (jnp.concatenate([wr, z], axis=-1),
                jnp.concatenate([wr[..., half:], wr[..., :half], z], axis=-1))

    sizes = (H_A * DK_A, H_A * DK_A, H_A * DV_A, H_A, H_A, H_A * DV_A, Q_LORA, KV_LORA, ROPE_DIM, D_MODEL, D_MODEL)
    offs = [0]
    for n in sizes:
        offs.append(offs[-1] + n)
    qa, ka, va, ip, fp, oa, cq, ckv, kr, ga, gb = [w_in[:, offs[i]:offs[i + 1]] for i in range(len(sizes))]
    kr_p, kr_s = swap_pad(kr)
    gates = jnp.concatenate([ip, fp, jnp.zeros((D_MODEL, LANE - 2 * H_A), w_in.dtype)], axis=1)
    w_in2 = jnp.concatenate([qa, ka, va, oa, ga, gb, cq, ckv, kr_p, kr_s, gates], axis=1).astype(BF16)
    assert w_in2.shape[1] == IN_COLS_PAD

    uq = w_uq.reshape(Q_LORA, H_B, NOPE_DIM + ROPE_DIM)
    uq_p, uq_s = swap_pad(uq[..., NOPE_DIM:])
    w_uq2 = jnp.concatenate([uq[..., :NOPE_DIM].reshape(Q_LORA, -1), uq_p.reshape(Q_LORA, -1),
                             uq_s.reshape(Q_LORA, -1)], axis=1).astype(BF16)
    return dict(
        g_ffn1_pre=row(norm_ffn1_pre), g_ffn1_post=row(norm_ffn1_post),
        w_ffn1_gu=w_ffn1_gu.astype(BF16), w_ffn1_down=w_ffn1_down.astype(BF16),
        g_ffn2_pre=row(norm_ffn2_pre), g_ffn2_post=row(norm_ffn2_post),
        w_ffn2_gu=w_ffn2_gu.astype(BF16), w_ffn2_down=w_ffn2_down.astype(BF16),
        g_mix_pre=row(norm_mix_pre), g_mix_post=row(norm_mix_post),
        w_in=w_in2, w_uq=w_uq2,
        b_gates=jnp.concatenate([b_gates.astype(F32), jnp.zeros((LANE - 2 * H_A,), F32)]).reshape(1, LANE),
        g_q=row(norm_q_lat), g_kv=row(norm_kv_lat),
        w_uk=jnp.transpose(w_uk, (1, 2, 0)).astype(BF16),
        w_uv=jnp.transpose(w_uv, (1, 0, 2)).astype(BF16),
        g_h=row(norm_mlstm_h), w_out=w_out.astype(BF16))


def _rope_tables(pos):
    half = ROPE_DIM // 2
    inv_freq = ROPE_THETA ** (-jnp.arange(half, dtype=F32) / half)
    ang = pos.astype(F32)[:, None] * inv_freq[None, :]
    cos, sin = jnp.cos(ang), jnp.sin(ang)
    z = jnp.zeros((pos.shape[0], LANE - ROPE_DIM), F32)
    return jnp.concatenate([cos, cos, z], axis=1), jnp.concatenate([-sin, sin, z], axis=1)


def _gates_by_head(gt, n_batch, t_len, chunk):
    g = gt[:, :2 * H_A].reshape(n_batch, t_len // chunk, chunk, 2, H_A)
    return jnp.transpose(g, (0, 4, 1, 3, 2))


def kernel(x_prompt, x_sample, cache_kv_latent, cache_k_rope, state_mlstm_C, state_mlstm_n, state_mlstm_m,
           page_table, norm_ffn1_pre, norm_ffn1_post, w_ffn1_gu, w_ffn1_down, norm_mix_pre, norm_mix_post,
           w_in, b_gates, w_uq, norm_q_lat, norm_kv_lat, w_uk, w_uv, norm_mlstm_h, w_out,
           norm_ffn2_pre, norm_ffn2_post, w_ffn2_gu, w_ffn2_down):
    assert w_in.shape[0] == 1, "single-layer trunk"
    b_p, t_p, _ = x_prompt.shape
    b_s, t_s, _ = x_sample.shape
    past_len = page_table.shape[1] * PAGE_SIZE
    w = _prep_weights(norm_ffn1_pre[0], norm_ffn1_post[0], w_ffn1_gu[0], w_ffn1_down[0], norm_mix_pre[0],
                      norm_mix_post[0], w_in[0], b_gates[0], w_uq[0], norm_q_lat[0], norm_kv_lat[0], w_uk[0],
                      w_uv[0], norm_mlstm_h[0], w_out[0], norm_ffn2_pre[0], norm_ffn2_post[0], w_ffn2_gu[0],
                      w_ffn2_down[0])

    xp = x_prompt.reshape(b_p * t_p, D_MODEL)
    xp = _ffn(xp, w["g_ffn1_pre"], w["g_ffn1_post"], w["w_ffn1_gu"], w["w_ffn1_down"])
    cos_p, sin_p = _rope_tables(jnp.arange(t_p, dtype=jnp.int32))
    qa, ka, va, ga, gb, gt, ckv_p, kr_p, kcat, qcat = _proj(xp, w, cos_p, sin_p)
    chunk = min(MLSTM_CHUNK, t_p)
    ha, c_p, n_p, m_p = _mlstm(qa, ka, va, _gates_by_head(gt, b_p, t_p, chunk), w["g_h"], None, b_p, t_p, chunk)
    o_lat = _attn_prompt(qcat, kcat, b_p, t_p)
    xp = _out(xp, ha, ga, gb, o_lat, w)
    yp = _ffn(xp, w["g_ffn2_pre"], w["g_ffn2_post"], w["w_ffn2_gu"], w["w_ffn2_down"])

    t_pad = 16
    new_pad = 8
    xs = x_sample.reshape(b_s * t_s, D_MODEL)
    xs = _ffn(xs, w["g_ffn1_pre"], w["g_ffn1_post"], w["w_ffn1_gu"], w["w_ffn1_down"])
    cos_s, sin_s = _rope_tables(past_len + jnp.arange(t_s, dtype=jnp.int32))
    reps = min(ROW_TILE, b_s * t_s) // t_s
    qa, ka, va, ga, gb, gt, ckv_s, kr_s, _, qcat = _proj(xs, w, jnp.tile(cos_s, (reps, 1)), jnp.tile(sin_s, (reps, 1)))

    def pad_t(a, value=0.0):
        a = a.reshape(b_s, t_s, a.shape[-1])
        a = jnp.pad(a, ((0, 0), (0, t_pad - t_s), (0, 0)), constant_values=value)
        return a.reshape(b_s * t_pad, a.shape[-1])

    tok = jnp.arange(b_s * t_pad) % t_pad
    lane = jnp.arange(LANE)
    gt_pad = jnp.where((tok[:, None] >= t_s) & (lane[None, :] < H_A), NEG_BIG, pad_t(gt))
    m0 = jnp.broadcast_to(state_mlstm_m[0][:, :, None, None], (b_s, H_A, 1, LANE))
    ha, c_s, n_s, m_s = _mlstm(pad_t(qa), pad_t(ka), pad_t(va), _gates_by_head(gt_pad, b_s, t_pad, t_pad), w["g_h"],
                               (state_mlstm_C[0], state_mlstm_n[0][:, :, None, :], m0), b_s, t_pad, t_pad)
    ha = ha.reshape(b_s, t_pad, D_MODEL)[:, :t_s].reshape(b_s * t_s, D_MODEL)

    q_s = qcat.reshape(b_s, t_s, H_B, QK_PAD).transpose(0, 2, 1, 3).reshape(b_s, H_B * t_s, QK_PAD)
    pad_new = lambda a: jnp.pad(a.reshape(b_s, t_s, -1), ((0, 0), (0, new_pad - t_s), (0, 0)))
    o_s = _attn_sample(page_table, q_s, pad_new(ckv_s), pad_new(kr_s), cache_kv_latent[0], cache_k_rope[0], t_s)
    o_s = o_s.reshape(b_s, H_B, t_s, KV_LORA).transpose(0, 2, 1, 3).reshape(b_s * t_s, H_B * KV_LORA)
    xs = _out(xs, ha, ga, gb, o_s, w)
    ys = _ffn(xs, w["g_ffn2_pre"], w["g_ffn2_post"], w["w_ffn2_gu"], w["w_ffn2_down"])

    return (yp.reshape(b_p, t_p, D_MODEL), ys.reshape(b_s, t_s, D_MODEL),
            ckv_p.reshape(1, b_p, t_p, KV_LORA), kr_p.reshape(1, b_p, t_p, ROPE_DIM),
            c_p[None], n_p[:, :, 0, :][None], m_p[:, :, 0, 0][None],
            ckv_s.reshape(1, b_s, t_s, KV_LORA), kr_s.reshape(1, b_s, t_s, ROPE_DIM),
            c_s[None], n_s[:, :, 0, :][None], m_s[:, :, 0, 0][None])
```

```python
import functools
import math

import jax
import jax.numpy as jnp
from jax import lax
from jax.experimental import pallas as pl
from jax.experimental.pallas import tpu as pltpu

F32 = jnp.float32
BF16 = jnp.bfloat16

D_MODEL = 1024
H_A = 4
DV_A = D_MODEL // H_A
DK_A = DV_A // 2
V_DIM = 128
H_B = D_MODEL // V_DIM
NOPE_DIM = 128
ROPE_DIM = 64
Q_LORA = 256
KV_LORA = 128
ROPE_THETA = 10000.0
SM_SCALE = (NOPE_DIM + ROPE_DIM) ** -0.5
Q_SCALE = SM_SCALE * math.log2(math.e)
D_FF = 2816
EPS = 1e-6
PAGE_SIZE = 128
QK_PAD = 256

LANE = 128
VMEM_LIMIT = 56 * 1024 * 1024
ROW_TILE = 512
FF_CHUNK = 256
MLSTM_CHUNK = 256
MLSTM_SAMPLE_SEQS = 2
ATTN_TILE = 256
PAGES_PER_STEP = 64
SAMPLE_SUB_BLOCKS = 4
NEG_BIG = -1e30


def _params(*sem):
    return pltpu.CompilerParams(dimension_semantics=sem, vmem_limit_bytes=VMEM_LIMIT)


def _resident(shape):
    return pl.BlockSpec(shape, lambda *_: (0,) * len(shape), pipeline_mode=pl.Buffered(1))


def _rms(x, g):
    return x * lax.rsqrt(jnp.mean(x * x, axis=-1, keepdims=True) + EPS) * g


def _dot(a, b):
    return jnp.dot(a, b, preferred_element_type=F32)


def _dot_nt(a, b):
    return lax.dot_general(a, b, (((1,), (1,)), ((), ())), preferred_element_type=F32)


def _dot_tn(a, b):
    return lax.dot_general(a, b, (((0,), (0,)), ((), ())), preferred_element_type=F32)


def _ffn_body(x, gpre, gpost, wgu_ref, wdn_ref, act_ref):
    h = _rms(x, gpre).astype(BF16)
    for lo in range(0, D_FF, FF_CHUNK):
        g = _dot(h, wgu_ref[:, lo:lo + FF_CHUNK])
        u = _dot(h, wgu_ref[:, D_FF + lo:D_FF + lo + FF_CHUNK])
        act_ref[:, lo:lo + FF_CHUNK] = (g * jax.nn.sigmoid(g) * u).astype(BF16)
    f = _dot(act_ref[...], wdn_ref[...])
    return x + 0.5 * _rms(f, gpost)


def _ffn_kernel(x_ref, gpre_ref, gpost_ref, wgu_ref, wdn_ref, o_ref, act_ref):
    o_ref[...] = _ffn_body(x_ref[...], gpre_ref[...], gpost_ref[...], wgu_ref, wdn_ref, act_ref)


def _ffn(x, gpre, gpost, wgu, wdn):
    m = x.shape[0]
    tm = min(ROW_TILE, m)
    row = pl.BlockSpec((tm, D_MODEL), lambda i: (i, 0))
    return pl.pallas_call(
        _ffn_kernel,
        grid=(m // tm,),
        in_specs=[row, _resident((1, D_MODEL)), _resident((1, D_MODEL)),
                  _resident((D_MODEL, 2 * D_FF)), _resident((D_FF, D_MODEL))],
        out_specs=row,
        out_shape=jax.ShapeDtypeStruct((m, D_MODEL), F32),
        scratch_shapes=[pltpu.VMEM((tm, D_FF), BF16)],
        compiler_params=_params("parallel"),
        name="ffn",
    )(x, gpre, gpost, wgu, wdn)


_O_QA, _O_KA, _O_VA, _O_OA, _O_GA, _O_GB = 0, 512, 1024, 2048, 3072, 4096
_O_CQ, _O_CKV, _O_KR, _O_KRS, _O_GT = 5120, 5376, 5504, 5632, 5760
IN_COLS_PAD = 5888


def _proj_kernel(x_ref, gpre_ref, win_ref, bg_ref, gq_ref, gkv_ref, wuq_ref, wuk_ref, cos_ref, sin_ref,
                 qa_ref, ka_ref, va_ref, ga_ref, gb_ref, gt_ref, ckv_ref, kr_ref, kcat_ref, qcat_ref):
    h = _rms(x_ref[...], gpre_ref[...]).astype(BF16)

    def seg(lo, n):
        return _dot(h, win_ref[:, lo:lo + n])

    qa_ref[...] = seg(_O_QA, H_A * DK_A).astype(BF16)
    ka_ref[...] = (seg(_O_KA, H_A * DK_A) * (DK_A ** -0.5)).astype(BF16)
    va_ref[...] = seg(_O_VA, D_MODEL).astype(BF16)
    ga_ref[...] = (jax.nn.sigmoid(seg(_O_GA, D_MODEL)) * jax.nn.sigmoid(seg(_O_OA, D_MODEL))).astype(BF16)
    gb_ref[...] = jax.nn.sigmoid(seg(_O_GB, D_MODEL)).astype(BF16)

    gt = seg(_O_GT, LANE) + bg_ref[...]
    logsig = jnp.minimum(gt, 0.0) - jnp.log1p(jnp.exp(-jnp.abs(gt)))
    lane = lax.broadcasted_iota(jnp.int32, gt.shape, 1)
    gt_ref[...] = jnp.where(lane < H_A, gt, logsig)

    cos = cos_ref[...]
    sin = sin_ref[...]
    ckv = _rms(seg(_O_CKV, KV_LORA), gkv_ref[...])
    kr = seg(_O_KR, LANE) * cos + seg(_O_KRS, LANE) * sin
    ckv_ref[...] = ckv
    kr_ref[...] = kr[:, :ROPE_DIM]
    kcat_ref[:, :KV_LORA] = ckv.astype(BF16)
    kcat_ref[:, KV_LORA:] = kr.astype(BF16)

    cq = _rms(seg(_O_CQ, Q_LORA), gq_ref[...]).astype(BF16)
    q = _dot(cq, wuq_ref[...])
    for hh in range(H_B):
        q_nope = q[:, hh * LANE:(hh + 1) * LANE].astype(BF16)
        q_lat = _dot(q_nope, wuk_ref[hh])
        lo = H_B * LANE + hh * LANE
        q_rope = q[:, lo:lo + LANE] * cos + q[:, lo + H_B * LANE:lo + (H_B + 1) * LANE] * sin
        qcat_ref[:, hh * QK_PAD:hh * QK_PAD + LANE] = (q_lat * Q_SCALE).astype(BF16)
        qcat_ref[:, hh * QK_PAD + LANE:(hh + 1) * QK_PAD] = (q_rope * Q_SCALE).astype(BF16)


def _proj(x, w, cos, sin):
    m = x.shape[0]
    tm = min(ROW_TILE, m)
    ntab = cos.shape[0] // tm

    def row(n):
        return pl.BlockSpec((tm, n), lambda i: (i, 0))

    tab = pl.BlockSpec((tm, LANE), lambda i: (i % ntab, 0))
    outs = [(H_A * DK_A, BF16), (H_A * DK_A, BF16), (D_MODEL, BF16), (D_MODEL, BF16), (D_MODEL, BF16),
            (LANE, F32), (KV_LORA, F32), (ROPE_DIM, F32), (QK_PAD, BF16), (H_B * QK_PAD, BF16)]
    return pl.pallas_call(
        _proj_kernel,
        grid=(m // tm,),
        in_specs=[row(D_MODEL), _resident((1, D_MODEL)), _resident((D_MODEL, IN_COLS_PAD)),
                  _resident((1, LANE)), _resident((1, Q_LORA)), _resident((1, KV_LORA)),
                  _resident((Q_LORA, 3 * H_B * LANE)), _resident((H_B, NOPE_DIM, KV_LORA)), tab, tab],
        out_specs=[row(n) for n, _ in outs],
        out_shape=[jax.ShapeDtypeStruct((m, n), dt) for n, dt in outs],
        compiler_params=_params("parallel"),
        name="proj",
    )(x, w["g_mix_pre"], w["w_in"], w["b_gates"], w["g_q"], w["g_kv"], w["w_uq"], w["w_uk"], cos, sin)


def _mlstm_chunk(q, k, v, i_row, f_row, c_st, n_row, m, tri, eye):
    ln = q.shape[0]
    b_col = jnp.sum(jnp.where(tri, f_row, 0.0), axis=1, keepdims=True)
    b_row = jnp.sum(jnp.where(eye, b_col, 0.0), axis=0, keepdims=True)
    d = jnp.where(tri, b_col - b_row + i_row, -jnp.inf)
    m_inter = b_col + m
    m_t = jnp.maximum(m_inter, jnp.max(d, axis=1, keepdims=True))
    wgt = jnp.exp(d - m_t)
    inter = jnp.exp(m_inter - m_t)
    s = _dot_nt(q, k) * wgt
    num = _dot(s.astype(BF16), v) + inter * _dot_nt(q, c_st.astype(BF16))
    qn = jnp.sum(q.astype(F32) * n_row, axis=1, keepdims=True)
    den = jnp.sum(s, axis=1, keepdims=True) + inter * qn
    h = num / jnp.maximum(jnp.abs(den), jnp.exp(-m_t))

    b_last = b_row[:, ln - 1:ln]
    g_row = b_last - b_row + i_row
    m_new = jnp.maximum(b_last + m, jnp.max(g_row, axis=1, keepdims=True))
    a_row = jnp.exp(g_row - m_new)
    decay = jnp.exp(b_last + m - m_new)
    a_col = jnp.sum(jnp.where(eye, a_row, 0.0), axis=1, keepdims=True)
    va = (v.astype(F32) * a_col).astype(BF16)
    c_new = decay * c_st + _dot_tn(va, k)
    n_new = decay * n_row + jnp.sum(a_col * k.astype(F32), axis=0, keepdims=True)
    return h, c_new, n_new, m_new


def _mlstm_kernel(*refs, chunk, n_chunks, n_seq, has_init):
    if has_init:
        q_ref, k_ref, v_ref, gt_ref, gh_ref, c0_ref, n0_ref, m0_ref, h_ref, c_ref, n_ref, m_ref = refs
        c_ref[...] = c0_ref[...]
        n_ref[...] = n0_ref[...]
        m_ref[...] = m0_ref[...]
    else:
        q_ref, k_ref, v_ref, gt_ref, gh_ref, h_ref, c_ref, n_ref, m_ref = refs
        c_ref[...] = jnp.zeros(c_ref.shape, F32)
        n_ref[...] = jnp.zeros(n_ref.shape, F32)
        m_ref[...] = jnp.zeros(m_ref.shape, F32)
    t_idx = lax.broadcasted_iota(jnp.int32, (chunk, chunk), 0)
    s_idx = lax.broadcasted_iota(jnp.int32, (chunk, chunk), 1)
    tri = s_idx <= t_idx
    eye = s_idx == t_idx
    t_len = chunk * n_chunks

    def step(c, carry):
        for b in range(n_seq):
            start = b * t_len + c * chunk
            rows = pl.ds(start if isinstance(start, int) else pl.multiple_of(start, chunk), chunk)
            for hh in range(H_A):
                qk_cols = slice(hh * DK_A, (hh + 1) * DK_A)
                v_cols = slice(hh * DV_A, (hh + 1) * DV_A)
                h, c_st, n_row, m = _mlstm_chunk(
                    q_ref[rows, qk_cols], k_ref[rows, qk_cols], v_ref[rows, v_cols],
                    gt_ref[b, hh, c, 0:1, :], gt_ref[b, hh, c, 1:2, :],
                    c_ref[b, hh], n_ref[b, hh], m_ref[b, hh][:, :1], tri, eye)
                h_ref[rows, v_cols] = _rms(h, gh_ref[:, v_cols]).astype(h_ref.dtype)
                c_ref[b, hh] = c_st
                n_ref[b, hh] = n_row
                m_ref[b, hh] = jnp.broadcast_to(m, (1, LANE))
        return carry

    if n_chunks == 1:
        step(0, 0)
    else:
        lax.fori_loop(0, n_chunks, step, 0)


def _mlstm(qa, ka, va, gates_t, g_h, init, n_batch, t_len, chunk, n_seq):
    has_init = init is not None
    n_chunks = t_len // chunk
    st4 = lambda n: pl.BlockSpec((n_seq, H_A, 1, n), lambda b: (b, 0, 0, 0))
    c_spec = pl.BlockSpec((n_seq, H_A, DV_A, DK_A), lambda b: (b, 0, 0, 0))
    row = lambda n: pl.BlockSpec((n_seq * t_len, n), lambda b: (b, 0))
    in_specs = [row(H_A * DK_A), row(H_A * DK_A), row(D_MODEL),
                pl.BlockSpec((n_seq, H_A, n_chunks, 2, chunk), lambda b: (b, 0, 0, 0, 0)),
                _resident((1, D_MODEL))]
    args = [qa, ka, va, gates_t, g_h]
    if has_init:
        in_specs += [c_spec, st4(DK_A), st4(LANE)]
        args += list(init)
    return pl.pallas_call(
        functools.partial(_mlstm_kernel, chunk=chunk, n_chunks=n_chunks, n_seq=n_seq, has_init=has_init),
        grid=(n_batch // n_seq,),
        in_specs=in_specs,
        out_specs=[row(D_MODEL), c_spec, st4(DK_A), st4(LANE)],
        out_shape=[jax.ShapeDtypeStruct((n_batch * t_len, D_MODEL), BF16),
                   jax.ShapeDtypeStruct((n_batch, H_A, DV_A, DK_A), F32),
                   jax.ShapeDtypeStruct((n_batch, H_A, 1, DK_A), F32),
                   jax.ShapeDtypeStruct((n_batch, H_A, 1, LANE), F32)],
        compiler_params=_params("parallel"),
        name="mlstm_init" if has_init else "mlstm",
    )(*args)


def _attn_kernel(q_ref, k_ref, o_ref, s_ref, p_ref):
    tq = q_ref.shape[0]
    n_q = k_ref.shape[0] // tq
    qi = pl.program_id(1)
    rows = H_B * tq
    q = jnp.concatenate([q_ref[:, hh * QK_PAD:(hh + 1) * QK_PAD] for hh in range(H_B)], axis=0)
    r_idx = lax.broadcasted_iota(jnp.int32, (H_B, tq, tq), 1).reshape(rows, tq)
    c_idx = lax.broadcasted_iota(jnp.int32, (rows, tq), 1)

    def with_ones(v):
        return jnp.concatenate([v, jnp.ones(v.shape, BF16)], axis=1)

    for i in range(n_q):
        @pl.when(qi == i)
        def _(i=i):
            past = i * tq
            k_diag = k_ref[past:past + tq, :]
            s_diag = jnp.where(c_idx <= r_idx, _dot_nt(q, k_diag), -jnp.inf)
            m = jnp.max(s_diag, axis=1, keepdims=True)
            if past:
                s_ref[:, :past] = _dot_nt(q, k_ref[:past, :])
                m = jnp.maximum(m, jnp.max(s_ref[:, :past], axis=1, keepdims=True))
                p_ref[:, :past] = jnp.exp2(s_ref[:, :past] - m).astype(BF16)
            o = _dot(jnp.exp2(s_diag - m).astype(BF16), with_ones(k_diag[:, :KV_LORA]))
            if past:
                o = o + _dot(p_ref[:, :past], with_ones(k_ref[:past, :KV_LORA]))
            o = o[:, :KV_LORA] / o[:, KV_LORA:]
            for hh in range(H_B):
                o_ref[:, hh * KV_LORA:(hh + 1) * KV_LORA] = o[hh * tq:(hh + 1) * tq].astype(o_ref.dtype)


def _attn_prompt(qcat, kcat, n_batch, t_len):
    tq = min(ATTN_TILE, t_len)
    nq = t_len // tq
    past_max = max(t_len - tq, LANE)
    return pl.pallas_call(
        _attn_kernel,
        grid=(n_batch, nq),
        in_specs=[pl.BlockSpec((tq, H_B * QK_PAD), lambda b, i: (b * nq + i, 0)),
                  pl.BlockSpec((t_len, QK_PAD), lambda b, i: (b, 0))],
        out_specs=pl.BlockSpec((tq, H_B * KV_LORA), lambda b, i: (b * nq + i, 0)),
        out_shape=jax.ShapeDtypeStruct((n_batch * t_len, H_B * KV_LORA), BF16),
        scratch_shapes=[pltpu.VMEM((H_B * tq, past_max), F32), pltpu.VMEM((H_B * tq, past_max), BF16)],
        compiler_params=_params("parallel", "parallel"),
        name="attn_prompt",
    )(qcat, kcat)


def _attn_sample_kernel(pt_ref, q_ref, kvn_ref, krn_ref, kv_hbm, kr_hbm, o_ref,
                        kv_buf, kr_buf, sem, m_ref, l_ref, acc_ref, *, n_steps, t_new):
    b = pl.program_id(0)
    c = pl.program_id(1)
    n_b = pl.num_programs(0)
    g = b * n_steps + c
    slot = g % 2

    def page_copies(bb, cc, sl):
        copies = []
        for p in range(PAGES_PER_STEP):
            page = pt_ref[bb, cc * PAGES_PER_STEP + p]
            span = pl.ds(p * PAGE_SIZE, PAGE_SIZE)
            copies.append(pltpu.make_async_copy(kv_hbm.at[page], kv_buf.at[sl, span, :], sem.at[sl, 0]))
            copies.append(pltpu.make_async_copy(kr_hbm.at[page], kr_buf.at[sl, :, span], sem.at[sl, 1]))
        return copies

    @pl.when(g == 0)
    def _():
        for cp in page_copies(b, c, slot):
            cp.start()

    @pl.when(g + 1 < n_b * n_steps)
    def _():
        nxt = g + 1
        for cp in page_copies(nxt // n_steps, nxt % n_steps, 1 - slot):
            cp.start()

    @pl.when(c == 0)
    def _():
        m_ref[...] = jnp.full(m_ref.shape, -jnp.inf, F32)
        l_ref[...] = jnp.zeros(l_ref.shape, F32)
        acc_ref[...] = jnp.zeros(acc_ref.shape, F32)

    q = q_ref[0]
    q_lat = q[:, :KV_LORA]
    q_rope = q[:, KV_LORA:KV_LORA + ROPE_DIM]

    def partial_softmax(s, v):
        m_blk = jnp.max(s, axis=1, keepdims=True)
        p = jnp.exp2(s - m_blk)
        return m_blk, jnp.sum(p, axis=1, keepdims=True), _dot(p.astype(BF16), v)

    def merge(parts):
        m_old = m_ref[...]
        m_new = m_old
        for m_blk, _, _ in parts:
            m_new = jnp.maximum(m_new, m_blk)
        alpha = jnp.exp2(m_old - m_new)
        l = alpha * l_ref[...]
        acc = alpha * acc_ref[...]
        for m_blk, l_blk, o_blk in parts:
            w_blk = jnp.exp2(m_blk - m_new)
            l = l + w_blk * l_blk
            acc = acc + w_blk * o_blk
        m_ref[...] = m_new
        l_ref[...] = l
        acc_ref[...] = acc

    for cp in page_copies(b, c, slot):
        cp.wait()
    sub = PAGES_PER_STEP * PAGE_SIZE // SAMPLE_SUB_BLOCKS
    parts = []
    for i in range(SAMPLE_SUB_BLOCKS):
        kv = kv_buf[slot, i * sub:(i + 1) * sub, :].astype(BF16)
        kr_t = kr_buf[slot, :, i * sub:(i + 1) * sub].astype(BF16)
        parts.append(partial_softmax(_dot_nt(q_lat, kv) + _dot(q_rope, kr_t), kv))
    merge(parts)

    @pl.when(c == n_steps - 1)
    def _():
        kvn = kvn_ref[0].astype(BF16)
        krn = krn_ref[0].astype(BF16)
        s = _dot_nt(q_lat, kvn) + _dot_nt(q_rope, krn)
        r_tok = lax.broadcasted_iota(jnp.int32, s.shape, 0) % t_new
        c_tok = lax.broadcasted_iota(jnp.int32, s.shape, 1)
        merge([partial_softmax(jnp.where(c_tok <= r_tok, s, -jnp.inf), kvn)])
        o_ref[0] = (acc_ref[...] / l_ref[...]).astype(o_ref.dtype)


def _attn_sample(page_table, q, kv_new, kr_new, cache_kv, cache_kr_t, t_new):
    n_b, n_pages = page_table.shape
    assert n_pages % PAGES_PER_STEP == 0
    step_rows = PAGES_PER_STEP * PAGE_SIZE
    n_steps = n_pages // PAGES_PER_STEP
    rows = q.shape[1]
    pad = kv_new.shape[1]
    grid_spec = pltpu.PrefetchScalarGridSpec(
        num_scalar_prefetch=1,
        grid=(n_b, n_steps),
        in_specs=[pl.BlockSpec((1, rows, QK_PAD), lambda b, c, pt: (b, 0, 0)),
                  pl.BlockSpec((1, pad, KV_LORA), lambda b, c, pt: (b, 0, 0)),
                  pl.BlockSpec((1, pad, ROPE_DIM), lambda b, c, pt: (b, 0, 0)),
                  pl.BlockSpec(memory_space=pl.ANY),
                  pl.BlockSpec(memory_space=pl.ANY)],
        out_specs=pl.BlockSpec((1, rows, KV_LORA), lambda b, c, pt: (b, 0, 0)),
        scratch_shapes=[pltpu.VMEM((2, step_rows, KV_LORA), F32),
                        pltpu.VMEM((2, ROPE_DIM, step_rows), F32),
                        pltpu.SemaphoreType.DMA((2, 2)),
                        pltpu.VMEM((rows, 1), F32), pltpu.VMEM((rows, 1), F32),
                        pltpu.VMEM((rows, KV_LORA), F32)])
    return pl.pallas_call(
        functools.partial(_attn_sample_kernel, n_steps=n_steps, t_new=t_new),
        grid_spec=grid_spec,
        out_shape=jax.ShapeDtypeStruct((n_b, rows, KV_LORA), BF16),
        compiler_params=_params("arbitrary", "arbitrary"),
        name="attn_sample",
    )(page_table, q, kv_new, kr_new, cache_kv, cache_kr_t)


def _out_ffn_kernel(x_ref, ha_ref, ga_ref, gb_ref, ol_ref, wuv_ref, wout_ref, gmix_ref,
                    gpre_ref, gpost_ref, wgu_ref, wdn_ref, o_ref, act_ref):
    merged = []
    for hh in range(H_B):
        cols = slice(hh * V_DIM, (hh + 1) * V_DIM)
        y_b = _dot(ol_ref[:, cols], wuv_ref[hh])
        y = ga_ref[:, cols].astype(F32) * ha_ref[:, cols].astype(F32) + gb_ref[:, cols].astype(F32) * y_b
        merged.append(y.astype(BF16))
    mix = _dot(jnp.concatenate(merged, axis=1), wout_ref[...])
    x = x_ref[...] + _rms(mix, gmix_ref[...])
    o_ref[...] = _ffn_body(x, gpre_ref[...], gpost_ref[...], wgu_ref, wdn_ref, act_ref)


def _out_ffn(x, ha, ga, gb, o_lat, w):
    m = x.shape[0]
    tm = min(ROW_TILE, m)
    row = pl.BlockSpec((tm, D_MODEL), lambda i: (i, 0))
    vec = _resident((1, D_MODEL))
    return pl.pallas_call(
        _out_ffn_kernel,
        grid=(m // tm,),
        in_specs=[row, row, row, row, row, _resident((H_B, KV_LORA, V_DIM)), _resident((D_MODEL, D_MODEL)), vec,
                  vec, vec, _resident((D_MODEL, 2 * D_FF)), _resident((D_FF, D_MODEL))],
        out_specs=row,
        out_shape=jax.ShapeDtypeStruct((m, D_MODEL), F32),
        scratch_shapes=[pltpu.VMEM((tm, D_FF), BF16)],
        compiler_params=_params("parallel"),
        name="out_ffn",
    )(x, ha, ga, gb, o_lat, w["w_uv"], w["w_out"], w["g_mix_post"],
      w["g_ffn2_pre"], w["g_ffn2_post"], w["w_ffn2_gu"], w["w_ffn2_down"])


def _prep_weights(norm_ffn1_pre, norm_ffn1_post, w_ffn1_gu, w_ffn1_down, norm_mix_pre, norm_mix_post,
                  w_in, b_gates, w_uq, norm_q_lat, norm_kv_lat, w_uk, w_uv, norm_mlstm_h, w_out,
                  norm_ffn2_pre, norm_ffn2_post, w_ffn2_gu, w_ffn2_down):
    half = ROPE_DIM // 2
    row = lambda g: g.reshape(1, -1).astype(F32)

    def swap_pad(wr):
        z = jnp.zeros(wr.shape[:-1] + (LANE - ROPE_DIM,), wr.dtype)
        return (jnp.concatenate([wr, z], axis=-1),
                jnp.concatenate([wr[..., half:], wr[..., :half], z], axis=-1))

    sizes = (H_A * DK_A, H_A * DK_A, H_A * DV_A, H_A, H_A, H_A * DV_A, Q_LORA, KV_LORA, ROPE_DIM, D_MODEL, D_MODEL)
    offs = [0]
    for n in sizes:
        offs.append(offs[-1] + n)
    qa, ka, va, ip, fp, oa, cq, ckv, kr, ga, gb = [w_in[:, offs[i]:offs[i + 1]] for i in range(len(sizes))]
    kr_p, kr_s = swap_pad(kr)
    gates = jnp.concatenate([ip, fp, jnp.zeros((D_MODEL, LANE - 2 * H_A), w_in.dtype)], axis=1)
    w_in2 = jnp.concatenate([qa, ka, va, oa, ga, gb, cq, ckv, kr_p, kr_s, gates], axis=1).astype(BF16)
    assert w_in2.shape[1] == IN_COLS_PAD

    uq = w_uq.reshape(Q_LORA, H_B, NOPE_DIM + ROPE_DIM)
    uq_p, uq_s = swap_pad(uq[..., NOPE_DIM:])
    w_uq2 = jnp.concatenate([uq[..., :NOPE_DIM].reshape(Q_LORA, -1), uq_p.reshape(Q_LORA, -1),
                             uq_s.reshape(Q_LORA, -1)], axis=1).astype(BF16)
    return dict(
        g_ffn1_pre=row(norm_ffn1_pre), g_ffn1_post=row(norm_ffn1_post),
        w_ffn1_gu=w_ffn1_gu.astype(BF16), w_ffn1_down=w_ffn1_down.astype(BF16),
        g_ffn2_pre=row(norm_ffn2_pre), g_ffn2_post=row(norm_ffn2_post),
        w_ffn2_gu=w_ffn2_gu.astype(BF16), w_ffn2_down=w_ffn2_down.astype(BF16),
        g_mix_pre=row(norm_mix_pre), g_mix_post=row(norm_mix_post),
        w_in=w_in2, w_uq=w_uq2,
        b_gates=jnp.concatenate([b_gates.astype(F32), jnp.zeros((LANE - 2 * H_A,), F32)]).reshape(1, LANE),
        g_q=row(norm_q_lat), g_kv=row(norm_kv_lat),
        w_uk=jnp.transpose(w_uk, (1, 2, 0)).astype(BF16),
        w_uv=jnp.transpose(w_uv, (1, 0, 2)).astype(BF16),
        g_h=row(norm_mlstm_h), w_out=w_out.astype(BF16))


def _rope_tables(pos):
    half = ROPE_DIM // 2
    inv_freq = ROPE_THETA ** (-jnp.arange(half, dtype=F32) / half)
    ang = pos.astype(F32)[:, None] * inv_freq[None, :]
    cos, sin = jnp.cos(ang), jnp.sin(ang)
    z = jnp.zeros((pos.shape[0], LANE - ROPE_DIM), F32)
    return jnp.concatenate([cos, cos, z], axis=1), jnp.concatenate([-sin, sin, z], axis=1)


def _gates_by_head(gt, n_batch, t_len, chunk):
    g = gt[:, :2 * H_A].reshape(n_batch, t_len // chunk, chunk, 2, H_A)
    return jnp.transpose(g, (0, 4, 1, 3, 2))


def kernel(x_prompt, x_sample, cache_kv_latent, cache_k_rope, state_mlstm_C, state_mlstm_n, state_mlstm_m,
           page_table, norm_ffn1_pre, norm_ffn1_post, w_ffn1_gu, w_ffn1_down, norm_mix_pre, norm_mix_post,
           w_in, b_gates, w_uq, norm_q_lat, norm_kv_lat, w_uk, w_uv, norm_mlstm_h, w_out,
           norm_ffn2_pre, norm_ffn2_post, w_ffn2_gu, w_ffn2_down):
    assert w_in.shape[0] == 1, "single-layer trunk"
    b_p, t_p, _ = x_prompt.shape
    b_s, t_s, _ = x_sample.shape
    past_len = page_table.shape[1] * PAGE_SIZE
    w = _prep_weights(norm_ffn1_pre[0], norm_ffn1_post[0], w_ffn1_gu[0], w_ffn1_down[0], norm_mix_pre[0],
                      norm_mix_post[0], w_in[0], b_gates[0], w_uq[0], norm_q_lat[0], norm_kv_lat[0], w_uk[0],
                      w_uv[0], norm_mlstm_h[0], w_out[0], norm_ffn2_pre[0], norm_ffn2_post[0], w_ffn2_gu[0],
                      w_ffn2_down[0])

    xp = x_prompt.reshape(b_p * t_p, D_MODEL)
    xp = _ffn(xp, w["g_ffn1_pre"], w["g_ffn1_post"], w["w_ffn1_gu"], w["w_ffn1_down"])
    cos_p, sin_p = _rope_tables(jnp.arange(t_p, dtype=jnp.int32))
    qa, ka, va, ga, gb, gt, ckv_p, kr_p, kcat, qcat = _proj(xp, w, cos_p, sin_p)
    chunk = min(MLSTM_CHUNK, t_p)
    ha, c_p, n_p, m_p = _mlstm(qa, ka, va, _gates_by_head(gt, b_p, t_p, chunk), w["g_h"], None, b_p, t_p, chunk, 1)
    o_lat = _attn_prompt(qcat, kcat, b_p, t_p)
    yp = _out_ffn(xp, ha, ga, gb, o_lat, w)

    t_pad = 16
    new_pad = 8
    xs = x_sample.reshape(b_s * t_s, D_MODEL)
    xs = _ffn(xs, w["g_ffn1_pre"], w["g_ffn1_post"], w["w_ffn1_gu"], w["w_ffn1_down"])
    cos_s, sin_s = _rope_tables(past_len + jnp.arange(t_s, dtype=jnp.int32))
    reps = min(ROW_TILE, b_s * t_s) // t_s
    qa, ka, va, ga, gb, gt, ckv_s, kr_s, _, qcat = _proj(xs, w, jnp.tile(cos_s, (reps, 1)), jnp.tile(sin_s, (reps, 1)))

    def pad_t(a, value=0.0):
        a = a.reshape(b_s, t_s, a.shape[-1])
        a = jnp.pad(a, ((0, 0), (0, t_pad - t_s), (0, 0)), constant_values=value)
        return a.reshape(b_s * t_pad, a.shape[-1])

    tok = jnp.arange(b_s * t_pad) % t_pad
    lane = jnp.arange(LANE)
    gt_pad = jnp.where((tok[:, None] >= t_s) & (lane[None, :] < H_A), NEG_BIG, pad_t(gt))
    m0 = jnp.broadcast_to(state_mlstm_m[0][:, :, None, None], (b_s, H_A, 1, LANE))
    ha, c_s, n_s, m_s = _mlstm(pad_t(qa), pad_t(ka), pad_t(va), _gates_by_head(gt_pad, b_s, t_pad, t_pad), w["g_h"],
                               (state_mlstm_C[0], state_mlstm_n[0][:, :, None, :], m0), b_s, t_pad, t_pad,
                               math.gcd(b_s, MLSTM_SAMPLE_SEQS))
    ha = ha.reshape(b_s, t_pad, D_MODEL)[:, :t_s].reshape(b_s * t_s, D_MODEL)

    q_s = qcat.reshape(b_s, t_s, H_B, QK_PAD).transpose(0, 2, 1, 3).reshape(b_s, H_B * t_s, QK_PAD)
    pad_new = lambda a: jnp.pad(a.reshape(b_s, t_s, -1), ((0, 0), (0, new_pad - t_s), (0, 0)))
    o_s = _attn_sample(page_table, q_s, pad_new(ckv_s), pad_new(kr_s), cache_kv_latent[0],
                       jnp.swapaxes(cache_k_rope[0], 1, 2), t_s)
    o_s = o_s.reshape(b_s, H_B, t_s, KV_LORA).transpose(0, 2, 1, 3).reshape(b_s * t_s, H_B * KV_LORA)
    ys = _out_ffn(xs, ha, ga, gb, o_s, w)

    return (yp.reshape(b_p, t_p, D_MODEL), ys.reshape(b_s, t_s, D_MODEL),
            ckv_p.reshape(1, b_p, t_p, KV_LORA), kr_p.reshape(1, b_p, t_p, ROPE_DIM),
            c_p[None], n_p[:, :, 0, :][None], m_p[:, :, 0, 0][None],
            ckv_s.reshape(1, b_s, t_s, KV_LORA), kr_s.reshape(1, b_s, t_s, ROPE_DIM),
            c_s[None], n_s[:, :, 0, :][None], m_s[:, :, 0, 0][None])
```

```python
import functools
import math

import jax
import jax.numpy as jnp
from jax import lax
from jax.experimental import pallas as pl
from jax.experimental.pallas import tpu as pltpu

F32 = jnp.float32
BF16 = jnp.bfloat16

D_MODEL = 1024
H_A = 4
DV_A = D_MODEL // H_A
DK_A = DV_A // 2
V_DIM = 128
H_B = D_MODEL // V_DIM
NOPE_DIM = 128
ROPE_DIM = 64
Q_LORA = 256
KV_LORA = 128
ROPE_THETA = 10000.0
SM_SCALE = (NOPE_DIM + ROPE_DIM) ** -0.5
Q_SCALE = SM_SCALE * math.log2(math.e)
D_FF = 2816
EPS = 1e-6
PAGE_SIZE = 128
QK_PAD = 256

LANE = 128
VMEM_LIMIT = 56 * 1024 * 1024
ROW_TILE = 512
FF_CHUNK = 256
MLSTM_CHUNK = 256
MLSTM_SAMPLE_SEQS = 2
ATTN_TILE = 256
PAGES_PER_STEP = 64
SAMPLE_SUB_BLOCKS = 4
NEG_BIG = -1e30


def _params(*sem):
    return pltpu.CompilerParams(dimension_semantics=sem, vmem_limit_bytes=VMEM_LIMIT)


def _resident(shape):
    return pl.BlockSpec(shape, lambda *_: (0,) * len(shape), pipeline_mode=pl.Buffered(1))


def _rms(x, g):
    return x * lax.rsqrt(jnp.mean(x * x, axis=-1, keepdims=True) + EPS) * g


def _dot(a, b):
    return jnp.dot(a, b, preferred_element_type=F32)


def _dot_nt(a, b):
    return lax.dot_general(a, b, (((1,), (1,)), ((), ())), preferred_element_type=F32)


def _dot_tn(a, b):
    return lax.dot_general(a, b, (((0,), (0,)), ((), ())), preferred_element_type=F32)


def _ffn_body(x, gpre, gpost, wgu_ref, wdn_ref, act_ref):
    h = _rms(x, gpre).astype(BF16)
    for lo in range(0, D_FF, FF_CHUNK):
        g = _dot(h, wgu_ref[:, lo:lo + FF_CHUNK])
        u = _dot(h, wgu_ref[:, D_FF + lo:D_FF + lo + FF_CHUNK])
        act_ref[:, lo:lo + FF_CHUNK] = (g * jax.nn.sigmoid(g) * u).astype(BF16)
    f = _dot(act_ref[...], wdn_ref[...])
    return x + 0.5 * _rms(f, gpost)


def _ffn_kernel(x_ref, gpre_ref, gpost_ref, wgu_ref, wdn_ref, o_ref, act_ref):
    o_ref[...] = _ffn_body(x_ref[...], gpre_ref[...], gpost_ref[...], wgu_ref, wdn_ref, act_ref)


def _ffn(x, gpre, gpost, wgu, wdn):
    m = x.shape[0]
    tm = min(ROW_TILE, m)
    row = pl.BlockSpec((tm, D_MODEL), lambda i: (i, 0))
    return pl.pallas_call(
        _ffn_kernel,
        grid=(m // tm,),
        in_specs=[row, _resident((1, D_MODEL)), _resident((1, D_MODEL)),
                  _resident((D_MODEL, 2 * D_FF)), _resident((D_FF, D_MODEL))],
        out_specs=row,
        out_shape=jax.ShapeDtypeStruct((m, D_MODEL), F32),
        scratch_shapes=[pltpu.VMEM((tm, D_FF), BF16)],
        compiler_params=_params("parallel"),
        name="ffn",
    )(x, gpre, gpost, wgu, wdn)


_O_QA, _O_KA, _O_VA, _O_OA, _O_GA, _O_GB = 0, 512, 1024, 2048, 3072, 4096
_O_CQ, _O_CKV, _O_KR, _O_KRS, _O_GT = 5120, 5376, 5504, 5632, 5760
IN_COLS_PAD = 5888


def _proj_kernel(x_ref, gpre_ref, win_ref, bg_ref, gq_ref, gkv_ref, wuq_ref, wuk_ref, cos_ref, sin_ref,
                 qa_ref, ka_ref, va_ref, ga_ref, gb_ref, gt_ref, ckv_ref, kr_ref, kcat_ref, qcat_ref):
    h = _rms(x_ref[...], gpre_ref[...]).astype(BF16)

    def seg(lo, n):
        return _dot(h, win_ref[:, lo:lo + n])

    qa_ref[...] = seg(_O_QA, H_A * DK_A).astype(BF16)
    ka_ref[...] = (seg(_O_KA, H_A * DK_A) * (DK_A ** -0.5)).astype(BF16)
    va_ref[...] = seg(_O_VA, D_MODEL).astype(BF16)
    ga_ref[...] = (jax.nn.sigmoid(seg(_O_GA, D_MODEL)) * jax.nn.sigmoid(seg(_O_OA, D_MODEL))).astype(BF16)
    gb_ref[...] = jax.nn.sigmoid(seg(_O_GB, D_MODEL)).astype(BF16)

    gt = seg(_O_GT, LANE) + bg_ref[...]
    logsig = jnp.minimum(gt, 0.0) - jnp.log1p(jnp.exp(-jnp.abs(gt)))
    lane = lax.broadcasted_iota(jnp.int32, gt.shape, 1)
    gt_ref[...] = jnp.where(lane < H_A, gt, logsig)

    cos = cos_ref[...]
    sin = sin_ref[...]
    ckv = _rms(seg(_O_CKV, KV_LORA), gkv_ref[...])
    kr = seg(_O_KR, LANE) * cos + seg(_O_KRS, LANE) * sin
    ckv_ref[...] = ckv
    kr_ref[...] = kr[:, :ROPE_DIM]
    kcat_ref[:, :KV_LORA] = ckv.astype(BF16)
    kcat_ref[:, KV_LORA:] = kr.astype(BF16)

    cq = _rms(seg(_O_CQ, Q_LORA), gq_ref[...]).astype(BF16)
    q = _dot(cq, wuq_ref[...])
    for hh in range(H_B):
        q_nope = q[:, hh * LANE:(hh + 1) * LANE].astype(BF16)
        q_lat = _dot(q_nope, wuk_ref[hh])
        lo = H_B * LANE + hh * LANE
        q_rope = q[:, lo:lo + LANE] * cos + q[:, lo + H_B * LANE:lo + (H_B + 1) * LANE] * sin
        qcat_ref[:, hh * QK_PAD:hh * QK_PAD + LANE] = (q_lat * Q_SCALE).astype(BF16)
        qcat_ref[:, hh * QK_PAD + LANE:(hh + 1) * QK_PAD] = (q_rope * Q_SCALE).astype(BF16)


def _proj(x, w, cos, sin):
    m = x.shape[0]
    tm = min(ROW_TILE, m)
    ntab = cos.shape[0] // tm

    def row(n):
        return pl.BlockSpec((tm, n), lambda i: (i, 0))

    tab = pl.BlockSpec((tm, LANE), lambda i: (i % ntab, 0))
    outs = [(H_A * DK_A, BF16), (H_A * DK_A, BF16), (D_MODEL, BF16), (D_MODEL, BF16), (D_MODEL, BF16),
            (LANE, F32), (KV_LORA, F32), (ROPE_DIM, F32), (QK_PAD, BF16), (H_B * QK_PAD, BF16)]
    return pl.pallas_call(
        _proj_kernel,
        grid=(m // tm,),
        in_specs=[row(D_MODEL), _resident((1, D_MODEL)), _resident((D_MODEL, IN_COLS_PAD)),
                  _resident((1, LANE)), _resident((1, Q_LORA)), _resident((1, KV_LORA)),
                  _resident((Q_LORA, 3 * H_B * LANE)), _resident((H_B, NOPE_DIM, KV_LORA)), tab, tab],
        out_specs=[row(n) for n, _ in outs],
        out_shape=[jax.ShapeDtypeStruct((m, n), dt) for n, dt in outs],
        compiler_params=_params("parallel"),
        name="proj",
    )(x, w["g_mix_pre"], w["w_in"], w["b_gates"], w["g_q"], w["g_kv"], w["w_uq"], w["w_uk"], cos, sin)


def _mlstm_chunk(q, k, v, i_row, f_row, c_st, n_row, m, tri, eye):
    ln = q.shape[0]
    b_col = jnp.sum(jnp.where(tri, f_row, 0.0), axis=1, keepdims=True)
    b_row = jnp.sum(jnp.where(eye, b_col, 0.0), axis=0, keepdims=True)
    d = jnp.where(tri, b_col - b_row + i_row, -jnp.inf)
    m_inter = b_col + m
    m_t = jnp.maximum(m_inter, jnp.max(d, axis=1, keepdims=True))
    wgt = jnp.exp(d - m_t)
    inter = jnp.exp(m_inter - m_t)
    s = _dot_nt(q, k) * wgt
    num = _dot(s.astype(BF16), v) + inter * _dot_nt(q, c_st.astype(BF16))
    qn = jnp.sum(q.astype(F32) * n_row, axis=1, keepdims=True)
    den = jnp.sum(s, axis=1, keepdims=True) + inter * qn
    h = num / jnp.maximum(jnp.abs(den), jnp.exp(-m_t))

    b_last = b_row[:, ln - 1:ln]
    g_row = b_last - b_row + i_row
    m_new = jnp.maximum(b_last + m, jnp.max(g_row, axis=1, keepdims=True))
    a_row = jnp.exp(g_row - m_new)
    decay = jnp.exp(b_last + m - m_new)
    a_col = jnp.sum(jnp.where(eye, a_row, 0.0), axis=1, keepdims=True)
    va = (v.astype(F32) * a_col).astype(BF16)
    c_new = decay * c_st + _dot_tn(va, k)
    n_new = decay * n_row + jnp.sum(a_col * k.astype(F32), axis=0, keepdims=True)
    return h, c_new, n_new, m_new


def _mlstm_kernel(*refs, chunk, n_chunks, n_seq, has_init):
    if has_init:
        q_ref, k_ref, v_ref, gt_ref, gh_ref, c0_ref, n0_ref, m0_ref, h_ref, c_ref, n_ref, m_ref = refs
        c_ref[...] = c0_ref[...]
        n_ref[...] = n0_ref[...]
        m_ref[...] = m0_ref[...]
    else:
        q_ref, k_ref, v_ref, gt_ref, gh_ref, h_ref, c_ref, n_ref, m_ref = refs
        c_ref[...] = jnp.zeros(c_ref.shape, F32)
        n_ref[...] = jnp.zeros(n_ref.shape, F32)
        m_ref[...] = jnp.zeros(m_ref.shape, F32)
    t_idx = lax.broadcasted_iota(jnp.int32, (chunk, chunk), 0)
    s_idx = lax.broadcasted_iota(jnp.int32, (chunk, chunk), 1)
    tri = s_idx <= t_idx
    eye = s_idx == t_idx
    t_len = chunk * n_chunks

    def step(c, carry):
        for b in range(n_seq):
            start = b * t_len + c * chunk
            rows = pl.ds(start if isinstance(start, int) else pl.multiple_of(start, chunk), chunk)
            for hh in range(H_A):
                qk_cols = slice(hh * DK_A, (hh + 1) * DK_A)
                v_cols = slice(hh * DV_A, (hh + 1) * DV_A)
                h, c_st, n_row, m = _mlstm_chunk(
                    q_ref[rows, qk_cols], k_ref[rows, qk_cols], v_ref[rows, v_cols],
                    gt_ref[b, hh, c, 0:1, :], gt_ref[b, hh, c, 1:2, :],
                    c_ref[b, hh], n_ref[b, hh], m_ref[b, hh][:, :1], tri, eye)
                h_ref[rows, v_cols] = _rms(h, gh_ref[:, v_cols]).astype(h_ref.dtype)
                c_ref[b, hh] = c_st
                n_ref[b, hh] = n_row
                m_ref[b, hh] = jnp.broadcast_to(m, (1, LANE))
        return carry

    if n_chunks == 1:
        step(0, 0)
    else:
        lax.fori_loop(0, n_chunks, step, 0)


def _mlstm(qa, ka, va, gates_t, g_h, init, n_batch, t_len, chunk, n_seq):
    has_init = init is not None
    n_chunks = t_len // chunk
    st4 = lambda n: pl.BlockSpec((n_seq, H_A, 1, n), lambda b: (b, 0, 0, 0))
    c_spec = pl.BlockSpec((n_seq, H_A, DV_A, DK_A), lambda b: (b, 0, 0, 0))
    row = lambda n: pl.BlockSpec((n_seq * t_len, n), lambda b: (b, 0))
    in_specs = [row(H_A * DK_A), row(H_A * DK_A), row(D_MODEL),
                pl.BlockSpec((n_seq, H_A, n_chunks, 2, chunk), lambda b: (b, 0, 0, 0, 0)),
                _resident((1, D_MODEL))]
    args = [qa, ka, va, gates_t, g_h]
    if has_init:
        in_specs += [c_spec, st4(DK_A), st4(LANE)]
        args += list(init)
    return pl.pallas_call(
        functools.partial(_mlstm_kernel, chunk=chunk, n_chunks=n_chunks, n_seq=n_seq, has_init=has_init),
        grid=(n_batch // n_seq,),
        in_specs=in_specs,
        out_specs=[row(D_MODEL), c_spec, st4(DK_A), st4(LANE)],
        out_shape=[jax.ShapeDtypeStruct((n_batch * t_len, D_MODEL), BF16),
                   jax.ShapeDtypeStruct((n_batch, H_A, DV_A, DK_A), F32),
                   jax.ShapeDtypeStruct((n_batch, H_A, 1, DK_A), F32),
                   jax.ShapeDtypeStruct((n_batch, H_A, 1, LANE), F32)],
        compiler_params=_params("parallel"),
        name="mlstm_init" if has_init else "mlstm",
    )(*args)


def _attn_kernel(q_ref, k_ref, o_ref, q_scr, s_ref, mx_ref, acc_ref):
    tq = q_ref.shape[0]
    qi = pl.program_id(1)
    rows = H_B * tq
    for hh in range(H_B):
        q_scr[hh * tq:(hh + 1) * tq, :] = q_ref[:, hh * QK_PAD:(hh + 1) * QK_PAD]
    ones = jnp.ones((tq, LANE), BF16)

    def key_block(j):
        return k_ref[pl.ds(pl.multiple_of(j * tq, tq), tq), :]

    def lane_fold(s):
        return jnp.maximum(s[:, :LANE], s[:, LANE:])

    def weighted(s, m2, kj):
        return _dot(jnp.exp2(s - m2).astype(BF16), jnp.concatenate([kj[:, :KV_LORA], ones], axis=1))

    k_diag = key_block(qi)
    r_idx = lax.broadcasted_iota(jnp.int32, (H_B, tq, tq), 1).reshape(rows, tq)
    c_idx = lax.broadcasted_iota(jnp.int32, (rows, tq), 1)
    s_diag = jnp.where(c_idx <= r_idx, _dot_nt(q_scr[...], k_diag), -jnp.inf)

    mx_ref[...] = lane_fold(s_diag)

    @pl.loop(0, qi)
    def _(j):
        s = _dot_nt(q_scr[...], key_block(j))
        s_ref[j] = s
        mx_ref[...] = jnp.maximum(mx_ref[...], lane_fold(s))

    m = jnp.broadcast_to(jnp.max(mx_ref[...], axis=1, keepdims=True), (rows, LANE))
    mx_ref[...] = m
    acc_ref[...] = weighted(s_diag, jnp.concatenate([m, m], axis=1), k_diag)

    @pl.loop(0, qi)
    def _(j):
        m = mx_ref[...]
        acc_ref[...] += weighted(s_ref[j], jnp.concatenate([m, m], axis=1), key_block(j))

    o = acc_ref[:, :KV_LORA] / acc_ref[:, KV_LORA:]
    for hh in range(H_B):
        o_ref[:, hh * KV_LORA:(hh + 1) * KV_LORA] = o[hh * tq:(hh + 1) * tq].astype(o_ref.dtype)


def _attn_prompt(qcat, kcat, n_batch, t_len):
    tq = min(ATTN_TILE, t_len)
    assert tq == 2 * LANE
    nq = t_len // tq
    return pl.pallas_call(
        _attn_kernel,
        grid=(n_batch, nq),
        in_specs=[pl.BlockSpec((tq, H_B * QK_PAD), lambda b, i: (b * nq + i, 0)),
                  pl.BlockSpec((t_len, QK_PAD), lambda b, i: (b, 0))],
        out_specs=pl.BlockSpec((tq, H_B * KV_LORA), lambda b, i: (b * nq + i, 0)),
        out_shape=jax.ShapeDtypeStruct((n_batch * t_len, H_B * KV_LORA), BF16),
        scratch_shapes=[pltpu.VMEM((H_B * tq, QK_PAD), BF16),
                        pltpu.VMEM((max(nq - 1, 1), H_B * tq, tq), F32),
                        pltpu.VMEM((H_B * tq, LANE), F32),
                        pltpu.VMEM((H_B * tq, 2 * KV_LORA), F32)],
        compiler_params=_params("parallel", "parallel"),
        name="attn_prompt",
    )(qcat, kcat)


def _attn_sample_kernel(pt_ref, q_ref, kvn_ref, krn_ref, kv_hbm, kr_hbm, o_ref,
                        kv_buf, kr_buf, sem, m_ref, l_ref, acc_ref, *, n_steps, t_new):
    b = pl.program_id(0)
    c = pl.program_id(1)
    n_b = pl.num_programs(0)
    g = b * n_steps + c
    slot = g % 2

    def page_copies(bb, cc, sl):
        copies = []
        for p in range(PAGES_PER_STEP):
            page = pt_ref[bb, cc * PAGES_PER_STEP + p]
            span = pl.ds(p * PAGE_SIZE, PAGE_SIZE)
            copies.append(pltpu.make_async_copy(kv_hbm.at[page], kv_buf.at[sl, span, :], sem.at[sl, 0]))
            copies.append(pltpu.make_async_copy(kr_hbm.at[page], kr_buf.at[sl, :, span], sem.at[sl, 1]))
        return copies

    @pl.when(g == 0)
    def _():
        for cp in page_copies(b, c, slot):
            cp.start()

    @pl.when(g + 1 < n_b * n_steps)
    def _():
        nxt = g + 1
        for cp in page_copies(nxt // n_steps, nxt % n_steps, 1 - slot):
            cp.start()

    @pl.when(c == 0)
    def _():
        m_ref[...] = jnp.full(m_ref.shape, -jnp.inf, F32)
        l_ref[...] = jnp.zeros(l_ref.shape, F32)
        acc_ref[...] = jnp.zeros(acc_ref.shape, F32)

    q = q_ref[0]
    q_lat = q[:, :KV_LORA]
    q_rope = q[:, KV_LORA:KV_LORA + ROPE_DIM]

    def partial_softmax(s, v):
        m_blk = jnp.max(s, axis=1, keepdims=True)
        p = jnp.exp2(s - m_blk)
        return m_blk, jnp.sum(p, axis=1, keepdims=True), _dot(p.astype(BF16), v)

    def merge(parts):
        m_old = m_ref[...]
        m_new = m_old
        for m_blk, _, _ in parts:
            m_new = jnp.maximum(m_new, m_blk)
        alpha = jnp.exp2(m_old - m_new)
        l = alpha * l_ref[...]
        acc = alpha * acc_ref[...]
        for m_blk, l_blk, o_blk in parts:
            w_blk = jnp.exp2(m_blk - m_new)
            l = l + w_blk * l_blk
            acc = acc + w_blk * o_blk
        m_ref[...] = m_new
        l_ref[...] = l
        acc_ref[...] = acc

    for cp in page_copies(b, c, slot):
        cp.wait()
    sub = PAGES_PER_STEP * PAGE_SIZE // SAMPLE_SUB_BLOCKS
    parts = []
    for i in range(SAMPLE_SUB_BLOCKS):
        kv = kv_buf[slot, i * sub:(i + 1) * sub, :].astype(BF16)
        kr_t = kr_buf[slot, :, i * sub:(i + 1) * sub].astype(BF16)
        parts.append(partial_softmax(_dot_nt(q_lat, kv) + _dot(q_rope, kr_t), kv))
    merge(parts)

    @pl.when(c == n_steps - 1)
    def _():
        kvn = kvn_ref[0].astype(BF16)
        krn = krn_ref[0].astype(BF16)
        s = _dot_nt(q_lat, kvn) + _dot_nt(q_rope, krn)
        r_tok = lax.broadcasted_iota(jnp.int32, s.shape, 0) % t_new
        c_tok = lax.broadcasted_iota(jnp.int32, s.shape, 1)
        merge([partial_softmax(jnp.where(c_tok <= r_tok, s, -jnp.inf), kvn)])
        o_ref[0] = (acc_ref[...] / l_ref[...]).astype(o_ref.dtype)


def _attn_sample(page_table, q, kv_new, kr_new, cache_kv, cache_kr_t, t_new):
    n_b, n_pages = page_table.shape
    assert n_pages % PAGES_PER_STEP == 0
    step_rows = PAGES_PER_STEP * PAGE_SIZE
    n_steps = n_pages // PAGES_PER_STEP
    rows = q.shape[1]
    pad = kv_new.shape[1]
    grid_spec = pltpu.PrefetchScalarGridSpec(
        num_scalar_prefetch=1,
        grid=(n_b, n_steps),
        in_specs=[pl.BlockSpec((1, rows, QK_PAD), lambda b, c, pt: (b, 0, 0)),
                  pl.BlockSpec((1, pad, KV_LORA), lambda b, c, pt: (b, 0, 0)),
                  pl.BlockSpec((1, pad, ROPE_DIM), lambda b, c, pt: (b, 0, 0)),
                  pl.BlockSpec(memory_space=pl.ANY),
                  pl.BlockSpec(memory_space=pl.ANY)],
        out_specs=pl.BlockSpec((1, rows, KV_LORA), lambda b, c, pt: (b, 0, 0)),
        scratch_shapes=[pltpu.VMEM((2, step_rows, KV_LORA), F32),
                        pltpu.VMEM((2, ROPE_DIM, step_rows), F32),
                        pltpu.SemaphoreType.DMA((2, 2)),
                        pltpu.VMEM((rows, 1), F32), pltpu.VMEM((rows, 1), F32),
                        pltpu.VMEM((rows, KV_LORA), F32)])
    return pl.pallas_call(
        functools.partial(_attn_sample_kernel, n_steps=n_steps, t_new=t_new),
        grid_spec=grid_spec,
        out_shape=jax.ShapeDtypeStruct((n_b, rows, KV_LORA), BF16),
        compiler_params=_params("arbitrary", "arbitrary"),
        name="attn_sample",
    )(page_table, q, kv_new, kr_new, cache_kv, cache_kr_t)


def _out_ffn_kernel(x_ref, ha_ref, ga_ref, gb_ref, ol_ref, wuv_ref, wout_ref, gmix_ref,
                    gpre_ref, gpost_ref, wgu_ref, wdn_ref, o_ref, act_ref):
    merged = []
    for hh in range(H_B):
        cols = slice(hh * V_DIM, (hh + 1) * V_DIM)
        y_b = _dot(ol_ref[:, cols], wuv_ref[hh])
        y = ga_ref[:, cols].astype(F32) * ha_ref[:, cols].astype(F32) + gb_ref[:, cols].astype(F32) * y_b
        merged.append(y.astype(BF16))
    mix = _dot(jnp.concatenate(merged, axis=1), wout_ref[...])
    x = x_ref[...] + _rms(mix, gmix_ref[...])
    o_ref[...] = _ffn_body(x, gpre_ref[...], gpost_ref[...], wgu_ref, wdn_ref, act_ref)


def _out_ffn(x, ha, ga, gb, o_lat, w):
    m = x.shape[0]
    tm = min(ROW_TILE, m)
    row = pl.BlockSpec((tm, D_MODEL), lambda i: (i, 0))
    vec = _resident((1, D_MODEL))
    return pl.pallas_call(
        _out_ffn_kernel,
        grid=(m // tm,),
        in_specs=[row, row, row, row, row, _resident((H_B, KV_LORA, V_DIM)), _resident((D_MODEL, D_MODEL)), vec,
                  vec, vec, _resident((D_MODEL, 2 * D_FF)), _resident((D_FF, D_MODEL))],
        out_specs=row,
        out_shape=jax.ShapeDtypeStruct((m, D_MODEL), F32),
        scratch_shapes=[pltpu.VMEM((tm, D_FF), BF16)],
        compiler_params=_params("parallel"),
        name="out_ffn",
    )(x, ha, ga, gb, o_lat, w["w_uv"], w["w_out"], w["g_mix_post"],
      w["g_ffn2_pre"], w["g_ffn2_post"], w["w_ffn2_gu"], w["w_ffn2_down"])


def _prep_weights(norm_ffn1_pre, norm_ffn1_post, w_ffn1_gu, w_ffn1_down, norm_mix_pre, norm_mix_post,
                  w_in, b_gates, w_uq, norm_q_lat, norm_kv_lat, w_uk, w_uv, norm_mlstm_h, w_out,
                  norm_ffn2_pre, norm_ffn2_post, w_ffn2_gu, w_ffn2_down):
    half = ROPE_DIM // 2
    row = lambda g: g.reshape(1, -1).astype(F32)

    def swap_pad(wr):
        z = jnp.zeros(wr.shape[:-1] + (LANE - ROPE_DIM,), wr.dtype)
        return (jnp.concatenate([wr, z], axis=-1),
                jnp.concatenate([wr[..., half:], wr[..., :half], z], axis=-1))

    sizes = (H_A * DK_A, H_A * DK_A, H_A * DV_A, H_A, H_A, H_A * DV_A, Q_LORA, KV_LORA, ROPE_DIM, D_MODEL, D_MODEL)
    offs = [0]
    for n in sizes:
        offs.append(offs[-1] + n)
    qa, ka, va, ip, fp, oa, cq, ckv, kr, ga, gb = [w_in[:, offs[i]:offs[i + 1]] for i in range(len(sizes))]
    kr_p, kr_s = swap_pad(kr)
    gates = jnp.concatenate([ip, fp, jnp.zeros((D_MODEL, LANE - 2 * H_A), w_in.dtype)], axis=1)
    w_in2 = jnp.concatenate([qa, ka, va, oa, ga, gb, cq, ckv, kr_p, kr_s, gates], axis=1).astype(BF16)
    assert w_in2.shape[1] == IN_COLS_PAD

    uq = w_uq.reshape(Q_LORA, H_B, NOPE_DIM + ROPE_DIM)
    uq_p, uq_s = swap_pad(uq[..., NOPE_DIM:])
    w_uq2 = jnp.concatenate([uq[..., :NOPE_DIM].reshape(Q_LORA, -1), uq_p.reshape(Q_LORA, -1),
                             uq_s.reshape(Q_LORA, -1)], axis=1).astype(BF16)
    return dict(
        g_ffn1_pre=row(norm_ffn1_pre), g_ffn1_post=row(norm_ffn1_post),
        w_ffn1_gu=w_ffn1_gu.astype(BF16), w_ffn1_down=w_ffn1_down.astype(BF16),
        g_ffn2_pre=row(norm_ffn2_pre), g_ffn2_post=row(norm_ffn2_post),
        w_ffn2_gu=w_ffn2_gu.astype(BF16), w_ffn2_down=w_ffn2_down.astype(BF16),
        g_mix_pre=row(norm_mix_pre), g_mix_post=row(norm_mix_post),
        w_in=w_in2, w_uq=w_uq2,
        b_gates=jnp.concatenate([b_gates.astype(F32), jnp.zeros((LANE - 2 * H_A,), F32)]).reshape(1, LANE),
        g_q=row(norm_q_lat), g_kv=row(norm_kv_lat),
        w_uk=jnp.transpose(w_uk, (1, 2, 0)).astype(BF16),
        w_uv=jnp.transpose(w_uv, (1, 0, 2)).astype(BF16),
        g_h=row(norm_mlstm_h), w_out=w_out.astype(BF16))


def _rope_tables(pos):
    half = ROPE_DIM // 2
    inv_freq = ROPE_THETA ** (-jnp.arange(half, dtype=F32) / half)
    ang = pos.astype(F32)[:, None] * inv_freq[None, :]
    cos, sin = jnp.cos(ang), jnp.sin(ang)
    z = jnp.zeros((pos.shape[0], LANE - ROPE_DIM), F32)
    return jnp.concatenate([cos, cos, z], axis=1), jnp.concatenate([-sin, sin, z], axis=1)


def _gates_by_head(gt, n_batch, t_len, chunk):
    g = gt[:, :2 * H_A].reshape(n_batch, t_len // chunk, chunk, 2, H_A)
    return jnp.transpose(g, (0, 4, 1, 3, 2))


def kernel(x_prompt, x_sample, cache_kv_latent, cache_k_rope, state_mlstm_C, state_mlstm_n, state_mlstm_m,
           page_table, norm_ffn1_pre, norm_ffn1_post, w_ffn1_gu, w_ffn1_down, norm_mix_pre, norm_mix_post,
           w_in, b_gates, w_uq, norm_q_lat, norm_kv_lat, w_uk, w_uv, norm_mlstm_h, w_out,
           norm_ffn2_pre, norm_ffn2_post, w_ffn2_gu, w_ffn2_down):
    assert w_in.shape[0] == 1, "single-layer trunk"
    b_p, t_p, _ = x_prompt.shape
    b_s, t_s, _ = x_sample.shape
    past_len = page_table.shape[1] * PAGE_SIZE
    w = _prep_weights(norm_ffn1_pre[0], norm_ffn1_post[0], w_ffn1_gu[0], w_ffn1_down[0], norm_mix_pre[0],
                      norm_mix_post[0], w_in[0], b_gates[0], w_uq[0], norm_q_lat[0], norm_kv_lat[0], w_uk[0],
                      w_uv[0], norm_mlstm_h[0], w_out[0], norm_ffn2_pre[0], norm_ffn2_post[0], w_ffn2_gu[0],
                      w_ffn2_down[0])

    xp = x_prompt.reshape(b_p * t_p, D_MODEL)
    xp = _ffn(xp, w["g_ffn1_pre"], w["g_ffn1_post"], w["w_ffn1_gu"], w["w_ffn1_down"])
    cos_p, sin_p = _rope_tables(jnp.arange(t_p, dtype=jnp.int32))
    qa, ka, va, ga, gb, gt, ckv_p, kr_p, kcat, qcat = _proj(xp, w, cos_p, sin_p)
    chunk = min(MLSTM_CHUNK, t_p)
    ha, c_p, n_p, m_p = _mlstm(qa, ka, va, _gates_by_head(gt, b_p, t_p, chunk), w["g_h"], None, b_p, t_p, chunk, 1)
    o_lat = _attn_prompt(qcat, kcat, b_p, t_p)
    yp = _out_ffn(xp, ha, ga, gb, o_lat, w)

    t_pad = 16
    new_pad = 8
    xs = x_sample.reshape(b_s * t_s, D_MODEL)
    xs = _ffn(xs, w["g_ffn1_pre"], w["g_ffn1_post"], w["w_ffn1_gu"], w["w_ffn1_down"])
    cos_s, sin_s = _rope_tables(past_len + jnp.arange(t_s, dtype=jnp.int32))
    reps = min(ROW_TILE, b_s * t_s) // t_s
    qa, ka, va, ga, gb, gt, ckv_s, kr_s, _, qcat = _proj(xs, w, jnp.tile(cos_s, (reps, 1)), jnp.tile(sin_s, (reps, 1)))

    def pad_t(a, value=0.0):
        a = a.reshape(b_s, t_s, a.shape[-1])
        a = jnp.pad(a, ((0, 0), (0, t_pad - t_s), (0, 0)), constant_values=value)
        return a.reshape(b_s * t_pad, a.shape[-1])

    tok = jnp.arange(b_s * t_pad) % t_pad
    lane = jnp.arange(LANE)
    gt_pad = jnp.where((tok[:, None] >= t_s) & (lane[None, :] < H_A), NEG_BIG, pad_t(gt))
    m0 = jnp.broadcast_to(state_mlstm_m[0][:, :, None, None], (b_s, H_A, 1, LANE))
    ha, c_s, n_s, m_s = _mlstm(pad_t(qa), pad_t(ka), pad_t(va), _gates_by_head(gt_pad, b_s, t_pad, t_pad), w["g_h"],
                               (state_mlstm_C[0], state_mlstm_n[0][:, :, None, :], m0), b_s, t_pad, t_pad,
                               math.gcd(b_s, MLSTM_SAMPLE_SEQS))
    ha = ha.reshape(b_s, t_pad, D_MODEL)[:, :t_s].reshape(b_s * t_s, D_MODEL)

    q_s = qcat.reshape(b_s, t_s, H_B, QK_PAD).transpose(0, 2, 1, 3).reshape(b_s, H_B * t_s, QK_PAD)
    pad_new = lambda a: jnp.pad(a.reshape(b_s, t_s, -1), ((0, 0), (0, new_pad - t_s), (0, 0)))
    o_s = _attn_sample(page_table, q_s, pad_new(ckv_s), pad_new(kr_s), cache_kv_latent[0],
                       jnp.swapaxes(cache_k_rope[0], 1, 2), t_s)
    o_s = o_s.reshape(b_s, H_B, t_s, KV_LORA).transpose(0, 2, 1, 3).reshape(b_s * t_s, H_B * KV_LORA)
    ys = _out_ffn(xs, ha, ga, gb, o_s, w)

    return (yp.reshape(b_p, t_p, D_MODEL), ys.reshape(b_s, t_s, D_MODEL),
            ckv_p.reshape(1, b_p, t_p, KV_LORA), kr_p.reshape(1, b_p, t_p, ROPE_DIM),
            c_p[None], n_p[:, :, 0, :][None], m_p[:, :, 0, 0][None],
            ckv_s.reshape(1, b_s, t_s, KV_LORA), kr_s.reshape(1, b_s, t_s, ROPE_DIM),
            c_s[None], n_s[:, :, 0, :][None], m_s[:, :, 0, 0][None])
```

```python
import functools
import math

import jax
import jax.numpy as jnp
from jax import lax
from jax.experimental import pallas as pl
from jax.experimental.pallas import tpu as pltpu

F32 = jnp.float32
BF16 = jnp.bfloat16

D_MODEL = 1024
H_A = 4
DV_A = D_MODEL // H_A
DK_A = DV_A // 2
V_DIM = 128
H_B = D_MODEL // V_DIM
NOPE_DIM = 128
ROPE_DIM = 64
Q_LORA = 256
KV_LORA = 128
ROPE_THETA = 10000.0
SM_SCALE = (NOPE_DIM + ROPE_DIM) ** -0.5
Q_SCALE = SM_SCALE * math.log2(math.e)
D_FF = 2816
EPS = 1e-6
PAGE_SIZE = 128
QK_PAD = 256

LANE = 128
VMEM_LIMIT = 56 * 1024 * 1024
ROW_TILE = 512
FF_CHUNK = 256
MLSTM_CHUNK = 256
MLSTM_SAMPLE_SEQS = 4
ATTN_TILE = 256
SAMPLE_SUB_BLOCKS = 4
NEG_BIG = -1e30


def _params(*sem):
    return pltpu.CompilerParams(dimension_semantics=sem, vmem_limit_bytes=VMEM_LIMIT)


def _resident(shape):
    return pl.BlockSpec(shape, lambda *_: (0,) * len(shape), pipeline_mode=pl.Buffered(1))


def _rms(x, g):
    return x * lax.rsqrt(jnp.mean(x * x, axis=-1, keepdims=True) + EPS) * g


def _dot(a, b):
    return jnp.dot(a, b, preferred_element_type=F32)


def _dot_nt(a, b):
    return lax.dot_general(a, b, (((1,), (1,)), ((), ())), preferred_element_type=F32)


def _dot_tn(a, b):
    return lax.dot_general(a, b, (((0,), (0,)), ((), ())), preferred_element_type=F32)


def _ffn_body(x, gpre, gpost, wgu_ref, wdn_ref, act_ref):
    h = _rms(x, gpre).astype(BF16)
    for lo in range(0, D_FF, FF_CHUNK):
        g = _dot(h, wgu_ref[:, lo:lo + FF_CHUNK])
        u = _dot(h, wgu_ref[:, D_FF + lo:D_FF + lo + FF_CHUNK])
        act_ref[:, lo:lo + FF_CHUNK] = (g * jax.nn.sigmoid(g) * u).astype(BF16)
    f = _dot(act_ref[...], wdn_ref[...])
    return x + 0.5 * _rms(f, gpost)


def _ffn_kernel(x_ref, gpre_ref, gpost_ref, wgu_ref, wdn_ref, o_ref, act_ref):
    o_ref[...] = _ffn_body(x_ref[...], gpre_ref[...], gpost_ref[...], wgu_ref, wdn_ref, act_ref)


def _ffn(x, gpre, gpost, wgu, wdn):
    m = x.shape[0]
    tm = min(ROW_TILE, m)
    row = pl.BlockSpec((tm, D_MODEL), lambda i: (i, 0))
    return pl.pallas_call(
        _ffn_kernel,
        grid=(m // tm,),
        in_specs=[row, _resident((1, D_MODEL)), _resident((1, D_MODEL)),
                  _resident((D_MODEL, 2 * D_FF)), _resident((D_FF, D_MODEL))],
        out_specs=row,
        out_shape=jax.ShapeDtypeStruct((m, D_MODEL), F32),
        scratch_shapes=[pltpu.VMEM((tm, D_FF), BF16)],
        compiler_params=_params("parallel"),
        name="ffn",
    )(x, gpre, gpost, wgu, wdn)


_O_QA, _O_KA, _O_VA, _O_OA, _O_GA, _O_GB = 0, 512, 1024, 2048, 3072, 4096
_O_CQ, _O_CKV, _O_KR, _O_KRS, _O_GT = 5120, 5376, 5504, 5632, 5760
IN_COLS_PAD = 5888


def _proj_kernel(x_ref, gpre_ref, win_ref, bg_ref, gq_ref, gkv_ref, wuq_ref, wuk_ref, cos_ref, sin_ref,
                 qa_ref, ka_ref, va_ref, ga_ref, gb_ref, gt_ref, ckv_ref, kr_ref, kcat_ref, qcat_ref):
    h = _rms(x_ref[...], gpre_ref[...]).astype(BF16)

    def seg(lo, n):
        return _dot(h, win_ref[:, lo:lo + n])

    qa_ref[...] = seg(_O_QA, H_A * DK_A).astype(BF16)
    ka_ref[...] = (seg(_O_KA, H_A * DK_A) * (DK_A ** -0.5)).astype(BF16)
    va_ref[...] = seg(_O_VA, D_MODEL).astype(BF16)
    ga_ref[...] = (jax.nn.sigmoid(seg(_O_GA, D_MODEL)) * jax.nn.sigmoid(seg(_O_OA, D_MODEL))).astype(BF16)
    gb_ref[...] = jax.nn.sigmoid(seg(_O_GB, D_MODEL)).astype(BF16)

    gt = seg(_O_GT, LANE) + bg_ref[...]
    logsig = jnp.minimum(gt, 0.0) - jnp.log1p(jnp.exp(-jnp.abs(gt)))
    lane = lax.broadcasted_iota(jnp.int32, gt.shape, 1)
    gt_ref[...] = jnp.where(lane < H_A, gt, logsig)

    cos = cos_ref[...]
    sin = sin_ref[...]
    ckv = _rms(seg(_O_CKV, KV_LORA), gkv_ref[...])
    kr = seg(_O_KR, LANE) * cos + seg(_O_KRS, LANE) * sin
    ckv_ref[...] = ckv
    kr_ref[...] = kr[:, :ROPE_DIM]
    kcat_ref[:, :KV_LORA] = ckv.astype(BF16)
    kcat_ref[:, KV_LORA:] = kr.astype(BF16)

    cq = _rms(seg(_O_CQ, Q_LORA), gq_ref[...]).astype(BF16)
    q = _dot(cq, wuq_ref[...])
    for hh in range(H_B):
        q_nope = q[:, hh * LANE:(hh + 1) * LANE].astype(BF16)
        q_lat = _dot(q_nope, wuk_ref[hh])
        lo = H_B * LANE + hh * LANE
        q_rope = q[:, lo:lo + LANE] * cos + q[:, lo + H_B * LANE:lo + (H_B + 1) * LANE] * sin
        qcat_ref[:, hh * QK_PAD:hh * QK_PAD + LANE] = (q_lat * Q_SCALE).astype(BF16)
        qcat_ref[:, hh * QK_PAD + LANE:(hh + 1) * QK_PAD] = (q_rope * Q_SCALE).astype(BF16)


def _proj(x, w, cos, sin):
    m = x.shape[0]
    tm = min(ROW_TILE, m)
    ntab = cos.shape[0] // tm

    def row(n):
        return pl.BlockSpec((tm, n), lambda i: (i, 0))

    tab = pl.BlockSpec((tm, LANE), lambda i: (i % ntab, 0))
    outs = [(H_A * DK_A, BF16), (H_A * DK_A, BF16), (D_MODEL, BF16), (D_MODEL, BF16), (D_MODEL, BF16),
            (LANE, F32), (KV_LORA, F32), (ROPE_DIM, F32), (QK_PAD, BF16), (H_B * QK_PAD, BF16)]
    return pl.pallas_call(
        _proj_kernel,
        grid=(m // tm,),
        in_specs=[row(D_MODEL), _resident((1, D_MODEL)), _resident((D_MODEL, IN_COLS_PAD)),
                  _resident((1, LANE)), _resident((1, Q_LORA)), _resident((1, KV_LORA)),
                  _resident((Q_LORA, 3 * H_B * LANE)), _resident((H_B, NOPE_DIM, KV_LORA)), tab, tab],
        out_specs=[row(n) for n, _ in outs],
        out_shape=[jax.ShapeDtypeStruct((m, n), dt) for n, dt in outs],
        compiler_params=_params("parallel"),
        name="proj",
    )(x, w["g_mix_pre"], w["w_in"], w["b_gates"], w["g_q"], w["g_kv"], w["w_uq"], w["w_uk"], cos, sin)


def _mlstm_chunks(probs, tri, eye):
    ln = probs[0]["q"].shape[0]
    each = lambda fn, *lists: [fn(*a) for a in zip(*lists)]
    f_row = [p["f_row"] for p in probs]
    i_row = [p["i_row"] for p in probs]
    q = [p["q"] for p in probs]
    k = [p["k"] for p in probs]
    v = [p["v"] for p in probs]
    c_st = [p["c"] for p in probs]
    n_row = [p["n"] for p in probs]
    m = [p["m"] for p in probs]

    b_col = each(lambda f: jnp.sum(jnp.where(tri, f, 0.0), axis=1, keepdims=True), f_row)
    b_row = each(lambda b: jnp.sum(jnp.where(eye, b, 0.0), axis=0, keepdims=True), b_col)
    b_last = each(lambda b: b[:, ln - 1:ln], b_row)
    g_row = each(lambda bl, br, ir: bl - br + ir, b_last, b_row, i_row)
    m_new = each(lambda bl, mm, g: jnp.maximum(bl + mm, jnp.max(g, axis=1, keepdims=True)), b_last, m, g_row)
    a_row = each(lambda g, mn: jnp.exp(g - mn), g_row, m_new)
    decay = each(lambda bl, mm, mn: jnp.exp(bl + mm - mn), b_last, m, m_new)
    a_col = each(lambda a: jnp.sum(jnp.where(eye, a, 0.0), axis=1, keepdims=True), a_row)
    d = each(lambda bc, br, ir: jnp.where(tri, bc - br + ir, -jnp.inf), b_col, b_row, i_row)
    m_inter = each(lambda bc, mm: bc + mm, b_col, m)
    m_t = each(lambda mi, dd: jnp.maximum(mi, jnp.max(dd, axis=1, keepdims=True)), m_inter, d)
    s = each(lambda qq, kk, dd, mt: _dot_nt(qq, kk) * jnp.exp(dd - mt), q, k, d, m_t)
    inter = each(lambda mi, mt: jnp.exp(mi - mt), m_inter, m_t)
    qn = each(lambda qq, nn: jnp.sum(qq.astype(F32) * nn, axis=1, keepdims=True), q, n_row)
    den = each(lambda ss, it, x: jnp.sum(ss, axis=1, keepdims=True) + it * x, s, inter, qn)
    num = each(lambda ss, vv, it, qq, cc: _dot(ss.astype(BF16), vv) + it * _dot_nt(qq, cc.astype(BF16)),
               s, v, inter, q, c_st)
    h = each(lambda nu, de, mt: nu / jnp.maximum(jnp.abs(de), jnp.exp(-mt)), num, den, m_t)
    va = each(lambda vv, a: (vv.astype(F32) * a).astype(BF16), v, a_col)
    c_new = each(lambda dc, cc, x, kk: dc * cc + _dot_tn(x, kk), decay, c_st, va, k)
    n_new = each(lambda dc, nn, a, kk: dc * nn + jnp.sum(a * kk.astype(F32), axis=0, keepdims=True),
                 decay, n_row, a_col, k)
    return list(zip(h, c_new, n_new, m_new))


def _mlstm_kernel(*refs, chunk, n_chunks, n_seq, has_init):
    if has_init:
        q_ref, k_ref, v_ref, gt_ref, gh_ref, c0_ref, n0_ref, m0_ref, h_ref, c_ref, n_ref, m_ref = refs
        c_ref[...] = c0_ref[...]
        n_ref[...] = n0_ref[...]
        m_ref[...] = m0_ref[...]
    else:
        q_ref, k_ref, v_ref, gt_ref, gh_ref, h_ref, c_ref, n_ref, m_ref = refs
        c_ref[...] = jnp.zeros(c_ref.shape, F32)
        n_ref[...] = jnp.zeros(n_ref.shape, F32)
        m_ref[...] = jnp.zeros(m_ref.shape, F32)
    t_idx = lax.broadcasted_iota(jnp.int32, (chunk, chunk), 0)
    s_idx = lax.broadcasted_iota(jnp.int32, (chunk, chunk), 1)
    tri = s_idx <= t_idx
    eye = s_idx == t_idx
    t_len = chunk * n_chunks

    def step(c, carry):
        where, probs = [], []
        for b in range(n_seq):
            start = b * t_len + c * chunk
            rows = pl.ds(start if isinstance(start, int) else pl.multiple_of(start, chunk), chunk)
            for hh in range(H_A):
                qk_cols = slice(hh * DK_A, (hh + 1) * DK_A)
                v_cols = slice(hh * DV_A, (hh + 1) * DV_A)
                where.append((b, hh, rows, v_cols))
                probs.append(dict(q=q_ref[rows, qk_cols], k=k_ref[rows, qk_cols], v=v_ref[rows, v_cols],
                                  i_row=gt_ref[b, hh, c, 0:1, :], f_row=gt_ref[b, hh, c, 1:2, :],
                                  c=c_ref[b, hh], n=n_ref[b, hh], m=m_ref[b, hh][:, :1]))
        results = _mlstm_chunks(probs, tri, eye)
        normed = [_rms(h, gh_ref[:, v_cols]) for (h, _, _, _), (_, _, _, v_cols) in zip(results, where)]
        for (b, hh, rows, v_cols), (_, c_st, n_row, m), hn in zip(where, results, normed):
            h_ref[rows, v_cols] = hn.astype(h_ref.dtype)
            c_ref[b, hh] = c_st
            n_ref[b, hh] = n_row
            m_ref[b, hh] = jnp.broadcast_to(m, (1, LANE))
        return carry

    if n_chunks == 1:
        step(0, 0)
    else:
        lax.fori_loop(0, n_chunks, step, 0)


def _mlstm(qa, ka, va, gates_t, g_h, init, n_batch, t_len, chunk, n_seq):
    has_init = init is not None
    n_chunks = t_len // chunk
    st4 = lambda n: pl.BlockSpec((n_seq, H_A, 1, n), lambda b: (b, 0, 0, 0))
    c_spec = pl.BlockSpec((n_seq, H_A, DV_A, DK_A), lambda b: (b, 0, 0, 0))
    row = lambda n: pl.BlockSpec((n_seq * t_len, n), lambda b: (b, 0))
    in_specs = [row(H_A * DK_A), row(H_A * DK_A), row(D_MODEL),
                pl.BlockSpec((n_seq, H_A, n_chunks, 2, chunk), lambda b: (b, 0, 0, 0, 0)),
                _resident((1, D_MODEL))]
    args = [qa, ka, va, gates_t, g_h]
    if has_init:
        in_specs += [c_spec, st4(DK_A), st4(LANE)]
        args += list(init)
    return pl.pallas_call(
        functools.partial(_mlstm_kernel, chunk=chunk, n_chunks=n_chunks, n_seq=n_seq, has_init=has_init),
        grid=(n_batch // n_seq,),
        in_specs=in_specs,
        out_specs=[row(D_MODEL), c_spec, st4(DK_A), st4(LANE)],
        out_shape=[jax.ShapeDtypeStruct((n_batch * t_len, D_MODEL), BF16),
                   jax.ShapeDtypeStruct((n_batch, H_A, DV_A, DK_A), F32),
                   jax.ShapeDtypeStruct((n_batch, H_A, 1, DK_A), F32),
                   jax.ShapeDtypeStruct((n_batch, H_A, 1, LANE), F32)],
        compiler_params=_params("parallel"),
        name="mlstm_init" if has_init else "mlstm",
    )(*args)


def _attn_kernel(q_ref, k_ref, o_ref, q_scr, s_ref, mx_ref, acc_ref):
    tq = q_ref.shape[0]
    qi = pl.program_id(1)
    rows = H_B * tq
    for hh in range(H_B):
        q_scr[hh * tq:(hh + 1) * tq, :] = q_ref[:, hh * QK_PAD:(hh + 1) * QK_PAD]
    ones = jnp.ones((tq, LANE), BF16)

    def key_block(j):
        return k_ref[pl.ds(pl.multiple_of(j * tq, tq), tq), :]

    def lane_fold(s):
        return jnp.maximum(s[:, :LANE], s[:, LANE:])

    def weighted(s, m2, kj):
        return _dot(jnp.exp2(s - m2).astype(BF16), jnp.concatenate([kj[:, :KV_LORA], ones], axis=1))

    k_diag = key_block(qi)
    r_idx = lax.broadcasted_iota(jnp.int32, (H_B, tq, tq), 1).reshape(rows, tq)
    c_idx = lax.broadcasted_iota(jnp.int32, (rows, tq), 1)
    s_diag = jnp.where(c_idx <= r_idx, _dot_nt(q_scr[...], k_diag), -jnp.inf)

    mx_ref[...] = lane_fold(s_diag)

    @pl.loop(0, qi)
    def _(j):
        s = _dot_nt(q_scr[...], key_block(j))
        s_ref[j] = s
        mx_ref[...] = jnp.maximum(mx_ref[...], lane_fold(s))

    m = jnp.broadcast_to(jnp.max(mx_ref[...], axis=1, keepdims=True), (rows, LANE))
    mx_ref[...] = m
    acc_ref[...] = weighted(s_diag, jnp.concatenate([m, m], axis=1), k_diag)

    @pl.loop(0, qi)
    def _(j):
        m = mx_ref[...]
        acc_ref[...] += weighted(s_ref[j], jnp.concatenate([m, m], axis=1), key_block(j))

    o = acc_ref[:, :KV_LORA] / acc_ref[:, KV_LORA:]
    for hh in range(H_B):
        o_ref[:, hh * KV_LORA:(hh + 1) * KV_LORA] = o[hh * tq:(hh + 1) * tq].astype(o_ref.dtype)


def _attn_prompt(qcat, kcat, n_batch, t_len):
    tq = min(ATTN_TILE, t_len)
    assert tq == 2 * LANE
    nq = t_len // tq
    return pl.pallas_call(
        _attn_kernel,
        grid=(n_batch, nq),
        in_specs=[pl.BlockSpec((tq, H_B * QK_PAD), lambda b, i: (b * nq + i, 0)),
                  pl.BlockSpec((t_len, QK_PAD), lambda b, i: (b, 0))],
        out_specs=pl.BlockSpec((tq, H_B * KV_LORA), lambda b, i: (b * nq + i, 0)),
        out_shape=jax.ShapeDtypeStruct((n_batch * t_len, H_B * KV_LORA), BF16),
        scratch_shapes=[pltpu.VMEM((H_B * tq, QK_PAD), BF16),
                        pltpu.VMEM((max(nq - 1, 1), H_B * tq, tq), F32),
                        pltpu.VMEM((H_B * tq, LANE), F32),
                        pltpu.VMEM((H_B * tq, 2 * KV_LORA), F32)],
        compiler_params=_params("parallel", "parallel"),
        name="attn_prompt",
    )(qcat, kcat)


def _attn_sample_kernel(pt_ref, q_ref, kvn_ref, krn_ref, kv_hbm, kr_hbm, o_ref,
                        kv_a, kr_a, kv_b, kr_b, sem, *, pages, t_new):
    b = pl.program_id(0)
    last = b + 1 == pl.num_programs(0)
    bufs = ((kv_a, kr_a), (kv_b, kr_b))

    def half_copies(seq, half):
        kv_buf, kr_buf = bufs[half]
        copies = []
        for p in range(pages):
            page = pt_ref[seq, half * pages + p]
            span = pl.ds(p * PAGE_SIZE, PAGE_SIZE)
            copies.append(pltpu.make_async_copy(kv_hbm.at[page], kv_buf.at[span, :], sem.at[half, 0]))
            copies.append(pltpu.make_async_copy(kr_hbm.at[page], kr_buf.at[:, span], sem.at[half, 1]))
        return copies

    @pl.when(b == 0)
    def _():
        for cp in half_copies(b, 0):
            cp.start()

    q = q_ref[0]
    q_lat = q[:, :KV_LORA]
    q_rope = q[:, KV_LORA:KV_LORA + ROPE_DIM]

    def partial_softmax(s, v):
        m_blk = jnp.max(s, axis=1, keepdims=True)
        p = jnp.exp2(s - m_blk)
        return m_blk, jnp.sum(p, axis=1, keepdims=True), _dot(p.astype(BF16), v)

    def half_parts(half):
        kv_buf, kr_buf = bufs[half]
        sub = pages * PAGE_SIZE // SAMPLE_SUB_BLOCKS
        parts = []
        for i in range(SAMPLE_SUB_BLOCKS):
            kv = kv_buf[i * sub:(i + 1) * sub, :].astype(BF16)
            kr_t = kr_buf[:, i * sub:(i + 1) * sub].astype(BF16)
            parts.append(partial_softmax(_dot_nt(q_lat, kv) + _dot(q_rope, kr_t), kv))
        return parts

    for cp in half_copies(b, 0):
        cp.wait()
    for cp in half_copies(b, 1):
        cp.start()
    parts = half_parts(0)

    for cp in half_copies(b, 1):
        cp.wait()
    for cp in half_copies(jnp.where(last, 0, b + 1), 0):
        cp.start()
    parts += half_parts(1)

    kvn = kvn_ref[0].astype(BF16)
    krn = krn_ref[0].astype(BF16)
    s = _dot_nt(q_lat, kvn) + _dot_nt(q_rope, krn)
    r_tok = lax.broadcasted_iota(jnp.int32, s.shape, 0) % t_new
    c_tok = lax.broadcasted_iota(jnp.int32, s.shape, 1)
    parts.append(partial_softmax(jnp.where(c_tok <= r_tok, s, -jnp.inf), kvn))

    m = parts[0][0]
    for m_blk, _, _ in parts[1:]:
        m = jnp.maximum(m, m_blk)
    l = jnp.zeros_like(m)
    acc = jnp.zeros((q.shape[0], KV_LORA), F32)
    for m_blk, l_blk, o_blk in parts:
        w_blk = jnp.exp2(m_blk - m)
        l = l + w_blk * l_blk
        acc = acc + w_blk * o_blk
    o_ref[0] = (acc / l).astype(o_ref.dtype)

    @pl.when(last)
    def _():
        for cp in half_copies(0, 0):
            cp.wait()


def _attn_sample(page_table, q, kv_new, kr_new, cache_kv, cache_kr_t, t_new):
    n_b, n_pages = page_table.shape
    assert n_pages % 2 == 0
    pages = n_pages // 2
    half_rows = pages * PAGE_SIZE
    assert half_rows % (SAMPLE_SUB_BLOCKS * LANE) == 0
    rows = q.shape[1]
    pad = kv_new.shape[1]
    grid_spec = pltpu.PrefetchScalarGridSpec(
        num_scalar_prefetch=1,
        grid=(n_b,),
        in_specs=[pl.BlockSpec((1, rows, QK_PAD), lambda b, pt: (b, 0, 0)),
                  pl.BlockSpec((1, pad, KV_LORA), lambda b, pt: (b, 0, 0)),
                  pl.BlockSpec((1, pad, ROPE_DIM), lambda b, pt: (b, 0, 0)),
                  pl.BlockSpec(memory_space=pl.ANY),
                  pl.BlockSpec(memory_space=pl.ANY)],
        out_specs=pl.BlockSpec((1, rows, KV_LORA), lambda b, pt: (b, 0, 0)),
        scratch_shapes=[pltpu.VMEM((half_rows, KV_LORA), F32), pltpu.VMEM((ROPE_DIM, half_rows), F32),
                        pltpu.VMEM((half_rows, KV_LORA), F32), pltpu.VMEM((ROPE_DIM, half_rows), F32),
                        pltpu.SemaphoreType.DMA((2, 2))])
    return pl.pallas_call(
        functools.partial(_attn_sample_kernel, pages=pages, t_new=t_new),
        grid_spec=grid_spec,
        out_shape=jax.ShapeDtypeStruct((n_b, rows, KV_LORA), BF16),
        compiler_params=_params("arbitrary"),
        name="attn_sample",
    )(page_table, q, kv_new, kr_new, cache_kv, cache_kr_t)


def _out_ffn_kernel(x_ref, ha_ref, ga_ref, gb_ref, ol_ref, wuv_ref, wout_ref, gmix_ref,
                    gpre_ref, gpost_ref, wgu_ref, wdn_ref, o_ref, act_ref):
    merged = []
    for hh in range(H_B):
        cols = slice(hh * V_DIM, (hh + 1) * V_DIM)
        y_b = _dot(ol_ref[:, cols], wuv_ref[hh])
        y = ga_ref[:, cols].astype(F32) * ha_ref[:, cols].astype(F32) + gb_ref[:, cols].astype(F32) * y_b
        merged.append(y.astype(BF16))
    mix = _dot(jnp.concatenate(merged, axis=1), wout_ref[...])
    x = x_ref[...] + _rms(mix, gmix_ref[...])
    o_ref[...] = _ffn_body(x, gpre_ref[...], gpost_ref[...], wgu_ref, wdn_ref, act_ref)


def _out_ffn(x, ha, ga, gb, o_lat, w):
    m = x.shape[0]
    tm = min(ROW_TILE, m)
    row = pl.BlockSpec((tm, D_MODEL), lambda i: (i, 0))
    vec = _resident((1, D_MODEL))
    return pl.pallas_call(
        _out_ffn_kernel,
        grid=(m // tm,),
        in_specs=[row, row, row, row, row, _resident((H_B, KV_LORA, V_DIM)), _resident((D_MODEL, D_MODEL)), vec,
                  vec, vec, _resident((D_MODEL, 2 * D_FF)), _resident((D_FF, D_MODEL))],
        out_specs=row,
        out_shape=jax.ShapeDtypeStruct((m, D_MODEL), F32),
        scratch_shapes=[pltpu.VMEM((tm, D_FF), BF16)],
        compiler_params=_params("parallel"),
        name="out_ffn",
    )(x, ha, ga, gb, o_lat, w["w_uv"], w["w_out"], w["g_mix_post"],
      w["g_ffn2_pre"], w["g_ffn2_post"], w["w_ffn2_gu"], w["w_ffn2_down"])


def _prep_weights(norm_ffn1_pre, norm_ffn1_post, w_ffn1_gu, w_ffn1_down, norm_mix_pre, norm_mix_post,
                  w_in, b_gates, w_uq, norm_q_lat, norm_kv_lat, w_uk, w_uv, norm_mlstm_h, w_out,
                  norm_ffn2_pre, norm_ffn2_post, w_ffn2_gu, w_ffn2_down):
    half = ROPE_DIM // 2
    row = lambda g: g.reshape(1, -1).astype(F32)

    def swap_pad(wr):
        z = jnp.zeros(wr.shape[:-1] + (LANE - ROPE_DIM,), wr.dtype)
        return (jnp.concatenate([wr, z], axis=-1),
                jnp.concatenate([wr[..., half:], wr[..., :half], z], axis=-1))

    sizes = (H_A * DK_A, H_A * DK_A, H_A * DV_A, H_A, H_A, H_A * DV_A, Q_LORA, KV_LORA, ROPE_DIM, D_MODEL, D_MODEL)
    offs = [0]
    for n in sizes:
        offs.append(offs[-1] + n)
    qa, ka, va, ip, fp, oa, cq, ckv, kr, ga, gb = [w_in[:, offs[i]:offs[i + 1]] for i in range(len(sizes))]
    kr_p, kr_s = swap_pad(kr)
    gates = jnp.concatenate([ip, fp, jnp.zeros((D_MODEL, LANE - 2 * H_A), w_in.dtype)], axis=1)
    w_in2 = jnp.concatenate([qa, ka, va, oa, ga, gb, cq, ckv, kr_p, kr_s, gates], axis=1).astype(BF16)
    assert w_in2.shape[1] == IN_COLS_PAD

    uq = w_uq.reshape(Q_LORA, H_B, NOPE_DIM + ROPE_DIM)
    uq_p, uq_s = swap_pad(uq[..., NOPE_DIM:])
    w_uq2 = jnp.concatenate([uq[..., :NOPE_DIM].reshape(Q_LORA, -1), uq_p.reshape(Q_LORA, -1),
                             uq_s.reshape(Q_LORA, -1)], axis=1).astype(BF16)
    return dict(
        g_ffn1_pre=row(norm_ffn1_pre), g_ffn1_post=row(norm_ffn1_post),
        w_ffn1_gu=w_ffn1_gu.astype(BF16), w_ffn1_down=w_ffn1_down.astype(BF16),
        g_ffn2_pre=row(norm_ffn2_pre), g_ffn2_post=row(norm_ffn2_post),
        w_ffn2_gu=w_ffn2_gu.astype(BF16), w_ffn2_down=w_ffn2_down.astype(BF16),
        g_mix_pre=row(norm_mix_pre), g_mix_post=row(norm_mix_post),
        w_in=w_in2, w_uq=w_uq2,
        b_gates=jnp.concatenate([b_gates.astype(F32), jnp.zeros((LANE - 2 * H_A,), F32)]).reshape(1, LANE),
        g_q=row(norm_q_lat), g_kv=row(norm_kv_lat),
        w_uk=jnp.transpose(w_uk, (1, 2, 0)).astype(BF16),
        w_uv=jnp.transpose(w_uv, (1, 0, 2)).astype(BF16),
        g_h=row(norm_mlstm_h), w_out=w_out.astype(BF16))


def _rope_tables(pos):
    half = ROPE_DIM // 2
    inv_freq = ROPE_THETA ** (-jnp.arange(half, dtype=F32) / half)
    ang = pos.astype(F32)[:, None] * inv_freq[None, :]
    cos, sin = jnp.cos(ang), jnp.sin(ang)
    z = jnp.zeros((pos.shape[0], LANE - ROPE_DIM), F32)
    return jnp.concatenate([cos, cos, z], axis=1), jnp.concatenate([-sin, sin, z], axis=1)


def _gates_by_head(gt, n_batch, t_len, chunk):
    g = gt[:, :2 * H_A].reshape(n_batch, t_len // chunk, chunk, 2, H_A)
    return jnp.transpose(g, (0, 4, 1, 3, 2))


def kernel(x_prompt, x_sample, cache_kv_latent, cache_k_rope, state_mlstm_C, state_mlstm_n, state_mlstm_m,
           page_table, norm_ffn1_pre, norm_ffn1_post, w_ffn1_gu, w_ffn1_down, norm_mix_pre, norm_mix_post,
           w_in, b_gates, w_uq, norm_q_lat, norm_kv_lat, w_uk, w_uv, norm_mlstm_h, w_out,
           norm_ffn2_pre, norm_ffn2_post, w_ffn2_gu, w_ffn2_down):
    assert w_in.shape[0] == 1, "single-layer trunk"
    b_p, t_p, _ = x_prompt.shape
    b_s, t_s, _ = x_sample.shape
    past_len = page_table.shape[1] * PAGE_SIZE
    w = _prep_weights(norm_ffn1_pre[0], norm_ffn1_post[0], w_ffn1_gu[0], w_ffn1_down[0], norm_mix_pre[0],
                      norm_mix_post[0], w_in[0], b_gates[0], w_uq[0], norm_q_lat[0], norm_kv_lat[0], w_uk[0],
                      w_uv[0], norm_mlstm_h[0], w_out[0], norm_ffn2_pre[0], norm_ffn2_post[0], w_ffn2_gu[0],
                      w_ffn2_down[0])

    xp = x_prompt.reshape(b_p * t_p, D_MODEL)
    xp = _ffn(xp, w["g_ffn1_pre"], w["g_ffn1_post"], w["w_ffn1_gu"], w["w_ffn1_down"])
    cos_p, sin_p = _rope_tables(jnp.arange(t_p, dtype=jnp.int32))
    qa, ka, va, ga, gb, gt, ckv_p, kr_p, kcat, qcat = _proj(xp, w, cos_p, sin_p)
    chunk = min(MLSTM_CHUNK, t_p)
    ha, c_p, n_p, m_p = _mlstm(qa, ka, va, _gates_by_head(gt, b_p, t_p, chunk), w["g_h"], None, b_p, t_p, chunk, 1)
    o_lat = _attn_prompt(qcat, kcat, b_p, t_p)
    yp = _out_ffn(xp, ha, ga, gb, o_lat, w)

    t_pad = 16
    new_pad = 8
    xs = x_sample.reshape(b_s * t_s, D_MODEL)
    xs = _ffn(xs, w["g_ffn1_pre"], w["g_ffn1_post"], w["w_ffn1_gu"], w["w_ffn1_down"])
    cos_s, sin_s = _rope_tables(past_len + jnp.arange(t_s, dtype=jnp.int32))
    reps = min(ROW_TILE, b_s * t_s) // t_s
    qa, ka, va, ga, gb, gt, ckv_s, kr_s, _, qcat = _proj(xs, w, jnp.tile(cos_s, (reps, 1)), jnp.tile(sin_s, (reps, 1)))

    def pad_t(a, value=0.0):
        a = a.reshape(b_s, t_s, a.shape[-1])
        a = jnp.pad(a, ((0, 0), (0, t_pad - t_s), (0, 0)), constant_values=value)
        return a.reshape(b_s * t_pad, a.shape[-1])

    tok = jnp.arange(b_s * t_pad) % t_pad
    lane = jnp.arange(LANE)
    gt_pad = jnp.where((tok[:, None] >= t_s) & (lane[None, :] < H_A), NEG_BIG, pad_t(gt))
    m0 = jnp.broadcast_to(state_mlstm_m[0][:, :, None, None], (b_s, H_A, 1, LANE))
    ha, c_s, n_s, m_s = _mlstm(pad_t(qa), pad_t(ka), pad_t(va), _gates_by_head(gt_pad, b_s, t_pad, t_pad), w["g_h"],
                               (state_mlstm_C[0], state_mlstm_n[0][:, :, None, :], m0), b_s, t_pad, t_pad,
                               math.gcd(b_s, MLSTM_SAMPLE_SEQS))
    ha = ha.reshape(b_s, t_pad, D_MODEL)[:, :t_s].reshape(b_s * t_s, D_MODEL)

    q_s = qcat.reshape(b_s, t_s, H_B, QK_PAD).transpose(0, 2, 1, 3).reshape(b_s, H_B * t_s, QK_PAD)
    pad_new = lambda a: jnp.pad(a.reshape(b_s, t_s, -1), ((0, 0), (0, new_pad - t_s), (0, 0)))
    o_s = _attn_sample(page_table, q_s, pad_new(ckv_s), pad_new(kr_s), cache_kv_latent[0],
                       jnp.swapaxes(cache_k_rope[0], 1, 2), t_s)
    o_s = o_s.reshape(b_s, H_B, t_s, KV_LORA).transpose(0, 2, 1, 3).reshape(b_s * t_s, H_B * KV_LORA)
    ys = _out_ffn(xs, ha, ga, gb, o_s, w)

    return (yp.reshape(b_p, t_p, D_MODEL), ys.reshape(b_s, t_s, D_MODEL),
            ckv_p.reshape(1, b_p, t_p, KV_LORA), kr_p.reshape(1, b_p, t_p, ROPE_DIM),
            c_p[None], n_p[:, :, 0, :][None], m_p[:, :, 0, 0][None],
            ckv_s.reshape(1, b_s, t_s, KV_LORA), kr_s.reshape(1, b_s, t_s, ROPE_DIM),
            c_s[None], n_s[:, :, 0, :][None], m_s[:, :, 0, 0][None])
```

```python
import functools
import math

import jax
import jax.numpy as jnp
from jax import lax
from jax.experimental import pallas as pl
from jax.experimental.pallas import tpu as pltpu

F32 = jnp.float32
BF16 = jnp.bfloat16

D_MODEL = 1024
H_A = 4
DV_A = D_MODEL // H_A
DK_A = DV_A // 2
V_DIM = 128
H_B = D_MODEL // V_DIM
NOPE_DIM = 128
ROPE_DIM = 64
Q_LORA = 256
KV_LORA = 128
ROPE_THETA = 10000.0
SM_SCALE = (NOPE_DIM + ROPE_DIM) ** -0.5
Q_SCALE = SM_SCALE * math.log2(math.e)
D_FF = 2816
EPS = 1e-6
PAGE_SIZE = 128
QK_PAD = 256

LANE = 128
VMEM_LIMIT = 56 * 1024 * 1024
ROW_TILE = 512
FF_CHUNK = 256
MLSTM_CHUNK = 256
MLSTM_SAMPLE_SEQS = 4
ATTN_TILE = 256
SAMPLE_SUB_BLOCKS = 4
NEG_BIG = -1e30


def _params(*sem):
    return pltpu.CompilerParams(dimension_semantics=sem, vmem_limit_bytes=VMEM_LIMIT)


def _resident(shape):
    return pl.BlockSpec(shape, lambda *_: (0,) * len(shape), pipeline_mode=pl.Buffered(1))


def _rms(x, g):
    return x * lax.rsqrt(jnp.mean(x * x, axis=-1, keepdims=True) + EPS) * g


def _dot(a, b):
    return jnp.dot(a, b, preferred_element_type=F32)


def _dot_nt(a, b):
    return lax.dot_general(a, b, (((1,), (1,)), ((), ())), preferred_element_type=F32)


def _dot_tn(a, b):
    return lax.dot_general(a, b, (((0,), (0,)), ((), ())), preferred_element_type=F32)


def _ffn_body(x, gpre, gpost, wgu_ref, wdn_ref, act_ref):
    h = _rms(x, gpre).astype(BF16)
    for lo in range(0, D_FF, FF_CHUNK):
        g = _dot(h, wgu_ref[:, lo:lo + FF_CHUNK])
        u = _dot(h, wgu_ref[:, D_FF + lo:D_FF + lo + FF_CHUNK])
        act_ref[:, lo:lo + FF_CHUNK] = (g * jax.nn.sigmoid(g) * u).astype(BF16)
    f = _dot(act_ref[...], wdn_ref[...])
    return x + 0.5 * _rms(f, gpost)


def _ffn_kernel(x_ref, gpre_ref, gpost_ref, wgu_ref, wdn_ref, o_ref, act_ref):
    o_ref[...] = _ffn_body(x_ref[...], gpre_ref[...], gpost_ref[...], wgu_ref, wdn_ref, act_ref)


def _ffn(x, gpre, gpost, wgu, wdn):
    m = x.shape[0]
    tm = min(ROW_TILE, m)
    row = pl.BlockSpec((tm, D_MODEL), lambda i: (i, 0))
    return pl.pallas_call(
        _ffn_kernel,
        grid=(m // tm,),
        in_specs=[row, _resident((1, D_MODEL)), _resident((1, D_MODEL)),
                  _resident((D_MODEL, 2 * D_FF)), _resident((D_FF, D_MODEL))],
        out_specs=row,
        out_shape=jax.ShapeDtypeStruct((m, D_MODEL), F32),
        scratch_shapes=[pltpu.VMEM((tm, D_FF), BF16)],
        compiler_params=_params("parallel"),
        name="ffn",
    )(x, gpre, gpost, wgu, wdn)


_O_QA, _O_KA, _O_VA, _O_OA, _O_GA, _O_GB = 0, 512, 1024, 2048, 3072, 4096
_O_CQ, _O_CKV, _O_KR, _O_KRS, _O_GT = 5120, 5376, 5504, 5632, 5760
IN_COLS_PAD = 5888


def _proj_kernel(x_ref, gpre_ref, win_ref, bg_ref, gq_ref, gkv_ref, wuq_ref, wuk_ref, cos_ref, sin_ref,
                 qa_ref, ka_ref, va_ref, ga_ref, gb_ref, gt_ref, ckv_ref, kr_ref, kcat_ref, qcat_ref):
    h = _rms(x_ref[...], gpre_ref[...]).astype(BF16)

    def seg(lo, n):
        return _dot(h, win_ref[:, lo:lo + n])

    qa_ref[...] = seg(_O_QA, H_A * DK_A).astype(BF16)
    ka_ref[...] = (seg(_O_KA, H_A * DK_A) * (DK_A ** -0.5)).astype(BF16)
    va_ref[...] = seg(_O_VA, D_MODEL).astype(BF16)
    ga_ref[...] = (jax.nn.sigmoid(seg(_O_GA, D_MODEL)) * jax.nn.sigmoid(seg(_O_OA, D_MODEL))).astype(BF16)
    gb_ref[...] = jax.nn.sigmoid(seg(_O_GB, D_MODEL)).astype(BF16)

    gt = seg(_O_GT, LANE) + bg_ref[...]
    logsig = jnp.minimum(gt, 0.0) - jnp.log1p(jnp.exp(-jnp.abs(gt)))
    lane = lax.broadcasted_iota(jnp.int32, gt.shape, 1)
    gt_ref[...] = jnp.where(lane < H_A, gt, logsig)

    cos = cos_ref[...]
    sin = sin_ref[...]
    ckv = _rms(seg(_O_CKV, KV_LORA), gkv_ref[...])
    kr = seg(_O_KR, LANE) * cos + seg(_O_KRS, LANE) * sin
    ckv_ref[...] = ckv
    kr_ref[...] = kr[:, :ROPE_DIM]
    kcat_ref[:, :KV_LORA] = ckv.astype(BF16)
    kcat_ref[:, KV_LORA:] = kr.astype(BF16)

    cq = _rms(seg(_O_CQ, Q_LORA), gq_ref[...]).astype(BF16)
    q = _dot(cq, wuq_ref[...])
    for hh in range(H_B):
        q_nope = q[:, hh * LANE:(hh + 1) * LANE].astype(BF16)
        q_lat = _dot(q_nope, wuk_ref[hh])
        lo = H_B * LANE + hh * LANE
        q_rope = q[:, lo:lo + LANE] * cos + q[:, lo + H_B * LANE:lo + (H_B + 1) * LANE] * sin
        qcat_ref[:, hh * QK_PAD:hh * QK_PAD + LANE] = (q_lat * Q_SCALE).astype(BF16)
        qcat_ref[:, hh * QK_PAD + LANE:(hh + 1) * QK_PAD] = (q_rope * Q_SCALE).astype(BF16)


def _proj(x, w, cos, sin):
    m = x.shape[0]
    tm = min(ROW_TILE, m)
    ntab = cos.shape[0] // tm

    def row(n):
        return pl.BlockSpec((tm, n), lambda i: (i, 0))

    tab = pl.BlockSpec((tm, LANE), lambda i: (i % ntab, 0))
    outs = [(H_A * DK_A, BF16), (H_A * DK_A, BF16), (D_MODEL, BF16), (D_MODEL, BF16), (D_MODEL, BF16),
            (LANE, F32), (KV_LORA, F32), (ROPE_DIM, F32), (QK_PAD, BF16), (H_B * QK_PAD, BF16)]
    return pl.pallas_call(
        _proj_kernel,
        grid=(m // tm,),
        in_specs=[row(D_MODEL), _resident((1, D_MODEL)), _resident((D_MODEL, IN_COLS_PAD)),
                  _resident((1, LANE)), _resident((1, Q_LORA)), _resident((1, KV_LORA)),
                  _resident((Q_LORA, 3 * H_B * LANE)), _resident((H_B, NOPE_DIM, KV_LORA)), tab, tab],
        out_specs=[row(n) for n, _ in outs],
        out_shape=[jax.ShapeDtypeStruct((m, n), dt) for n, dt in outs],
        compiler_params=_params("parallel"),
        name="proj",
    )(x, w["g_mix_pre"], w["w_in"], w["b_gates"], w["g_q"], w["g_kv"], w["w_uq"], w["w_uk"], cos, sin)


def _mlstm_chunks(probs, tri, eye):
    ln = probs[0]["q"].shape[0]
    each = lambda fn, *lists: [fn(*a) for a in zip(*lists)]
    f_row = [p["f_row"] for p in probs]
    i_row = [p["i_row"] for p in probs]
    q = [p["q"] for p in probs]
    k = [p["k"] for p in probs]
    v = [p["v"] for p in probs]
    c_st = [p["c"] for p in probs]
    n_row = [p["n"] for p in probs]
    m = [p["m"] for p in probs]

    b_col = each(lambda f: jnp.sum(jnp.where(tri, f, 0.0), axis=1, keepdims=True), f_row)
    b_row = each(lambda b: jnp.sum(jnp.where(eye, b, 0.0), axis=0, keepdims=True), b_col)
    b_last = each(lambda b: b[:, ln - 1:ln], b_row)
    g_row = each(lambda bl, br, ir: bl - br + ir, b_last, b_row, i_row)
    m_new = each(lambda bl, mm, g: jnp.maximum(bl + mm, jnp.max(g, axis=1, keepdims=True)), b_last, m, g_row)
    a_row = each(lambda g, mn: jnp.exp(g - mn), g_row, m_new)
    decay = each(lambda bl, mm, mn: jnp.exp(bl + mm - mn), b_last, m, m_new)
    a_col = each(lambda a: jnp.sum(jnp.where(eye, a, 0.0), axis=1, keepdims=True), a_row)
    d = each(lambda bc, br, ir: jnp.where(tri, bc - br + ir, -jnp.inf), b_col, b_row, i_row)
    m_inter = each(lambda bc, mm: bc + mm, b_col, m)
    m_t = each(lambda mi, dd: jnp.maximum(mi, jnp.max(dd, axis=1, keepdims=True)), m_inter, d)
    s = each(lambda qq, kk, dd, mt: _dot_nt(qq, kk) * jnp.exp(dd - mt), q, k, d, m_t)
    inter = each(lambda mi, mt: jnp.exp(mi - mt), m_inter, m_t)
    qn = each(lambda qq, nn: jnp.sum(qq.astype(F32) * nn, axis=1, keepdims=True), q, n_row)
    den = each(lambda ss, it, x: jnp.sum(ss, axis=1, keepdims=True) + it * x, s, inter, qn)
    num = each(lambda ss, vv, it, qq, cc: _dot(ss.astype(BF16), vv) + it * _dot_nt(qq, cc.astype(BF16)),
               s, v, inter, q, c_st)
    h = each(lambda nu, de, mt: nu / jnp.maximum(jnp.abs(de), jnp.exp(-mt)), num, den, m_t)
    va = each(lambda vv, a: (vv.astype(F32) * a).astype(BF16), v, a_col)
    c_new = each(lambda dc, cc, x, kk: dc * cc + _dot_tn(x, kk), decay, c_st, va, k)
    n_new = each(lambda dc, nn, a, kk: dc * nn + jnp.sum(a * kk.astype(F32), axis=0, keepdims=True),
                 decay, n_row, a_col, k)
    return list(zip(h, c_new, n_new, m_new))


def _mlstm_kernel(*refs, chunk, n_chunks, n_seq, has_init):
    if has_init:
        q_ref, k_ref, v_ref, gt_ref, gh_ref, c0_ref, n0_ref, m0_ref, h_ref, c_ref, n_ref, m_ref = refs
        c_ref[...] = c0_ref[...]
        n_ref[...] = n0_ref[...]
        m_ref[...] = m0_ref[...]
    else:
        q_ref, k_ref, v_ref, gt_ref, gh_ref, h_ref, c_ref, n_ref, m_ref = refs
        c_ref[...] = jnp.zeros(c_ref.shape, F32)
        n_ref[...] = jnp.zeros(n_ref.shape, F32)
        m_ref[...] = jnp.zeros(m_ref.shape, F32)
    t_idx = lax.broadcasted_iota(jnp.int32, (chunk, chunk), 0)
    s_idx = lax.broadcasted_iota(jnp.int32, (chunk, chunk), 1)
    tri = s_idx <= t_idx
    eye = s_idx == t_idx
    t_len = chunk * n_chunks

    def step(c, carry):
        where, probs = [], []
        for b in range(n_seq):
            start = b * t_len + c * chunk
            rows = pl.ds(start if isinstance(start, int) else pl.multiple_of(start, chunk), chunk)
            for hh in range(H_A):
                qk_cols = slice(hh * DK_A, (hh + 1) * DK_A)
                v_cols = slice(hh * DV_A, (hh + 1) * DV_A)
                where.append((b, hh, rows, v_cols))
                probs.append(dict(q=q_ref[rows, qk_cols], k=k_ref[rows, qk_cols], v=v_ref[rows, v_cols],
                                  i_row=gt_ref[b, hh, c, 0:1, :], f_row=gt_ref[b, hh, c, 1:2, :],
                                  c=c_ref[b, hh], n=n_ref[b, hh], m=m_ref[b, hh][:, :1]))
        results = _mlstm_chunks(probs, tri, eye)
        normed = [_rms(h, gh_ref[:, v_cols]) for (h, _, _, _), (_, _, _, v_cols) in zip(results, where)]
        for (b, hh, rows, v_cols), (_, c_st, n_row, m), hn in zip(where, results, normed):
            h_ref[rows, v_cols] = hn.astype(h_ref.dtype)
            c_ref[b, hh] = c_st
            n_ref[b, hh] = n_row
            m_ref[b, hh] = jnp.broadcast_to(m, (1, LANE))
        return carry

    if n_chunks == 1:
        step(0, 0)
    else:
        lax.fori_loop(0, n_chunks, step, 0)


def _mlstm(qa, ka, va, gates_t, g_h, init, n_batch, t_len, chunk, n_seq):
    has_init = init is not None
    n_chunks = t_len // chunk
    st4 = lambda n: pl.BlockSpec((n_seq, H_A, 1, n), lambda b: (b, 0, 0, 0))
    c_spec = pl.BlockSpec((n_seq, H_A, DV_A, DK_A), lambda b: (b, 0, 0, 0))
    row = lambda n: pl.BlockSpec((n_seq * t_len, n), lambda b: (b, 0))
    in_specs = [row(H_A * DK_A), row(H_A * DK_A), row(D_MODEL),
                pl.BlockSpec((n_seq, H_A, n_chunks, 2, chunk), lambda b: (b, 0, 0, 0, 0)),
                _resident((1, D_MODEL))]
    args = [qa, ka, va, gates_t, g_h]
    if has_init:
        in_specs += [c_spec, st4(DK_A), st4(LANE)]
        args += list(init)
    return pl.pallas_call(
        functools.partial(_mlstm_kernel, chunk=chunk, n_chunks=n_chunks, n_seq=n_seq, has_init=has_init),
        grid=(n_batch // n_seq,),
        in_specs=in_specs,
        out_specs=[row(D_MODEL), c_spec, st4(DK_A), st4(LANE)],
        out_shape=[jax.ShapeDtypeStruct((n_batch * t_len, D_MODEL), BF16),
                   jax.ShapeDtypeStruct((n_batch, H_A, DV_A, DK_A), F32),
                   jax.ShapeDtypeStruct((n_batch, H_A, 1, DK_A), F32),
                   jax.ShapeDtypeStruct((n_batch, H_A, 1, LANE), F32)],
        compiler_params=_params("parallel"),
        name="mlstm_init" if has_init else "mlstm",
    )(*args)


def _attn_kernel(q_ref, k_ref, o_ref, q_scr, s_ref, mx_ref, acc_ref):
    tq = q_ref.shape[0]
    qi = pl.program_id(1)
    rows = H_B * tq
    for hh in range(H_B):
        q_scr[hh * tq:(hh + 1) * tq, :] = q_ref[:, hh * QK_PAD:(hh + 1) * QK_PAD]
    ones = jnp.ones((tq, LANE), BF16)

    def key_block(j):
        return k_ref[pl.ds(pl.multiple_of(j * tq, tq), tq), :]

    def lane_fold(s):
        return jnp.maximum(s[:, :LANE], s[:, LANE:])

    def weighted(s, m2, kj):
        return _dot(jnp.exp2(s - m2).astype(BF16), jnp.concatenate([kj[:, :KV_LORA], ones], axis=1))

    k_diag = key_block(qi)
    r_idx = lax.broadcasted_iota(jnp.int32, (H_B, tq, tq), 1).reshape(rows, tq)
    c_idx = lax.broadcasted_iota(jnp.int32, (rows, tq), 1)
    s_diag = jnp.where(c_idx <= r_idx, _dot_nt(q_scr[...], k_diag), -jnp.inf)

    mx_ref[...] = lane_fold(s_diag)

    @pl.loop(0, qi)
    def _(j):
        s = _dot_nt(q_scr[...], key_block(j))
        s_ref[j] = s
        mx_ref[...] = jnp.maximum(mx_ref[...], lane_fold(s))

    m = jnp.broadcast_to(jnp.max(mx_ref[...], axis=1, keepdims=True), (rows, LANE))
    mx_ref[...] = m
    acc_ref[...] = weighted(s_diag, jnp.concatenate([m, m], axis=1), k_diag)

    @pl.loop(0, qi)
    def _(j):
        m = mx_ref[...]
        acc_ref[...] += weighted(s_ref[j], jnp.concatenate([m, m], axis=1), key_block(j))

    o = acc_ref[:, :KV_LORA] / acc_ref[:, KV_LORA:]
    for hh in range(H_B):
        o_ref[:, hh * KV_LORA:(hh + 1) * KV_LORA] = o[hh * tq:(hh + 1) * tq].astype(o_ref.dtype)


def _attn_prompt(qcat, kcat, n_batch, t_len):
    tq = min(ATTN_TILE, t_len)
    assert tq == 2 * LANE
    nq = t_len // tq
    return pl.pallas_call(
        _attn_kernel,
        grid=(n_batch, nq),
        in_specs=[pl.BlockSpec((tq, H_B * QK_PAD), lambda b, i: (b * nq + i, 0)),
                  pl.BlockSpec((t_len, QK_PAD), lambda b, i: (b, 0))],
        out_specs=pl.BlockSpec((tq, H_B * KV_LORA), lambda b, i: (b * nq + i, 0)),
        out_shape=jax.ShapeDtypeStruct((n_batch * t_len, H_B * KV_LORA), BF16),
        scratch_shapes=[pltpu.VMEM((H_B * tq, QK_PAD), BF16),
                        pltpu.VMEM((max(nq - 1, 1), H_B * tq, tq), F32),
                        pltpu.VMEM((H_B * tq, LANE), F32),
                        pltpu.VMEM((H_B * tq, 2 * KV_LORA), F32)],
        compiler_params=_params("parallel", "parallel"),
        name="attn_prompt",
    )(qcat, kcat)


def _attn_sample_kernel(pt_ref, q_ref, kvn_ref, krn_ref, kv_hbm, kr_hbm, o_ref,
                        kv_a, kr_a, kv_b, kr_b, sem, *, pages, t_new):
    b = pl.program_id(0)
    last = b + 1 == pl.num_programs(0)
    bufs = ((kv_a, kr_a), (kv_b, kr_b))

    def half_copies(seq, half):
        kv_buf, kr_buf = bufs[half]
        copies = []
        for p in range(pages):
            page = pt_ref[seq, half * pages + p]
            span = pl.ds(p * PAGE_SIZE, PAGE_SIZE)
            copies.append(pltpu.make_async_copy(kv_hbm.at[page], kv_buf.at[span, :], sem.at[half, 0]))
            copies.append(pltpu.make_async_copy(kr_hbm.at[page], kr_buf.at[:, span], sem.at[half, 1]))
        return copies

    @pl.when(b == 0)
    def _():
        for cp in half_copies(b, 0):
            cp.start()

    q = q_ref[0]
    q_lat = q[:, :KV_LORA]
    q_rope = q[:, KV_LORA:KV_LORA + ROPE_DIM]

    def partial_softmax(s, v):
        m_blk = jnp.max(s, axis=1, keepdims=True)
        p = jnp.exp2(s - m_blk)
        return m_blk, jnp.sum(p, axis=1, keepdims=True), _dot(p.astype(BF16), v)

    def half_parts(half):
        kv_buf, kr_buf = bufs[half]
        sub = pages * PAGE_SIZE // SAMPLE_SUB_BLOCKS
        parts = []
        for i in range(SAMPLE_SUB_BLOCKS):
            kv = kv_buf[i * sub:(i + 1) * sub, :].astype(BF16)
            kr_t = kr_buf[:, i * sub:(i + 1) * sub].astype(BF16)
            parts.append(partial_softmax(_dot_nt(q_lat, kv) + _dot(q_rope, kr_t), kv))
        return parts

    for cp in half_copies(b, 0):
        cp.wait()
    for cp in half_copies(b, 1):
        cp.start()
    parts = half_parts(0)

    for cp in half_copies(jnp.where(last, 0, b + 1), 0):
        cp.start()
    for cp in half_copies(b, 1):
        cp.wait()
    parts += half_parts(1)

    kvn = kvn_ref[0].astype(BF16)
    krn = krn_ref[0].astype(BF16)
    s = _dot_nt(q_lat, kvn) + _dot_nt(q_rope, krn)
    r_tok = lax.broadcasted_iota(jnp.int32, s.shape, 0) % t_new
    c_tok = lax.broadcasted_iota(jnp.int32, s.shape, 1)
    parts.append(partial_softmax(jnp.where(c_tok <= r_tok, s, -jnp.inf), kvn))

    m = parts[0][0]
    for m_blk, _, _ in parts[1:]:
        m = jnp.maximum(m, m_blk)
    l = jnp.zeros_like(m)
    acc = jnp.zeros((q.shape[0], KV_LORA), F32)
    for m_blk, l_blk, o_blk in parts:
        w_blk = jnp.exp2(m_blk - m)
        l = l + w_blk * l_blk
        acc = acc + w_blk * o_blk
    o_ref[0] = (acc / l).astype(o_ref.dtype)

    @pl.when(last)
    def _():
        for cp in half_copies(0, 0):
            cp.wait()


def _attn_sample(page_table, q, kv_new, kr_new, cache_kv, cache_kr_t, t_new):
    n_b, n_pages = page_table.shape
    assert n_pages % 2 == 0
    pages = n_pages // 2
    half_rows = pages * PAGE_SIZE
    assert half_rows % (SAMPLE_SUB_BLOCKS * LANE) == 0
    rows = q.shape[1]
    pad = kv_new.shape[1]
    grid_spec = pltpu.PrefetchScalarGridSpec(
        num_scalar_prefetch=1,
        grid=(n_b,),
        in_specs=[pl.BlockSpec((1, rows, QK_PAD), lambda b, pt: (b, 0, 0)),
                  pl.BlockSpec((1, pad, KV_LORA), lambda b, pt: (b, 0, 0)),
                  pl.BlockSpec((1, pad, ROPE_DIM), lambda b, pt: (b, 0, 0)),
                  pl.BlockSpec(memory_space=pl.ANY),
                  pl.BlockSpec(memory_space=pl.ANY)],
        out_specs=pl.BlockSpec((1, rows, KV_LORA), lambda b, pt: (b, 0, 0)),
        scratch_shapes=[pltpu.VMEM((half_rows, KV_LORA), F32), pltpu.VMEM((ROPE_DIM, half_rows), F32),
                        pltpu.VMEM((half_rows, KV_LORA), F32), pltpu.VMEM((ROPE_DIM, half_rows), F32),
                        pltpu.SemaphoreType.DMA((2, 2))])
    return pl.pallas_call(
        functools.partial(_attn_sample_kernel, pages=pages, t_new=t_new),
        grid_spec=grid_spec,
        out_shape=jax.ShapeDtypeStruct((n_b, rows, KV_LORA), BF16),
        compiler_params=_params("arbitrary"),
        name="attn_sample",
    )(page_table, q, kv_new, kr_new, cache_kv, cache_kr_t)


def _out_ffn_kernel(x_ref, ha_ref, ga_ref, gb_ref, ol_ref, wuv_ref, wout_ref, gmix_ref,
                    gpre_ref, gpost_ref, wgu_ref, wdn_ref, o_ref, act_ref):
    merged = []
    for hh in range(H_B):
        cols = slice(hh * V_DIM, (hh + 1) * V_DIM)
        y_b = _dot(ol_ref[:, cols], wuv_ref[hh])
        y = ga_ref[:, cols].astype(F32) * ha_ref[:, cols].astype(F32) + gb_ref[:, cols].astype(F32) * y_b
        merged.append(y.astype(BF16))
    mix = _dot(jnp.concatenate(merged, axis=1), wout_ref[...])
    x = x_ref[...] + _rms(mix, gmix_ref[...])
    o_ref[...] = _ffn_body(x, gpre_ref[...], gpost_ref[...], wgu_ref, wdn_ref, act_ref)


def _out_ffn(x, ha, ga, gb, o_lat, w):
    m = x.shape[0]
    tm = min(ROW_TILE, m)
    row = pl.BlockSpec((tm, D_MODEL), lambda i: (i, 0))
    vec = _resident((1, D_MODEL))
    return pl.pallas_call(
        _out_ffn_kernel,
        grid=(m // tm,),
        in_specs=[row, row, row, row, row, _resident((H_B, KV_LORA, V_DIM)), _resident((D_MODEL, D_MODEL)), vec,
                  vec, vec, _resident((D_MODEL, 2 * D_FF)), _resident((D_FF, D_MODEL))],
        out_specs=row,
        out_shape=jax.ShapeDtypeStruct((m, D_MODEL), F32),
        scratch_shapes=[pltpu.VMEM((tm, D_FF), BF16)],
        compiler_params=_params("parallel"),
        name="out_ffn",
    )(x, ha, ga, gb, o_lat, w["w_uv"], w["w_out"], w["g_mix_post"],
      w["g_ffn2_pre"], w["g_ffn2_post"], w["w_ffn2_gu"], w["w_ffn2_down"])


def _prep_weights(norm_ffn1_pre, norm_ffn1_post, w_ffn1_gu, w_ffn1_down, norm_mix_pre, norm_mix_post,
                  w_in, b_gates, w_uq, norm_q_lat, norm_kv_lat, w_uk, w_uv, norm_mlstm_h, w_out,
                  norm_ffn2_pre, norm_ffn2_post, w_ffn2_gu, w_ffn2_down):
    half = ROPE_DIM // 2
    row = lambda g: g.reshape(1, -1).astype(F32)

    def swap_pad(wr):
        z = jnp.zeros(wr.shape[:-1] + (LANE - ROPE_DIM,), wr.dtype)
        return (jnp.concatenate([wr, z], axis=-1),
                jnp.concatenate([wr[..., half:], wr[..., :half], z], axis=-1))

    sizes = (H_A * DK_A, H_A * DK_A, H_A * DV_A, H_A, H_A, H_A * DV_A, Q_LORA, KV_LORA, ROPE_DIM, D_MODEL, D_MODEL)
    offs = [0]
    for n in sizes:
        offs.append(offs[-1] + n)
    qa, ka, va, ip, fp, oa, cq, ckv, kr, ga, gb = [w_in[:, offs[i]:offs[i + 1]] for i in range(len(sizes))]
    kr_p, kr_s = swap_pad(kr)
    gates = jnp.concatenate([ip, fp, jnp.zeros((D_MODEL, LANE - 2 * H_A), w_in.dtype)], axis=1)
    w_in2 = jnp.concatenate([qa, ka, va, oa, ga, gb, cq, ckv, kr_p, kr_s, gates], axis=1).astype(BF16)
    assert w_in2.shape[1] == IN_COLS_PAD

    uq = w_uq.reshape(Q_LORA, H_B, NOPE_DIM + ROPE_DIM)
    uq_p, uq_s = swap_pad(uq[..., NOPE_DIM:])
    w_uq2 = jnp.concatenate([uq[..., :NOPE_DIM].reshape(Q_LORA, -1), uq_p.reshape(Q_LORA, -1),
                             uq_s.reshape(Q_LORA, -1)], axis=1).astype(BF16)
    return dict(
        g_ffn1_pre=row(norm_ffn1_pre), g_ffn1_post=row(norm_ffn1_post),
        w_ffn1_gu=w_ffn1_gu.astype(BF16), w_ffn1_down=w_ffn1_down.astype(BF16),
        g_ffn2_pre=row(norm_ffn2_pre), g_ffn2_post=row(norm_ffn2_post),
        w_ffn2_gu=w_ffn2_gu.astype(BF16), w_ffn2_down=w_ffn2_down.astype(BF16),
        g_mix_pre=row(norm_mix_pre), g_mix_post=row(norm_mix_post),
        w_in=w_in2, w_uq=w_uq2,
        b_gates=jnp.concatenate([b_gates.astype(F32), jnp.zeros((LANE - 2 * H_A,), F32)]).reshape(1, LANE),
        g_q=row(norm_q_lat), g_kv=row(norm_kv_lat),
        w_uk=jnp.transpose(w_uk, (1, 2, 0)).astype(BF16),
        w_uv=jnp.transpose(w_uv, (1, 0, 2)).astype(BF16),
        g_h=row(norm_mlstm_h), w_out=w_out.astype(BF16))


def _rope_tables(pos):
    half = ROPE_DIM // 2
    inv_freq = ROPE_THETA ** (-jnp.arange(half, dtype=F32) / half)
    ang = pos.astype(F32)[:, None] * inv_freq[None, :]
    cos, sin = jnp.cos(ang), jnp.sin(ang)
    z = jnp.zeros((pos.shape[0], LANE - ROPE_DIM), F32)
    return jnp.concatenate([cos, cos, z], axis=1), jnp.concatenate([-sin, sin, z], axis=1)


def _gates_by_head(gt, n_batch, t_len, chunk):
    g = gt[:, :2 * H_A].reshape(n_batch, t_len // chunk, chunk, 2, H_A)
    return jnp.transpose(g, (0, 4, 1, 3, 2))


def kernel(x_prompt, x_sample, cache_kv_latent, cache_k_rope, state_mlstm_C, state_mlstm_n, state_mlstm_m,
           page_table, norm_ffn1_pre, norm_ffn1_post, w_ffn1_gu, w_ffn1_down, norm_mix_pre, norm_mix_post,
           w_in, b_gates, w_uq, norm_q_lat, norm_kv_lat, w_uk, w_uv, norm_mlstm_h, w_out,
           norm_ffn2_pre, norm_ffn2_post, w_ffn2_gu, w_ffn2_down):
    assert w_in.shape[0] == 1, "single-layer trunk"
    b_p, t_p, _ = x_prompt.shape
    b_s, t_s, _ = x_sample.shape
    past_len = page_table.shape[1] * PAGE_SIZE
    w = _prep_weights(norm_ffn1_pre[0], norm_ffn1_post[0], w_ffn1_gu[0], w_ffn1_down[0], norm_mix_pre[0],
                      norm_mix_post[0], w_in[0], b_gates[0], w_uq[0], norm_q_lat[0], norm_kv_lat[0], w_uk[0],
                      w_uv[0], norm_mlstm_h[0], w_out[0], norm_ffn2_pre[0], norm_ffn2_post[0], w_ffn2_gu[0],
                      w_ffn2_down[0])

    xp = x_prompt.reshape(b_p * t_p, D_MODEL)
    xp = _ffn(xp, w["g_ffn1_pre"], w["g_ffn1_post"], w["w_ffn1_gu"], w["w_ffn1_down"])
    cos_p, sin_p = _rope_tables(jnp.arange(t_p, dtype=jnp.int32))
    qa, ka, va, ga, gb, gt, ckv_p, kr_p, kcat, qcat = _proj(xp, w, cos_p, sin_p)
    chunk = min(MLSTM_CHUNK, t_p)
    ha, c_p, n_p, m_p = _mlstm(qa, ka, va, _gates_by_head(gt, b_p, t_p, chunk), w["g_h"], None, b_p, t_p, chunk, 1)
    o_lat = _attn_prompt(qcat, kcat, b_p, t_p)
    yp = _out_ffn(xp, ha, ga, gb, o_lat, w)

    t_pad = 16
    new_pad = 8
    xs = x_sample.reshape(b_s * t_s, D_MODEL)
    xs = _ffn(xs, w["g_ffn1_pre"], w["g_ffn1_post"], w["w_ffn1_gu"], w["w_ffn1_down"])
    cos_s, sin_s = _rope_tables(past_len + jnp.arange(t_s, dtype=jnp.int32))
    reps = min(ROW_TILE, b_s * t_s) // t_s
    qa, ka, va, ga, gb, gt, ckv_s, kr_s, _, qcat = _proj(xs, w, jnp.tile(cos_s, (reps, 1)), jnp.tile(sin_s, (reps, 1)))

    def pad_t(a, value=0.0):
        a = a.reshape(b_s, t_s, a.shape[-1])
        a = jnp.pad(a, ((0, 0), (0, t_pad - t_s), (0, 0)), constant_values=value)
        return a.reshape(b_s * t_pad, a.shape[-1])

    tok = jnp.arange(b_s * t_pad) % t_pad
    lane = jnp.arange(LANE)
    gt_pad = jnp.where((tok[:, None] >= t_s) & (lane[None, :] < H_A), NEG_BIG, pad_t(gt))
    m0 = jnp.broadcast_to(state_mlstm_m[0][:, :, None, None], (b_s, H_A, 1, LANE))
    ha, c_s, n_s, m_s = _mlstm(pad_t(qa), pad_t(ka), pad_t(va), _gates_by_head(gt_pad, b_s, t_pad, t_pad), w["g_h"],
                               (state_mlstm_C[0], state_mlstm_n[0][:, :, None, :], m0), b_s, t_pad, t_pad,
                               math.gcd(b_s, MLSTM_SAMPLE_SEQS))
    ha = ha.reshape(b_s, t_pad, D_MODEL)[:, :t_s].reshape(b_s * t_s, D_MODEL)

    q_s = qcat.reshape(b_s, t_s, H_B, QK_PAD).transpose(0, 2, 1, 3).reshape(b_s, H_B * t_s, QK_PAD)
    pad_new = lambda a: jnp.pad(a.reshape(b_s, t_s, -1), ((0, 0), (0, new_pad - t_s), (0, 0)))
    o_s = _attn_sample(page_table, q_s, pad_new(ckv_s), pad_new(kr_s), cache_kv_latent[0],
                       jnp.swapaxes(cache_k_rope[0], 1, 2), t_s)
    o_s = o_s.reshape(b_s, H_B, t_s, KV_LORA).transpose(0, 2, 1, 3).reshape(b_s * t_s, H_B * KV_LORA)
    ys = _out_ffn(xs, ha, ga, gb, o_s, w)

    return (yp.reshape(b_p, t_p, D_MODEL), ys.reshape(b_s, t_s, D_MODEL),
            ckv_p.reshape(1, b_p, t_p, KV_LORA), kr_p.reshape(1, b_p, t_p, ROPE_DIM),
            c_p[None], n_p[:, :, 0, :][None], m_p[:, :, 0, 0][None],
            ckv_s.reshape(1, b_s, t_s, KV_LORA), kr_s.reshape(1, b_s, t_s, ROPE_DIM),
            c_s[None], n_s[:, :, 0, :][None], m_s[:, :, 0, 0][None])
```

```python
import functools
import math

import jax
import jax.numpy as jnp
from jax import lax
from jax.experimental import pallas as pl
from jax.experimental.pallas import tpu as pltpu

F32 = jnp.float32
BF16 = jnp.bfloat16

D_MODEL = 1024
H_A = 4
DV_A = D_MODEL // H_A
DK_A = DV_A // 2
V_DIM = 128
H_B = D_MODEL // V_DIM
NOPE_DIM = 128
ROPE_DIM = 64
Q_LORA = 256
KV_LORA = 128
ROPE_THETA = 10000.0
SM_SCALE = (NOPE_DIM + ROPE_DIM) ** -0.5
Q_SCALE = SM_SCALE * math.log2(math.e)
D_FF = 2816
EPS = 1e-6
PAGE_SIZE = 128
QK_PAD = 256

LANE = 128
VMEM_LIMIT = 56 * 1024 * 1024
ROW_TILE = 512
FF_CHUNK = 256
MLSTM_CHUNK = 256
MLSTM_SAMPLE_SEQS = 4
ATTN_TILE = 256
SAMPLE_SUB_BLOCKS = 4
NEG_BIG = -1e30


def _params(*sem):
    return pltpu.CompilerParams(dimension_semantics=sem, vmem_limit_bytes=VMEM_LIMIT)


def _resident(shape):
    return pl.BlockSpec(shape, lambda *_: (0,) * len(shape), pipeline_mode=pl.Buffered(1))


def _rms(x, g):
    return x * lax.rsqrt(jnp.mean(x * x, axis=-1, keepdims=True) + EPS) * g


def _dot(a, b):
    return jnp.dot(a, b, preferred_element_type=F32)


def _dot_nt(a, b):
    return lax.dot_general(a, b, (((1,), (1,)), ((), ())), preferred_element_type=F32)


def _dot_tn(a, b):
    return lax.dot_general(a, b, (((0,), (0,)), ((), ())), preferred_element_type=F32)


def _ffn_body(x, gpre, gpost, wgu_ref, wdn_ref, act_ref):
    h = _rms(x, gpre).astype(BF16)
    for lo in range(0, D_FF, FF_CHUNK):
        g = _dot(h, wgu_ref[:, lo:lo + FF_CHUNK])
        u = _dot(h, wgu_ref[:, D_FF + lo:D_FF + lo + FF_CHUNK])
        act_ref[:, lo:lo + FF_CHUNK] = (g * jax.nn.sigmoid(g) * u).astype(BF16)
    f = _dot(act_ref[...], wdn_ref[...])
    return x + 0.5 * _rms(f, gpost)


def _ffn_kernel(x_ref, gpre_ref, gpost_ref, wgu_ref, wdn_ref, o_ref, act_ref):
    o_ref[...] = _ffn_body(x_ref[...], gpre_ref[...], gpost_ref[...], wgu_ref, wdn_ref, act_ref)


def _ffn(x, gpre, gpost, wgu, wdn):
    m = x.shape[0]
    tm = min(ROW_TILE, m)
    row = pl.BlockSpec((tm, D_MODEL), lambda i: (i, 0))
    return pl.pallas_call(
        _ffn_kernel,
        grid=(m // tm,),
        in_specs=[row, _resident((1, D_MODEL)), _resident((1, D_MODEL)),
                  _resident((D_MODEL, 2 * D_FF)), _resident((D_FF, D_MODEL))],
        out_specs=row,
        out_shape=jax.ShapeDtypeStruct((m, D_MODEL), F32),
        scratch_shapes=[pltpu.VMEM((tm, D_FF), BF16)],
        compiler_params=_params("parallel"),
        name="ffn",
    )(x, gpre, gpost, wgu, wdn)


_O_QA, _O_KA, _O_VA, _O_OA, _O_GA, _O_GB = 0, 512, 1024, 2048, 3072, 4096
_O_CQ, _O_CKV, _O_KR, _O_KRS, _O_GT = 5120, 5376, 5504, 5632, 5760
IN_COLS_PAD = 5888


def _proj_kernel(x_ref, gpre_ref, win_ref, bg_ref, gq_ref, gkv_ref, wuq_ref, wuk_ref, cos_ref, sin_ref,
                 qa_ref, ka_ref, va_ref, ga_ref, gb_ref, gt_ref, ckv_ref, kr_ref, kcat_ref, qcat_ref):
    h = _rms(x_ref[...], gpre_ref[...]).astype(BF16)

    def seg(lo, n):
        return _dot(h, win_ref[:, lo:lo + n])

    qa_ref[...] = seg(_O_QA, H_A * DK_A).astype(BF16)
    ka_ref[...] = (seg(_O_KA, H_A * DK_A) * (DK_A ** -0.5)).astype(BF16)
    va_ref[...] = seg(_O_VA, D_MODEL).astype(BF16)
    ga_ref[...] = (jax.nn.sigmoid(seg(_O_GA, D_MODEL)) * jax.nn.sigmoid(seg(_O_OA, D_MODEL))).astype(BF16)
    gb_ref[...] = jax.nn.sigmoid(seg(_O_GB, D_MODEL)).astype(BF16)

    gt = seg(_O_GT, LANE) + bg_ref[...]
    logsig = jnp.minimum(gt, 0.0) - jnp.log1p(jnp.exp(-jnp.abs(gt)))
    lane = lax.broadcasted_iota(jnp.int32, gt.shape, 1)
    gt_ref[...] = jnp.where(lane < H_A, gt, logsig)

    cos = cos_ref[...]
    sin = sin_ref[...]
    ckv = _rms(seg(_O_CKV, KV_LORA), gkv_ref[...])
    kr = seg(_O_KR, LANE) * cos + seg(_O_KRS, LANE) * sin
    ckv_ref[...] = ckv
    kr_ref[...] = kr[:, :ROPE_DIM]
    kcat_ref[:, :KV_LORA] = ckv.astype(BF16)
    kcat_ref[:, KV_LORA:] = kr.astype(BF16)

    cq = _rms(seg(_O_CQ, Q_LORA), gq_ref[...]).astype(BF16)
    q = _dot(cq, wuq_ref[...])
    q_lat2 = [_dot(q[:, pp * 2 * LANE:(pp + 1) * 2 * LANE].astype(BF16), wuk_ref[pp]) for pp in range(H_B // 2)]
    for hh in range(H_B):
        q_lat = q_lat2[hh // 2][:, (hh % 2) * LANE:(hh % 2 + 1) * LANE]
        lo = H_B * LANE + hh * LANE
        q_rope = q[:, lo:lo + LANE] * cos + q[:, lo + H_B * LANE:lo + (H_B + 1) * LANE] * sin
        qcat_ref[:, hh * QK_PAD:hh * QK_PAD + LANE] = (q_lat * Q_SCALE).astype(BF16)
        qcat_ref[:, hh * QK_PAD + LANE:(hh + 1) * QK_PAD] = (q_rope * Q_SCALE).astype(BF16)


def _proj(x, w, cos, sin):
    m = x.shape[0]
    tm = min(ROW_TILE, m)
    ntab = cos.shape[0] // tm

    def row(n):
        return pl.BlockSpec((tm, n), lambda i: (i, 0))

    tab = pl.BlockSpec((tm, LANE), lambda i: (i % ntab, 0))
    outs = [(H_A * DK_A, BF16), (H_A * DK_A, BF16), (D_MODEL, BF16), (D_MODEL, BF16), (D_MODEL, BF16),
            (LANE, F32), (KV_LORA, F32), (ROPE_DIM, F32), (QK_PAD, BF16), (H_B * QK_PAD, BF16)]
    return pl.pallas_call(
        _proj_kernel,
        grid=(m // tm,),
        in_specs=[row(D_MODEL), _resident((1, D_MODEL)), _resident((D_MODEL, IN_COLS_PAD)),
                  _resident((1, LANE)), _resident((1, Q_LORA)), _resident((1, KV_LORA)),
                  _resident((Q_LORA, 3 * H_B * LANE)), _resident((H_B // 2, 2 * NOPE_DIM, 2 * KV_LORA)), tab, tab],
        out_specs=[row(n) for n, _ in outs],
        out_shape=[jax.ShapeDtypeStruct((m, n), dt) for n, dt in outs],
        compiler_params=_params("parallel"),
        name="proj",
    )(x, w["g_mix_pre"], w["w_in"], w["b_gates"], w["g_q"], w["g_kv"], w["w_uq"], w["w_uk"], cos, sin)


def _mlstm_chunks(probs, tri, eye):
    ln = probs[0]["q"].shape[0]
    each = lambda fn, *lists: [fn(*a) for a in zip(*lists)]
    f_row = [p["f_row"] for p in probs]
    i_row = [p["i_row"] for p in probs]
    q = [p["q"] for p in probs]
    k = [p["k"] for p in probs]
    v = [p["v"] for p in probs]
    c_st = [p["c"] for p in probs]
    n_row = [p["n"] for p in probs]
    m = [p["m"] for p in probs]

    b_col = each(lambda f: jnp.sum(jnp.where(tri, f, 0.0), axis=1, keepdims=True), f_row)
    b_row = each(lambda b: jnp.sum(jnp.where(eye, b, 0.0), axis=0, keepdims=True), b_col)
    b_last = each(lambda b: b[:, ln - 1:ln], b_row)
    g_row = each(lambda bl, br, ir: bl - br + ir, b_last, b_row, i_row)
    m_new = each(lambda bl, mm, g: jnp.maximum(bl + mm, jnp.max(g, axis=1, keepdims=True)), b_last, m, g_row)
    a_row = each(lambda g, mn: jnp.exp(g - mn), g_row, m_new)
    decay = each(lambda bl, mm, mn: jnp.exp(bl + mm - mn), b_last, m, m_new)
    a_col = each(lambda a: jnp.sum(jnp.where(eye, a, 0.0), axis=1, keepdims=True), a_row)
    d = each(lambda bc, br, ir: jnp.where(tri, bc - br + ir, -jnp.inf), b_col, b_row, i_row)
    m_inter = each(lambda bc, mm: bc + mm, b_col, m)
    m_t = each(lambda mi, dd: jnp.maximum(mi, jnp.max(dd, axis=1, keepdims=True)), m_inter, d)
    s = each(lambda qq, kk, dd, mt: _dot_nt(qq, kk) * jnp.exp(dd - mt), q, k, d, m_t)
    inter = each(lambda mi, mt: jnp.exp(mi - mt), m_inter, m_t)
    qn = each(lambda qq, nn: jnp.sum(qq.astype(F32) * nn, axis=1, keepdims=True), q, n_row)
    den = each(lambda ss, it, x: jnp.sum(ss, axis=1, keepdims=True) + it * x, s, inter, qn)
    num = each(lambda ss, vv, it, qq, cc: _dot(ss.astype(BF16), vv) + it * _dot_nt(qq, cc.astype(BF16)),
               s, v, inter, q, c_st)
    h = each(lambda nu, de, mt: nu / jnp.maximum(jnp.abs(de), jnp.exp(-mt)), num, den, m_t)
    va = each(lambda vv, a: (vv.astype(F32) * a).astype(BF16), v, a_col)
    c_new = each(lambda dc, cc, x, kk: dc * cc + _dot_tn(x, kk), decay, c_st, va, k)
    n_new = each(lambda dc, nn, a, kk: dc * nn + jnp.sum(a * kk.astype(F32), axis=0, keepdims=True),
                 decay, n_row, a_col, k)
    return list(zip(h, c_new, n_new, m_new))


def _mlstm_kernel(*refs, chunk, n_chunks, n_seq, has_init):
    if has_init:
        q_ref, k_ref, v_ref, gt_ref, gh_ref, c0_ref, n0_ref, m0_ref, h_ref, c_ref, n_ref, m_ref = refs
        c_ref[...] = c0_ref[...]
        n_ref[...] = n0_ref[...]
        m_ref[...] = m0_ref[...]
    else:
        q_ref, k_ref, v_ref, gt_ref, gh_ref, h_ref, c_ref, n_ref, m_ref = refs
        c_ref[...] = jnp.zeros(c_ref.shape, F32)
        n_ref[...] = jnp.zeros(n_ref.shape, F32)
        m_ref[...] = jnp.zeros(m_ref.shape, F32)
    t_idx = lax.broadcasted_iota(jnp.int32, (chunk, chunk), 0)
    s_idx = lax.broadcasted_iota(jnp.int32, (chunk, chunk), 1)
    tri = s_idx <= t_idx
    eye = s_idx == t_idx
    t_len = chunk * n_chunks

    def step(c, carry):
        where, probs = [], []
        for b in range(n_seq):
            start = b * t_len + c * chunk
            rows = pl.ds(start if isinstance(start, int) else pl.multiple_of(start, chunk), chunk)
            for hh in range(H_A):
                qk_cols = slice(hh * DK_A, (hh + 1) * DK_A)
                v_cols = slice(hh * DV_A, (hh + 1) * DV_A)
                where.append((b, hh, rows, v_cols))
                probs.append(dict(q=q_ref[rows, qk_cols], k=k_ref[rows, qk_cols], v=v_ref[rows, v_cols],
                                  i_row=gt_ref[b, hh, c, 0:1, :], f_row=gt_ref[b, hh, c, 1:2, :],
                                  c=c_ref[b, hh], n=n_ref[b, hh], m=m_ref[b, hh][:, :1]))
        results = _mlstm_chunks(probs, tri, eye)
        normed = [_rms(h, gh_ref[:, v_cols]) for (h, _, _, _), (_, _, _, v_cols) in zip(results, where)]
        for (b, hh, rows, v_cols), (_, c_st, n_row, m), hn in zip(where, results, normed):
            h_ref[rows, v_cols] = hn.astype(h_ref.dtype)
            c_ref[b, hh] = c_st
            n_ref[b, hh] = n_row
            m_ref[b, hh] = jnp.broadcast_to(m, (1, LANE))
        return carry

    if n_chunks == 1:
        step(0, 0)
    else:
        lax.fori_loop(0, n_chunks, step, 0)


def _mlstm(qa, ka, va, gates_t, g_h, init, n_batch, t_len, chunk, n_seq):
    has_init = init is not None
    n_chunks = t_len // chunk
    st4 = lambda n: pl.BlockSpec((n_seq, H_A, 1, n), lambda b: (b, 0, 0, 0))
    c_spec = pl.BlockSpec((n_seq, H_A, DV_A, DK_A), lambda b: (b, 0, 0, 0))
    row = lambda n: pl.BlockSpec((n_seq * t_len, n), lambda b: (b, 0))
    in_specs = [row(H_A * DK_A), row(H_A * DK_A), row(D_MODEL),
                pl.BlockSpec((n_seq, H_A, n_chunks, 2, chunk), lambda b: (b, 0, 0, 0, 0)),
                _resident((1, D_MODEL))]
    args = [qa, ka, va, gates_t, g_h]
    if has_init:
        in_specs += [c_spec, st4(DK_A), st4(LANE)]
        args += list(init)
    return pl.pallas_call(
        functools.partial(_mlstm_kernel, chunk=chunk, n_chunks=n_chunks, n_seq=n_seq, has_init=has_init),
        grid=(n_batch // n_seq,),
        in_specs=in_specs,
        out_specs=[row(D_MODEL), c_spec, st4(DK_A), st4(LANE)],
        out_shape=[jax.ShapeDtypeStruct((n_batch * t_len, D_MODEL), BF16),
                   jax.ShapeDtypeStruct((n_batch, H_A, DV_A, DK_A), F32),
                   jax.ShapeDtypeStruct((n_batch, H_A, 1, DK_A), F32),
                   jax.ShapeDtypeStruct((n_batch, H_A, 1, LANE), F32)],
        compiler_params=_params("parallel"),
        name="mlstm_init" if has_init else "mlstm",
    )(*args)


def _attn_kernel(q_ref, k_ref, o_ref, q_scr, s_ref, mx_ref, acc_ref):
    tq = q_ref.shape[0]
    qi = pl.program_id(1)
    rows = H_B * tq
    for hh in range(H_B):
        q_scr[hh * tq:(hh + 1) * tq, :] = q_ref[:, hh * QK_PAD:(hh + 1) * QK_PAD]
    ones = jnp.ones((tq, LANE), BF16)

    def key_block(j):
        return k_ref[pl.ds(pl.multiple_of(j * tq, tq), tq), :]

    def lane_fold(s):
        return jnp.maximum(s[:, :LANE], s[:, LANE:])

    def weighted(s, m2, kj):
        return _dot(jnp.exp2(s - m2).astype(BF16), jnp.concatenate([kj[:, :KV_LORA], ones], axis=1))

    k_diag = key_block(qi)
    r_idx = lax.broadcasted_iota(jnp.int32, (H_B, tq, tq), 1).reshape(rows, tq)
    c_idx = lax.broadcasted_iota(jnp.int32, (rows, tq), 1)
    s_diag = jnp.where(c_idx <= r_idx, _dot_nt(q_scr[...], k_diag), -jnp.inf)

    mx_ref[...] = lane_fold(s_diag)

    @pl.loop(0, qi)
    def _(j):
        s = _dot_nt(q_scr[...], key_block(j))
        s_ref[j] = s
        mx_ref[...] = jnp.maximum(mx_ref[...], lane_fold(s))

    m = jnp.broadcast_to(jnp.max(mx_ref[...], axis=1, keepdims=True), (rows, LANE))
    mx_ref[...] = m
    acc_ref[...] = weighted(s_diag, jnp.concatenate([m, m], axis=1), k_diag)

    @pl.loop(0, qi)
    def _(j):
        m = mx_ref[...]
        acc_ref[...] += weighted(s_ref[j], jnp.concatenate([m, m], axis=1), key_block(j))

    o = acc_ref[:, :KV_LORA] / acc_ref[:, KV_LORA:]
    for hh in range(H_B):
        o_ref[:, hh * KV_LORA:(hh + 1) * KV_LORA] = o[hh * tq:(hh + 1) * tq].astype(o_ref.dtype)


def _attn_prompt(qcat, kcat, n_batch, t_len):
    tq = min(ATTN_TILE, t_len)
    assert tq == 2 * LANE
    nq = t_len // tq
    return pl.pallas_call(
        _attn_kernel,
        grid=(n_batch, nq),
        in_specs=[pl.BlockSpec((tq, H_B * QK_PAD), lambda b, i: (b * nq + i, 0)),
                  pl.BlockSpec((t_len, QK_PAD), lambda b, i: (b, 0))],
        out_specs=pl.BlockSpec((tq, H_B * KV_LORA), lambda b, i: (b * nq + i, 0)),
        out_shape=jax.ShapeDtypeStruct((n_batch * t_len, H_B * KV_LORA), BF16),
        scratch_shapes=[pltpu.VMEM((H_B * tq, QK_PAD), BF16),
                        pltpu.VMEM((max(nq - 1, 1), H_B * tq, tq), F32),
                        pltpu.VMEM((H_B * tq, LANE), F32),
                        pltpu.VMEM((H_B * tq, 2 * KV_LORA), F32)],
        compiler_params=_params("parallel", "parallel"),
        name="attn_prompt",
    )(qcat, kcat)


def _attn_sample_kernel(pt_ref, q_ref, kvn_ref, krn_ref, kv_hbm, kr_hbm, o_ref,
                        kv_a, kr_a, kv_b, kr_b, sem, *, pages, t_new):
    b = pl.program_id(0)
    last = b + 1 == pl.num_programs(0)
    bufs = ((kv_a, kr_a), (kv_b, kr_b))

    def half_copies(seq, half):
        kv_buf, kr_buf = bufs[half]
        copies = []
        for p in range(pages):
            page = pt_ref[seq, half * pages + p]
            span = pl.ds(p * PAGE_SIZE, PAGE_SIZE)
            copies.append(pltpu.make_async_copy(kv_hbm.at[page], kv_buf.at[span, :], sem.at[half, 0]))
            copies.append(pltpu.make_async_copy(kr_hbm.at[page], kr_buf.at[p], sem.at[half, 1]))
        return copies

    @pl.when(b == 0)
    def _():
        for cp in half_copies(b, 0):
            cp.start()

    q = q_ref[0]
    q_lat = q[:, :KV_LORA]
    q_rope = q[:, KV_LORA:KV_LORA + ROPE_DIM]

    def partial_softmax(s, v):
        m_blk = jnp.max(s, axis=1, keepdims=True)
        p = jnp.exp2(s - m_blk)
        return m_blk, jnp.sum(p, axis=1, keepdims=True), _dot(p.astype(BF16), v)

    def half_parts(half):
        kv_buf, kr_buf = bufs[half]
        sub_pages = pages // SAMPLE_SUB_BLOCKS
        sub = sub_pages * PAGE_SIZE
        parts = []
        for i in range(SAMPLE_SUB_BLOCKS):
            kv = kv_buf[i * sub:(i + 1) * sub, :].astype(BF16)
            kr_t = jnp.concatenate([kr_buf[p].astype(BF16) for p in range(i * sub_pages, (i + 1) * sub_pages)],
                                   axis=1)
            parts.append(partial_softmax(_dot_nt(q_lat, kv) + _dot(q_rope, kr_t), kv))
        return parts

    for cp in half_copies(b, 0):
        cp.wait()
    for cp in half_copies(b, 1):
        cp.start()
    parts = half_parts(0)

    for cp in half_copies(jnp.where(last, 0, b + 1), 0):
        cp.start()
    for cp in half_copies(b, 1):
        cp.wait()
    parts += half_parts(1)

    kvn = kvn_ref[0].astype(BF16)
    krn = krn_ref[0].astype(BF16)
    s = _dot_nt(q_lat, kvn) + _dot_nt(q_rope, krn)
    r_tok = lax.broadcasted_iota(jnp.int32, s.shape, 0) % t_new
    c_tok = lax.broadcasted_iota(jnp.int32, s.shape, 1)
    parts.append(partial_softmax(jnp.where(c_tok <= r_tok, s, -jnp.inf), kvn))

    m = parts[0][0]
    for m_blk, _, _ in parts[1:]:
        m = jnp.maximum(m, m_blk)
    l = jnp.zeros_like(m)
    acc = jnp.zeros((q.shape[0], KV_LORA), F32)
    for m_blk, l_blk, o_blk in parts:
        w_blk = jnp.exp2(m_blk - m)
        l = l + w_blk * l_blk
        acc = acc + w_blk * o_blk
    o_ref[0] = (acc / l).astype(o_ref.dtype)

    @pl.when(last)
    def _():
        for cp in half_copies(0, 0):
            cp.wait()


def _attn_sample(page_table, q, kv_new, kr_new, cache_kv, cache_kr_t, t_new):
    n_b, n_pages = page_table.shape
    assert n_pages % 2 == 0
    pages = n_pages // 2
    half_rows = pages * PAGE_SIZE
    assert pages % SAMPLE_SUB_BLOCKS == 0
    rows = q.shape[1]
    pad = kv_new.shape[1]
    grid_spec = pltpu.PrefetchScalarGridSpec(
        num_scalar_prefetch=1,
        grid=(n_b,),
        in_specs=[pl.BlockSpec((1, rows, QK_PAD), lambda b, pt: (b, 0, 0)),
                  pl.BlockSpec((1, pad, KV_LORA), lambda b, pt: (b, 0, 0)),
                  pl.BlockSpec((1, pad, ROPE_DIM), lambda b, pt: (b, 0, 0)),
                  pl.BlockSpec(memory_space=pl.ANY),
                  pl.BlockSpec(memory_space=pl.ANY)],
        out_specs=pl.BlockSpec((1, rows, KV_LORA), lambda b, pt: (b, 0, 0)),
        scratch_shapes=[pltpu.VMEM((half_rows, KV_LORA), F32), pltpu.VMEM((pages, ROPE_DIM, PAGE_SIZE), F32),
                        pltpu.VMEM((half_rows, KV_LORA), F32), pltpu.VMEM((pages, ROPE_DIM, PAGE_SIZE), F32),
                        pltpu.SemaphoreType.DMA((2, 2))])
    return pl.pallas_call(
        functools.partial(_attn_sample_kernel, pages=pages, t_new=t_new),
        grid_spec=grid_spec,
        out_shape=jax.ShapeDtypeStruct((n_b, rows, KV_LORA), BF16),
        compiler_params=_params("arbitrary"),
        name="attn_sample",
    )(page_table, q, kv_new, kr_new, cache_kv, cache_kr_t)


def _out_ffn_kernel(x_ref, ha_ref, ga_ref, gb_ref, ol_ref, wuv_ref, wout_ref, gmix_ref,
                    gpre_ref, gpost_ref, wgu_ref, wdn_ref, o_ref, act_ref):
    merged = []
    for pp in range(H_B // 2):
        cols = slice(pp * 2 * V_DIM, (pp + 1) * 2 * V_DIM)
        y_b = _dot(ol_ref[:, cols], wuv_ref[pp])
        y = ga_ref[:, cols].astype(F32) * ha_ref[:, cols].astype(F32) + gb_ref[:, cols].astype(F32) * y_b
        merged.append(y.astype(BF16))
    mix = _dot(jnp.concatenate(merged, axis=1), wout_ref[...])
    x = x_ref[...] + _rms(mix, gmix_ref[...])
    o_ref[...] = _ffn_body(x, gpre_ref[...], gpost_ref[...], wgu_ref, wdn_ref, act_ref)


def _out_ffn(x, ha, ga, gb, o_lat, w):
    m = x.shape[0]
    tm = min(ROW_TILE, m)
    row = pl.BlockSpec((tm, D_MODEL), lambda i: (i, 0))
    vec = _resident((1, D_MODEL))
    return pl.pallas_call(
        _out_ffn_kernel,
        grid=(m // tm,),
        in_specs=[row, row, row, row, row, _resident((H_B // 2, 2 * KV_LORA, 2 * V_DIM)),
                  _resident((D_MODEL, D_MODEL)), vec,
                  vec, vec, _resident((D_MODEL, 2 * D_FF)), _resident((D_FF, D_MODEL))],
        out_specs=row,
        out_shape=jax.ShapeDtypeStruct((m, D_MODEL), F32),
        scratch_shapes=[pltpu.VMEM((tm, D_FF), BF16)],
        compiler_params=_params("parallel"),
        name="out_ffn",
    )(x, ha, ga, gb, o_lat, w["w_uv"], w["w_out"], w["g_mix_post"],
      w["g_ffn2_pre"], w["g_ffn2_post"], w["w_ffn2_gu"], w["w_ffn2_down"])


def _prep_weights(norm_ffn1_pre, norm_ffn1_post, w_ffn1_gu, w_ffn1_down, norm_mix_pre, norm_mix_post,
                  w_in, b_gates, w_uq, norm_q_lat, norm_kv_lat, w_uk, w_uv, norm_mlstm_h, w_out,
                  norm_ffn2_pre, norm_ffn2_post, w_ffn2_gu, w_ffn2_down):
    half = ROPE_DIM // 2
    row = lambda g: g.reshape(1, -1).astype(F32)

    def swap_pad(wr):
        z = jnp.zeros(wr.shape[:-1] + (LANE - ROPE_DIM,), wr.dtype)
        return (jnp.concatenate([wr, z], axis=-1),
                jnp.concatenate([wr[..., half:], wr[..., :half], z], axis=-1))

    sizes = (H_A * DK_A, H_A * DK_A, H_A * DV_A, H_A, H_A, H_A * DV_A, Q_LORA, KV_LORA, ROPE_DIM, D_MODEL, D_MODEL)
    offs = [0]
    for n in sizes:
        offs.append(offs[-1] + n)
    qa, ka, va, ip, fp, oa, cq, ckv, kr, ga, gb = [w_in[:, offs[i]:offs[i + 1]] for i in range(len(sizes))]
    kr_p, kr_s = swap_pad(kr)
    gates = jnp.concatenate([ip, fp, jnp.zeros((D_MODEL, LANE - 2 * H_A), w_in.dtype)], axis=1)
    w_in2 = jnp.concatenate([qa, ka, va, oa, ga, gb, cq, ckv, kr_p, kr_s, gates], axis=1).astype(BF16)
    assert w_in2.shape[1] == IN_COLS_PAD

    uq = w_uq.reshape(Q_LORA, H_B, NOPE_DIM + ROPE_DIM)
    uq_p, uq_s = swap_pad(uq[..., NOPE_DIM:])
    w_uq2 = jnp.concatenate([uq[..., :NOPE_DIM].reshape(Q_LORA, -1), uq_p.reshape(Q_LORA, -1),
                             uq_s.reshape(Q_LORA, -1)], axis=1).astype(BF16)
    return dict(
        g_ffn1_pre=row(norm_ffn1_pre), g_ffn1_post=row(norm_ffn1_post),
        w_ffn1_gu=w_ffn1_gu.astype(BF16), w_ffn1_down=w_ffn1_down.astype(BF16),
        g_ffn2_pre=row(norm_ffn2_pre), g_ffn2_post=row(norm_ffn2_post),
        w_ffn2_gu=w_ffn2_gu.astype(BF16), w_ffn2_down=w_ffn2_down.astype(BF16),
        g_mix_pre=row(norm_mix_pre), g_mix_post=row(norm_mix_post),
        w_in=w_in2, w_uq=w_uq2,
        b_gates=jnp.concatenate([b_gates.astype(F32), jnp.zeros((LANE - 2 * H_A,), F32)]).reshape(1, LANE),
        g_q=row(norm_q_lat), g_kv=row(norm_kv_lat),
        w_uk=_pair_block_diag(jnp.transpose(w_uk, (1, 2, 0)).astype(BF16)),
        w_uv=_pair_block_diag(jnp.transpose(w_uv, (1, 0, 2)).astype(BF16)),
        g_h=row(norm_mlstm_h), w_out=w_out.astype(BF16))


def _pair_block_diag(w):
    h, a, b = w.shape
    w = w.reshape(h // 2, 2, a, b)
    z = jnp.zeros((h // 2, a, b), w.dtype)
    top = jnp.concatenate([w[:, 0], z], axis=2)
    bottom = jnp.concatenate([z, w[:, 1]], axis=2)
    return jnp.concatenate([top, bottom], axis=1)


def _rope_tables(pos):
    half = ROPE_DIM // 2
    inv_freq = ROPE_THETA ** (-jnp.arange(half, dtype=F32) / half)
    ang = pos.astype(F32)[:, None] * inv_freq[None, :]
    cos, sin = jnp.cos(ang), jnp.sin(ang)
    z = jnp.zeros((pos.shape[0], LANE - ROPE_DIM), F32)
    return jnp.concatenate([cos, cos, z], axis=1), jnp.concatenate([-sin, sin, z], axis=1)


def _gates_by_head(gt, n_batch, t_len, chunk):
    g = gt[:, :2 * H_A].reshape(n_batch, t_len // chunk, chunk, 2, H_A)
    return jnp.transpose(g, (0, 4, 1, 3, 2))


def kernel(x_prompt, x_sample, cache_kv_latent, cache_k_rope, state_mlstm_C, state_mlstm_n, state_mlstm_m,
           page_table, norm_ffn1_pre, norm_ffn1_post, w_ffn1_gu, w_ffn1_down, norm_mix_pre, norm_mix_post,
           w_in, b_gates, w_uq, norm_q_lat, norm_kv_lat, w_uk, w_uv, norm_mlstm_h, w_out,
           norm_ffn2_pre, norm_ffn2_post, w_ffn2_gu, w_ffn2_down):
    assert w_in.shape[0] == 1, "single-layer trunk"
    b_p, t_p, _ = x_prompt.shape
    b_s, t_s, _ = x_sample.shape
    past_len = page_table.shape[1] * PAGE_SIZE
    w = _prep_weights(norm_ffn1_pre[0], norm_ffn1_post[0], w_ffn1_gu[0], w_ffn1_down[0], norm_mix_pre[0],
                      norm_mix_post[0], w_in[0], b_gates[0], w_uq[0], norm_q_lat[0], norm_kv_lat[0], w_uk[0],
                      w_uv[0], norm_mlstm_h[0], w_out[0], norm_ffn2_pre[0], norm_ffn2_post[0], w_ffn2_gu[0],
                      w_ffn2_down[0])

    xp = x_prompt.reshape(b_p * t_p, D_MODEL)
    xp = _ffn(xp, w["g_ffn1_pre"], w["g_ffn1_post"], w["w_ffn1_gu"], w["w_ffn1_down"])
    cos_p, sin_p = _rope_tables(jnp.arange(t_p, dtype=jnp.int32))
    qa, ka, va, ga, gb, gt, ckv_p, kr_p, kcat, qcat = _proj(xp, w, cos_p, sin_p)
    chunk = min(MLSTM_CHUNK, t_p)
    ha, c_p, n_p, m_p = _mlstm(qa, ka, va, _gates_by_head(gt, b_p, t_p, chunk), w["g_h"], None, b_p, t_p, chunk, 1)
    o_lat = _attn_prompt(qcat, kcat, b_p, t_p)
    yp = _out_ffn(xp, ha, ga, gb, o_lat, w)

    t_pad = 16
    new_pad = 8
    xs = x_sample.reshape(b_s * t_s, D_MODEL)
    xs = _ffn(xs, w["g_ffn1_pre"], w["g_ffn1_post"], w["w_ffn1_gu"], w["w_ffn1_down"])
    cos_s, sin_s = _rope_tables(past_len + jnp.arange(t_s, dtype=jnp.int32))
    reps = min(ROW_TILE, b_s * t_s) // t_s
    qa, ka, va, ga, gb, gt, ckv_s, kr_s, _, qcat = _proj(xs, w, jnp.tile(cos_s, (reps, 1)), jnp.tile(sin_s, (reps, 1)))

    def pad_t(a, value=0.0):
        a = a.reshape(b_s, t_s, a.shape[-1])
        a = jnp.pad(a, ((0, 0), (0, t_pad - t_s), (0, 0)), constant_values=value)
        return a.reshape(b_s * t_pad, a.shape[-1])

    tok = jnp.arange(b_s * t_pad) % t_pad
    lane = jnp.arange(LANE)
    gt_pad = jnp.where((tok[:, None] >= t_s) & (lane[None, :] < H_A), NEG_BIG, pad_t(gt))
    m0 = jnp.broadcast_to(state_mlstm_m[0][:, :, None, None], (b_s, H_A, 1, LANE))
    ha, c_s, n_s, m_s = _mlstm(pad_t(qa), pad_t(ka), pad_t(va), _gates_by_head(gt_pad, b_s, t_pad, t_pad), w["g_h"],
                               (state_mlstm_C[0], state_mlstm_n[0][:, :, None, :], m0), b_s, t_pad, t_pad,
                               math.gcd(b_s, MLSTM_SAMPLE_SEQS))
    ha = ha.reshape(b_s, t_pad, D_MODEL)[:, :t_s].reshape(b_s * t_s, D_MODEL)

    q_s = qcat.reshape(b_s, t_s, H_B, QK_PAD).transpose(0, 2, 1, 3).reshape(b_s, H_B * t_s, QK_PAD)
    pad_new = lambda a: jnp.pad(a.reshape(b_s, t_s, -1), ((0, 0), (0, new_pad - t_s), (0, 0)))
    o_s = _attn_sample(page_table, q_s, pad_new(ckv_s), pad_new(kr_s), cache_kv_latent[0],
                       jnp.swapaxes(cache_k_rope[0], 1, 2), t_s)
    o_s = o_s.reshape(b_s, H_B, t_s, KV_LORA).transpose(0, 2, 1, 3).reshape(b_s * t_s, H_B * KV_LORA)
    ys = _out_ffn(xs, ha, ga, gb, o_s, w)

    return (yp.reshape(b_p, t_p, D_MODEL), ys.reshape(b_s, t_s, D_MODEL),
            ckv_p.reshape(1, b_p, t_p, KV_LORA), kr_p.reshape(1, b_p, t_p, ROPE_DIM),
            c_p[None], n_p[:, :, 0, :][None], m_p[:, :, 0, 0][None],
            ckv_s.reshape(1, b_s, t_s, KV_LORA), kr_s.reshape(1, b_s, t_s, ROPE_DIM),
            c_s[None], n_s[:, :, 0, :][None], m_s[:, :, 0, 0][None])
```

```python
import functools
import math

import jax
import jax.numpy as jnp
from jax import lax
from jax.experimental import pallas as pl
from jax.experimental.pallas import tpu as pltpu

F32 = jnp.float32
BF16 = jnp.bfloat16

D_MODEL = 1024
H_A = 4
DV_A = D_MODEL // H_A
DK_A = DV_A // 2
V_DIM = 128
H_B = D_MODEL // V_DIM
NOPE_DIM = 128
ROPE_DIM = 64
Q_LORA = 256
KV_LORA = 128
ROPE_THETA = 10000.0
SM_SCALE = (NOPE_DIM + ROPE_DIM) ** -0.5
Q_SCALE = SM_SCALE * math.log2(math.e)
D_FF = 2816
EPS = 1e-6
PAGE_SIZE = 128
QK_PAD = 256

LANE = 128
VMEM_LIMIT = 56 * 1024 * 1024
ROW_TILE = 512
FF_CHUNK = 256
MLSTM_CHUNK = 256
MLSTM_SAMPLE_SEQS = 4
ATTN_TILE = 256
SAMPLE_SUB_BLOCKS = 4
NEG_BIG = -1e30


def _params(*sem):
    return pltpu.CompilerParams(dimension_semantics=sem, vmem_limit_bytes=VMEM_LIMIT)


def _resident(shape):
    return pl.BlockSpec(shape, lambda *_: (0,) * len(shape), pipeline_mode=pl.Buffered(1))


def _rms(x, g):
    return x * lax.rsqrt(jnp.mean(x * x, axis=-1, keepdims=True) + EPS) * g


def _dot(a, b):
    return jnp.dot(a, b, preferred_element_type=F32)


def _dot_nt(a, b):
    return lax.dot_general(a, b, (((1,), (1,)), ((), ())), preferred_element_type=F32)


def _dot_tn(a, b):
    return lax.dot_general(a, b, (((0,), (0,)), ((), ())), preferred_element_type=F32)


def _ffn_body(x, gpre, gpost, wgu_ref, wdn_ref, act_ref):
    h = _rms(x, gpre).astype(BF16)
    for lo in range(0, D_FF, FF_CHUNK):
        g = _dot(h, wgu_ref[:, lo:lo + FF_CHUNK])
        u = _dot(h, wgu_ref[:, D_FF + lo:D_FF + lo + FF_CHUNK])
        act_ref[:, lo:lo + FF_CHUNK] = (g * jax.nn.sigmoid(g) * u).astype(BF16)
    f = _dot(act_ref[...], wdn_ref[...])
    return x + 0.5 * _rms(f, gpost)


def _ffn_kernel(x_ref, gpre_ref, gpost_ref, wgu_ref, wdn_ref, o_ref, act_ref):
    o_ref[...] = _ffn_body(x_ref[...], gpre_ref[...], gpost_ref[...], wgu_ref, wdn_ref, act_ref)


def _ffn(x, gpre, gpost, wgu, wdn):
    m = x.shape[0]
    tm = min(ROW_TILE, m)
    row = pl.BlockSpec((tm, D_MODEL), lambda i: (i, 0))
    return pl.pallas_call(
        _ffn_kernel,
        grid=(m // tm,),
        in_specs=[row, _resident((1, D_MODEL)), _resident((1, D_MODEL)),
                  _resident((D_MODEL, 2 * D_FF)), _resident((D_FF, D_MODEL))],
        out_specs=row,
        out_shape=jax.ShapeDtypeStruct((m, D_MODEL), F32),
        scratch_shapes=[pltpu.VMEM((tm, D_FF), BF16)],
        compiler_params=_params("parallel"),
        name="ffn",
    )(x, gpre, gpost, wgu, wdn)


_O_QA, _O_KA, _O_VA, _O_OA, _O_GA, _O_GB = 0, 512, 1024, 2048, 3072, 4096
_O_CQ, _O_CKV, _O_KR, _O_KRS, _O_GT = 5120, 5376, 5504, 5632, 5760
IN_COLS_PAD = 5888


def _proj_kernel(x_ref, gpre_ref, win_ref, bg_ref, gq_ref, gkv_ref, wuq_ref, wuk_ref, cos_ref, sin_ref,
                 qa_ref, ka_ref, va_ref, ga_ref, gb_ref, gt_ref, ckv_ref, kr_ref, kcat_ref, qcat_ref):
    h = _rms(x_ref[...], gpre_ref[...]).astype(BF16)

    def seg(lo, n):
        return _dot(h, win_ref[:, lo:lo + n])

    qa_ref[...] = seg(_O_QA, H_A * DK_A).astype(BF16)
    ka_ref[...] = (seg(_O_KA, H_A * DK_A) * (DK_A ** -0.5)).astype(BF16)
    va_ref[...] = seg(_O_VA, D_MODEL).astype(BF16)
    ga_ref[...] = (jax.nn.sigmoid(seg(_O_GA, D_MODEL)) * jax.nn.sigmoid(seg(_O_OA, D_MODEL))).astype(BF16)
    gb_ref[...] = jax.nn.sigmoid(seg(_O_GB, D_MODEL)).astype(BF16)

    gt = seg(_O_GT, LANE) + bg_ref[...]
    logsig = jnp.minimum(gt, 0.0) - jnp.log1p(jnp.exp(-jnp.abs(gt)))
    lane = lax.broadcasted_iota(jnp.int32, gt.shape, 1)
    gt_ref[...] = jnp.where(lane < H_A, gt, logsig)

    cos = cos_ref[...]
    sin = sin_ref[...]
    ckv = _rms(seg(_O_CKV, KV_LORA), gkv_ref[...])
    kr = seg(_O_KR, LANE) * cos + seg(_O_KRS, LANE) * sin
    ckv_ref[...] = ckv
    kr_ref[...] = kr[:, :ROPE_DIM]
    kcat_ref[:, :KV_LORA] = ckv.astype(BF16)
    kcat_ref[:, KV_LORA:] = kr.astype(BF16)

    cq = _rms(seg(_O_CQ, Q_LORA), gq_ref[...]).astype(BF16)
    q = _dot(cq, wuq_ref[...])
    q_lat2 = [_dot(q[:, pp * 2 * LANE:(pp + 1) * 2 * LANE].astype(BF16), wuk_ref[pp]) for pp in range(H_B // 2)]
    for hh in range(H_B):
        q_lat = q_lat2[hh // 2][:, (hh % 2) * LANE:(hh % 2 + 1) * LANE]
        lo = H_B * LANE + hh * LANE
        q_rope = q[:, lo:lo + LANE] * cos + q[:, lo + H_B * LANE:lo + (H_B + 1) * LANE] * sin
        qcat_ref[hh, :, :LANE] = (q_lat * Q_SCALE).astype(BF16)
        qcat_ref[hh, :, LANE:] = (q_rope * Q_SCALE).astype(BF16)


def _proj(x, w, cos, sin):
    m = x.shape[0]
    tm = min(ROW_TILE, m)
    ntab = cos.shape[0] // tm

    def row(n):
        return pl.BlockSpec((tm, n), lambda i: (i, 0))

    tab = pl.BlockSpec((tm, LANE), lambda i: (i % ntab, 0))
    outs = [(H_A * DK_A, BF16), (H_A * DK_A, BF16), (D_MODEL, BF16), (D_MODEL, BF16), (D_MODEL, BF16),
            (LANE, F32), (KV_LORA, F32), (ROPE_DIM, F32), (QK_PAD, BF16)]
    out_specs = [row(n) for n, _ in outs] + [pl.BlockSpec((H_B, tm, QK_PAD), lambda i: (0, i, 0))]
    out_shape = ([jax.ShapeDtypeStruct((m, n), dt) for n, dt in outs]
                 + [jax.ShapeDtypeStruct((H_B, m, QK_PAD), BF16)])
    return pl.pallas_call(
        _proj_kernel,
        grid=(m // tm,),
        in_specs=[row(D_MODEL), _resident((1, D_MODEL)), _resident((D_MODEL, IN_COLS_PAD)),
                  _resident((1, LANE)), _resident((1, Q_LORA)), _resident((1, KV_LORA)),
                  _resident((Q_LORA, 3 * H_B * LANE)), _resident((H_B // 2, 2 * NOPE_DIM, 2 * KV_LORA)), tab, tab],
        out_specs=out_specs,
        out_shape=out_shape,
        compiler_params=_params("parallel"),
        name="proj",
    )(x, w["g_mix_pre"], w["w_in"], w["b_gates"], w["g_q"], w["g_kv"], w["w_uq"], w["w_uk"], cos, sin)


def _mlstm_chunks(probs, tri, eye):
    ln = probs[0]["q"].shape[0]
    each = lambda fn, *lists: [fn(*a) for a in zip(*lists)]
    f_row = [p["f_row"] for p in probs]
    i_row = [p["i_row"] for p in probs]
    q = [p["q"] for p in probs]
    k = [p["k"] for p in probs]
    v = [p["v"] for p in probs]
    c_st = [p["c"] for p in probs]
    n_row = [p["n"] for p in probs]
    m = [p["m"] for p in probs]

    b_col = each(lambda f: jnp.sum(jnp.where(tri, f, 0.0), axis=1, keepdims=True), f_row)
    b_row = each(lambda b: jnp.sum(jnp.where(eye, b, 0.0), axis=0, keepdims=True), b_col)
    b_last = each(lambda b: b[:, ln - 1:ln], b_row)
    g_row = each(lambda bl, br, ir: bl - br + ir, b_last, b_row, i_row)
    m_new = each(lambda bl, mm, g: jnp.maximum(bl + mm, jnp.max(g, axis=1, keepdims=True)), b_last, m, g_row)
    a_row = each(lambda g, mn: jnp.exp(g - mn), g_row, m_new)
    decay = each(lambda bl, mm, mn: jnp.exp(bl + mm - mn), b_last, m, m_new)
    a_col = each(lambda a: jnp.sum(jnp.where(eye, a, 0.0), axis=1, keepdims=True), a_row)
    d = each(lambda bc, br, ir: jnp.where(tri, bc - br + ir, -jnp.inf), b_col, b_row, i_row)
    m_inter = each(lambda bc, mm: bc + mm, b_col, m)
    m_t = each(lambda mi, dd: jnp.maximum(mi, jnp.max(dd, axis=1, keepdims=True)), m_inter, d)
    s = each(lambda qq, kk, dd, mt: _dot_nt(qq, kk) * jnp.exp(dd - mt), q, k, d, m_t)
    inter = each(lambda mi, mt: jnp.exp(mi - mt), m_inter, m_t)
    qn = each(lambda qq, nn: jnp.sum(qq.astype(F32) * nn, axis=1, keepdims=True), q, n_row)
    den = each(lambda ss, it, x: jnp.sum(ss, axis=1, keepdims=True) + it * x, s, inter, qn)
    num = each(lambda ss, vv, it, qq, cc: _dot(ss.astype(BF16), vv) + it * _dot_nt(qq, cc.astype(BF16)),
               s, v, inter, q, c_st)
    h = each(lambda nu, de, mt: nu / jnp.maximum(jnp.abs(de), jnp.exp(-mt)), num, den, m_t)
    va = each(lambda vv, a: (vv.astype(F32) * a).astype(BF16), v, a_col)
    c_new = each(lambda dc, cc, x, kk: dc * cc + _dot_tn(x, kk), decay, c_st, va, k)
    n_new = each(lambda dc, nn, a, kk: dc * nn + jnp.sum(a * kk.astype(F32), axis=0, keepdims=True),
                 decay, n_row, a_col, k)
    return list(zip(h, c_new, n_new, m_new))


def _mlstm_kernel(*refs, chunk, n_chunks, n_seq, has_init):
    if has_init:
        q_ref, k_ref, v_ref, gt_ref, gh_ref, c0_ref, n0_ref, m0_ref, h_ref, c_ref, n_ref, m_ref = refs
        c_ref[...] = c0_ref[...]
        n_ref[...] = n0_ref[...]
        m_ref[...] = m0_ref[...]
    else:
        q_ref, k_ref, v_ref, gt_ref, gh_ref, h_ref, c_ref, n_ref, m_ref = refs
        c_ref[...] = jnp.zeros(c_ref.shape, F32)
        n_ref[...] = jnp.zeros(n_ref.shape, F32)
        m_ref[...] = jnp.zeros(m_ref.shape, F32)
    t_idx = lax.broadcasted_iota(jnp.int32, (chunk, chunk), 0)
    s_idx = lax.broadcasted_iota(jnp.int32, (chunk, chunk), 1)
    tri = s_idx <= t_idx
    eye = s_idx == t_idx
    t_len = chunk * n_chunks

    def step(c, carry):
        where, probs = [], []
        for b in range(n_seq):
            start = b * t_len + c * chunk
            rows = pl.ds(start if isinstance(start, int) else pl.multiple_of(start, chunk), chunk)
            for hh in range(H_A):
                qk_cols = slice(hh * DK_A, (hh + 1) * DK_A)
                v_cols = slice(hh * DV_A, (hh + 1) * DV_A)
                where.append((b, hh, rows, v_cols))
                probs.append(dict(q=q_ref[rows, qk_cols], k=k_ref[rows, qk_cols], v=v_ref[rows, v_cols],
                                  i_row=gt_ref[b, hh, c, 0:1, :], f_row=gt_ref[b, hh, c, 1:2, :],
                                  c=c_ref[b, hh], n=n_ref[b, hh], m=m_ref[b, hh][:, :1]))
        results = _mlstm_chunks(probs, tri, eye)
        normed = [_rms(h, gh_ref[:, v_cols]) for (h, _, _, _), (_, _, _, v_cols) in zip(results, where)]
        for (b, hh, rows, v_cols), (_, c_st, n_row, m), hn in zip(where, results, normed):
            h_ref[rows, v_cols] = hn.astype(h_ref.dtype)
            c_ref[b, hh] = c_st
            n_ref[b, hh] = n_row
            m_ref[b, hh] = jnp.broadcast_to(m, (1, LANE))
        return carry

    if n_chunks == 1:
        step(0, 0)
    else:
        lax.fori_loop(0, n_chunks, step, 0)


def _mlstm(qa, ka, va, gates_t, g_h, init, n_batch, t_len, chunk, n_seq):
    has_init = init is not None
    n_chunks = t_len // chunk
    st4 = lambda n: pl.BlockSpec((n_seq, H_A, 1, n), lambda b: (b, 0, 0, 0))
    c_spec = pl.BlockSpec((n_seq, H_A, DV_A, DK_A), lambda b: (b, 0, 0, 0))
    row = lambda n: pl.BlockSpec((n_seq * t_len, n), lambda b: (b, 0))
    in_specs = [row(H_A * DK_A), row(H_A * DK_A), row(D_MODEL),
                pl.BlockSpec((n_seq, H_A, n_chunks, 2, chunk), lambda b: (b, 0, 0, 0, 0)),
                _resident((1, D_MODEL))]
    args = [qa, ka, va, gates_t, g_h]
    if has_init:
        in_specs += [c_spec, st4(DK_A), st4(LANE)]
        args += list(init)
    return pl.pallas_call(
        functools.partial(_mlstm_kernel, chunk=chunk, n_chunks=n_chunks, n_seq=n_seq, has_init=has_init),
        grid=(n_batch // n_seq,),
        in_specs=in_specs,
        out_specs=[row(D_MODEL), c_spec, st4(DK_A), st4(LANE)],
        out_shape=[jax.ShapeDtypeStruct((n_batch * t_len, D_MODEL), BF16),
                   jax.ShapeDtypeStruct((n_batch, H_A, DV_A, DK_A), F32),
                   jax.ShapeDtypeStruct((n_batch, H_A, 1, DK_A), F32),
                   jax.ShapeDtypeStruct((n_batch, H_A, 1, LANE), F32)],
        compiler_params=_params("parallel"),
        name="mlstm_init" if has_init else "mlstm",
    )(*args)


def _attn_kernel(q_ref, k_ref, o_ref, s_ref, mx_ref, acc_ref):
    tq = q_ref.shape[1]
    qi = pl.program_id(1)
    rows = H_B * tq

    def queries():
        return q_ref[...].reshape(rows, QK_PAD)

    def keys(j, n):
        return k_ref[pl.ds(pl.multiple_of(j * tq, tq), n * tq), :]

    def lane_fold(s):
        m = s[:, :LANE]
        for c in range(1, s.shape[1] // LANE):
            m = jnp.maximum(m, s[:, c * LANE:(c + 1) * LANE])
        return m

    def weighted(s, m, kj):
        p = jnp.exp2(s - jnp.concatenate([m] * (s.shape[1] // LANE), axis=1)).astype(BF16)
        return _dot(p, jnp.concatenate([kj[:, :KV_LORA], jnp.ones((kj.shape[0], LANE), BF16)], axis=1))

    def score_blocks(j, n):
        s = _dot_nt(queries(), keys(j, n))
        for c in range(n):
            s_ref[j + c] = s[:, c * tq:(c + 1) * tq]
        mx_ref[...] = jnp.maximum(mx_ref[...], lane_fold(s))

    def weigh_blocks(j, n):
        s = jnp.concatenate([s_ref[j + c] for c in range(n)], axis=1)
        acc_ref[...] += weighted(s, mx_ref[...], keys(j, n))

    k_diag = keys(qi, 1)
    r_idx = lax.broadcasted_iota(jnp.int32, (H_B, tq, tq), 1).reshape(rows, tq)
    c_idx = lax.broadcasted_iota(jnp.int32, (rows, tq), 1)
    s_diag = jnp.where(c_idx <= r_idx, _dot_nt(queries(), k_diag), -jnp.inf)
    odd = qi % 2 == 1

    mx_ref[...] = lane_fold(s_diag)
    pl.loop(0, qi // 2)(lambda jj: score_blocks(2 * jj, 2))
    pl.when(odd)(lambda: score_blocks(qi - 1, 1))

    m = jnp.broadcast_to(jnp.max(mx_ref[...], axis=1, keepdims=True), (rows, LANE))
    mx_ref[...] = m
    acc_ref[...] = weighted(s_diag, m, k_diag)
    pl.loop(0, qi // 2)(lambda jj: weigh_blocks(2 * jj, 2))
    pl.when(odd)(lambda: weigh_blocks(qi - 1, 1))

    o = acc_ref[:, :KV_LORA] / acc_ref[:, KV_LORA:]
    for hh in range(H_B):
        o_ref[:, hh * KV_LORA:(hh + 1) * KV_LORA] = o[hh * tq:(hh + 1) * tq].astype(o_ref.dtype)


def _attn_prompt(qcat, kcat, n_batch, t_len):
    tq = min(ATTN_TILE, t_len)
    assert tq == 2 * LANE
    nq = t_len // tq
    return pl.pallas_call(
        _attn_kernel,
        grid=(n_batch, nq),
        in_specs=[pl.BlockSpec((H_B, tq, QK_PAD), lambda b, i: (0, b * nq + i, 0)),
                  pl.BlockSpec((t_len, QK_PAD), lambda b, i: (b, 0))],
        out_specs=pl.BlockSpec((tq, H_B * KV_LORA), lambda b, i: (b * nq + i, 0)),
        out_shape=jax.ShapeDtypeStruct((n_batch * t_len, H_B * KV_LORA), BF16),
        scratch_shapes=[pltpu.VMEM((max(nq - 1, 1), H_B * tq, tq), F32),
                        pltpu.VMEM((H_B * tq, LANE), F32),
                        pltpu.VMEM((H_B * tq, 2 * KV_LORA), F32)],
        compiler_params=_params("parallel", "parallel"),
        name="attn_prompt",
    )(qcat, kcat)


def _attn_sample_kernel(pt_ref, q_ref, kvn_ref, krn_ref, kv_hbm, kr_hbm, o_ref,
                        kv_a, kr_a, kv_b, kr_b, sem, *, pages, t_new):
    b = pl.program_id(0)
    last = b + 1 == pl.num_programs(0)
    bufs = ((kv_a, kr_a), (kv_b, kr_b))

    def half_copies(seq, half):
        kv_buf, kr_buf = bufs[half]
        copies = []
        for p in range(pages):
            page = pt_ref[seq, half * pages + p]
            span = pl.ds(p * PAGE_SIZE, PAGE_SIZE)
            copies.append(pltpu.make_async_copy(kv_hbm.at[page], kv_buf.at[span, :], sem.at[half, 0]))
            copies.append(pltpu.make_async_copy(kr_hbm.at[page], kr_buf.at[p], sem.at[half, 1]))
        return copies

    @pl.when(b == 0)
    def _():
        for cp in half_copies(b, 0):
            cp.start()

    q = q_ref[0]
    q_lat = q[:, :KV_LORA]
    q_rope = q[:, KV_LORA:KV_LORA + ROPE_DIM]

    def partial_softmax(s, v):
        m_blk = jnp.max(s, axis=1, keepdims=True)
        p = jnp.exp2(s - m_blk)
        return m_blk, jnp.sum(p, axis=1, keepdims=True), _dot(p.astype(BF16), v)

    def half_parts(half):
        kv_buf, kr_buf = bufs[half]
        sub_pages = pages // SAMPLE_SUB_BLOCKS
        sub = sub_pages * PAGE_SIZE
        parts = []
        for i in range(SAMPLE_SUB_BLOCKS):
            kv = kv_buf[i * sub:(i + 1) * sub, :].astype(BF16)
            kr_t = jnp.concatenate([kr_buf[p].astype(BF16) for p in range(i * sub_pages, (i + 1) * sub_pages)],
                                   axis=1)
            parts.append(partial_softmax(_dot_nt(q_lat, kv) + _dot(q_rope, kr_t), kv))
        return parts

    for cp in half_copies(b, 0):
        cp.wait()
    for cp in half_copies(b, 1):
        cp.start()
    parts = half_parts(0)

    for cp in half_copies(jnp.where(last, 0, b + 1), 0):
        cp.start()
    for cp in half_copies(b, 1):
        cp.wait()
    parts += half_parts(1)

    kvn = kvn_ref[0].astype(BF16)
    krn = krn_ref[0].astype(BF16)
    s = _dot_nt(q_lat, kvn) + _dot_nt(q_rope, krn)
    r_tok = lax.broadcasted_iota(jnp.int32, s.shape, 0) % t_new
    c_tok = lax.broadcasted_iota(jnp.int32, s.shape, 1)
    parts.append(partial_softmax(jnp.where(c_tok <= r_tok, s, -jnp.inf), kvn))

    m = parts[0][0]
    for m_blk, _, _ in parts[1:]:
        m = jnp.maximum(m, m_blk)
    l = jnp.zeros_like(m)
    acc = jnp.zeros((q.shape[0], KV_LORA), F32)
    for m_blk, l_blk, o_blk in parts:
        w_blk = jnp.exp2(m_blk - m)
        l = l + w_blk * l_blk
        acc = acc + w_blk * o_blk
    o_ref[0] = (acc / l).astype(o_ref.dtype)

    @pl.when(last)
    def _():
        for cp in half_copies(0, 0):
            cp.wait()


def _attn_sample(page_table, q, kv_new, kr_new, cache_kv, cache_kr_t, t_new):
    n_b, n_pages = page_table.shape
    assert n_pages % 2 == 0
    pages = n_pages // 2
    half_rows = pages * PAGE_SIZE
    assert pages % SAMPLE_SUB_BLOCKS == 0
    rows = q.shape[1]
    pad = kv_new.shape[1]
    grid_spec = pltpu.PrefetchScalarGridSpec(
        num_scalar_prefetch=1,
        grid=(n_b,),
        in_specs=[pl.BlockSpec((1, rows, QK_PAD), lambda b, pt: (b, 0, 0)),
                  pl.BlockSpec((1, pad, KV_LORA), lambda b, pt: (b, 0, 0)),
                  pl.BlockSpec((1, pad, ROPE_DIM), lambda b, pt: (b, 0, 0)),
                  pl.BlockSpec(memory_space=pl.ANY),
                  pl.BlockSpec(memory_space=pl.ANY)],
        out_specs=pl.BlockSpec((1, rows, KV_LORA), lambda b, pt: (b, 0, 0)),
        scratch_shapes=[pltpu.VMEM((half_rows, KV_LORA), F32), pltpu.VMEM((pages, ROPE_DIM, PAGE_SIZE), F32),
                        pltpu.VMEM((half_rows, KV_LORA), F32), pltpu.VMEM((pages, ROPE_DIM, PAGE_SIZE), F32),
                        pltpu.SemaphoreType.DMA((2, 2))])
    return pl.pallas_call(
        functools.partial(_attn_sample_kernel, pages=pages, t_new=t_new),
        grid_spec=grid_spec,
        out_shape=jax.ShapeDtypeStruct((n_b, rows, KV_LORA), BF16),
        compiler_params=_params("arbitrary"),
        name="attn_sample",
    )(page_table, q, kv_new, kr_new, cache_kv, cache_kr_t)


def _out_ffn_kernel(x_ref, ha_ref, ga_ref, gb_ref, ol_ref, wuv_ref, wout_ref, gmix_ref,
                    gpre_ref, gpost_ref, wgu_ref, wdn_ref, o_ref, act_ref):
    merged = []
    for pp in range(H_B // 2):
        cols = slice(pp * 2 * V_DIM, (pp + 1) * 2 * V_DIM)
        y_b = _dot(ol_ref[:, cols], wuv_ref[pp])
        y = ga_ref[:, cols].astype(F32) * ha_ref[:, cols].astype(F32) + gb_ref[:, cols].astype(F32) * y_b
        merged.append(y.astype(BF16))
    mix = _dot(jnp.concatenate(merged, axis=1), wout_ref[...])
    x = x_ref[...] + _rms(mix, gmix_ref[...])
    o_ref[...] = _ffn_body(x, gpre_ref[...], gpost_ref[...], wgu_ref, wdn_ref, act_ref)


def _out_ffn(x, ha, ga, gb, o_lat, w):
    m = x.shape[0]
    tm = min(ROW_TILE, m)
    row = pl.BlockSpec((tm, D_MODEL), lambda i: (i, 0))
    vec = _resident((1, D_MODEL))
    return pl.pallas_call(
        _out_ffn_kernel,
        grid=(m // tm,),
        in_specs=[row, row, row, row, row, _resident((H_B // 2, 2 * KV_LORA, 2 * V_DIM)),
                  _resident((D_MODEL, D_MODEL)), vec,
                  vec, vec, _resident((D_MODEL, 2 * D_FF)), _resident((D_FF, D_MODEL))],
        out_specs=row,
        out_shape=jax.ShapeDtypeStruct((m, D_MODEL), F32),
        scratch_shapes=[pltpu.VMEM((tm, D_FF), BF16)],
        compiler_params=_params("parallel"),
        name="out_ffn",
    )(x, ha, ga, gb, o_lat, w["w_uv"], w["w_out"], w["g_mix_post"],
      w["g_ffn2_pre"], w["g_ffn2_post"], w["w_ffn2_gu"], w["w_ffn2_down"])


def _prep_weights(norm_ffn1_pre, norm_ffn1_post, w_ffn1_gu, w_ffn1_down, norm_mix_pre, norm_mix_post,
                  w_in, b_gates, w_uq, norm_q_lat, norm_kv_lat, w_uk, w_uv, norm_mlstm_h, w_out,
                  norm_ffn2_pre, norm_ffn2_post, w_ffn2_gu, w_ffn2_down):
    half = ROPE_DIM // 2
    row = lambda g: g.reshape(1, -1).astype(F32)

    def swap_pad(wr):
        z = jnp.zeros(wr.shape[:-1] + (LANE - ROPE_DIM,), wr.dtype)
        return (jnp.concatenate([wr, z], axis=-1),
                jnp.concatenate([wr[..., half:], wr[..., :half], z], axis=-1))

    sizes = (H_A * DK_A, H_A * DK_A, H_A * DV_A, H_A, H_A, H_A * DV_A, Q_LORA, KV_LORA, ROPE_DIM, D_MODEL, D_MODEL)
    offs = [0]
    for n in sizes:
        offs.append(offs[-1] + n)
    qa, ka, va, ip, fp, oa, cq, ckv, kr, ga, gb = [w_in[:, offs[i]:offs[i + 1]] for i in range(len(sizes))]
    kr_p, kr_s = swap_pad(kr)
    gates = jnp.concatenate([ip, fp, jnp.zeros((D_MODEL, LANE - 2 * H_A), w_in.dtype)], axis=1)
    w_in2 = jnp.concatenate([qa, ka, va, oa, ga, gb, cq, ckv, kr_p, kr_s, gates], axis=1).astype(BF16)
    assert w_in2.shape[1] == IN_COLS_PAD

    uq = w_uq.reshape(Q_LORA, H_B, NOPE_DIM + ROPE_DIM)
    uq_p, uq_s = swap_pad(uq[..., NOPE_DIM:])
    w_uq2 = jnp.concatenate([uq[..., :NOPE_DIM].reshape(Q_LORA, -1), uq_p.reshape(Q_LORA, -1),
                             uq_s.reshape(Q_LORA, -1)], axis=1).astype(BF16)
    return dict(
        g_ffn1_pre=row(norm_ffn1_pre), g_ffn1_post=row(norm_ffn1_post),
        w_ffn1_gu=w_ffn1_gu.astype(BF16), w_ffn1_down=w_ffn1_down.astype(BF16),
        g_ffn2_pre=row(norm_ffn2_pre), g_ffn2_post=row(norm_ffn2_post),
        w_ffn2_gu=w_ffn2_gu.astype(BF16), w_ffn2_down=w_ffn2_down.astype(BF16),
        g_mix_pre=row(norm_mix_pre), g_mix_post=row(norm_mix_post),
        w_in=w_in2, w_uq=w_uq2,
        b_gates=jnp.concatenate([b_gates.astype(F32), jnp.zeros((LANE - 2 * H_A,), F32)]).reshape(1, LANE),
        g_q=row(norm_q_lat), g_kv=row(norm_kv_lat),
        w_uk=_pair_block_diag(jnp.transpose(w_uk, (1, 2, 0)).astype(BF16)),
        w_uv=_pair_block_diag(jnp.transpose(w_uv, (1, 0, 2)).astype(BF16)),
        g_h=row(norm_mlstm_h), w_out=w_out.astype(BF16))


def _pair_block_diag(w):
    h, a, b = w.shape
    w = w.reshape(h // 2, 2, a, b)
    z = jnp.zeros((h // 2, a, b), w.dtype)
    top = jnp.concatenate([w[:, 0], z], axis=2)
    bottom = jnp.concatenate([z, w[:, 1]], axis=2)
    return jnp.concatenate([top, bottom], axis=1)


def _rope_tables(pos):
    half = ROPE_DIM // 2
    inv_freq = ROPE_THETA ** (-jnp.arange(half, dtype=F32) / half)
    ang = pos.astype(F32)[:, None] * inv_freq[None, :]
    cos, sin = jnp.cos(ang), jnp.sin(ang)
    z = jnp.zeros((pos.shape[0], LANE - ROPE_DIM), F32)
    return jnp.concatenate([cos, cos, z], axis=1), jnp.concatenate([-sin, sin, z], axis=1)


def _gates_by_head(gt, n_batch, t_len, chunk):
    g = gt[:, :2 * H_A].reshape(n_batch, t_len // chunk, chunk, 2, H_A)
    return jnp.transpose(g, (0, 4, 1, 3, 2))


def kernel(x_prompt, x_sample, cache_kv_latent, cache_k_rope, state_mlstm_C, state_mlstm_n, state_mlstm_m,
           page_table, norm_ffn1_pre, norm_ffn1_post, w_ffn1_gu, w_ffn1_down, norm_mix_pre, norm_mix_post,
           w_in, b_gates, w_uq, norm_q_lat, norm_kv_lat, w_uk, w_uv, norm_mlstm_h, w_out,
           norm_ffn2_pre, norm_ffn2_post, w_ffn2_gu, w_ffn2_down):
    assert w_in.shape[0] == 1, "single-layer trunk"
    b_p, t_p, _ = x_prompt.shape
    b_s, t_s, _ = x_sample.shape
    past_len = page_table.shape[1] * PAGE_SIZE
    w = _prep_weights(norm_ffn1_pre[0], norm_ffn1_post[0], w_ffn1_gu[0], w_ffn1_down[0], norm_mix_pre[0],
                      norm_mix_post[0], w_in[0], b_gates[0], w_uq[0], norm_q_lat[0], norm_kv_lat[0], w_uk[0],
                      w_uv[0], norm_mlstm_h[0], w_out[0], norm_ffn2_pre[0], norm_ffn2_post[0], w_ffn2_gu[0],
                      w_ffn2_down[0])

    xp = x_prompt.reshape(b_p * t_p, D_MODEL)
    xp = _ffn(xp, w["g_ffn1_pre"], w["g_ffn1_post"], w["w_ffn1_gu"], w["w_ffn1_down"])
    cos_p, sin_p = _rope_tables(jnp.arange(t_p, dtype=jnp.int32))
    qa, ka, va, ga, gb, gt, ckv_p, kr_p, kcat, qcat = _proj(xp, w, cos_p, sin_p)
    chunk = min(MLSTM_CHUNK, t_p)
    ha, c_p, n_p, m_p = _mlstm(qa, ka, va, _gates_by_head(gt, b_p, t_p, chunk), w["g_h"], None, b_p, t_p, chunk, 1)
    o_lat = _attn_prompt(qcat, kcat, b_p, t_p)
    yp = _out_ffn(xp, ha, ga, gb, o_lat, w)

    t_pad = 16
    new_pad = 8
    xs = x_sample.reshape(b_s * t_s, D_MODEL)
    xs = _ffn(xs, w["g_ffn1_pre"], w["g_ffn1_post"], w["w_ffn1_gu"], w["w_ffn1_down"])
    cos_s, sin_s = _rope_tables(past_len + jnp.arange(t_s, dtype=jnp.int32))
    reps = min(ROW_TILE, b_s * t_s) // t_s
    qa, ka, va, ga, gb, gt, ckv_s, kr_s, _, qcat = _proj(xs, w, jnp.tile(cos_s, (reps, 1)), jnp.tile(sin_s, (reps, 1)))

    def pad_t(a, value=0.0):
        a = a.reshape(b_s, t_s, a.shape[-1])
        a = jnp.pad(a, ((0, 0), (0, t_pad - t_s), (0, 0)), constant_values=value)
        return a.reshape(b_s * t_pad, a.shape[-1])

    tok = jnp.arange(b_s * t_pad) % t_pad
    lane = jnp.arange(LANE)
    gt_pad = jnp.where((tok[:, None] >= t_s) & (lane[None, :] < H_A), NEG_BIG, pad_t(gt))
    m0 = jnp.broadcast_to(state_mlstm_m[0][:, :, None, None], (b_s, H_A, 1, LANE))
    ha, c_s, n_s, m_s = _mlstm(pad_t(qa), pad_t(ka), pad_t(va), _gates_by_head(gt_pad, b_s, t_pad, t_pad), w["g_h"],
                               (state_mlstm_C[0], state_mlstm_n[0][:, :, None, :], m0), b_s, t_pad, t_pad,
                               math.gcd(b_s, MLSTM_SAMPLE_SEQS))
    ha = ha.reshape(b_s, t_pad, D_MODEL)[:, :t_s].reshape(b_s * t_s, D_MODEL)

    q_s = qcat.reshape(H_B, b_s, t_s, QK_PAD).transpose(1, 0, 2, 3).reshape(b_s, H_B * t_s, QK_PAD)
    pad_new = lambda a: jnp.pad(a.reshape(b_s, t_s, -1), ((0, 0), (0, new_pad - t_s), (0, 0)))
    o_s = _attn_sample(page_table, q_s, pad_new(ckv_s), pad_new(kr_s), cache_kv_latent[0],
                       jnp.swapaxes(cache_k_rope[0], 1, 2), t_s)
    o_s = o_s.reshape(b_s, H_B, t_s, KV_LORA).transpose(0, 2, 1, 3).reshape(b_s * t_s, H_B * KV_LORA)
    ys = _out_ffn(xs, ha, ga, gb, o_s, w)

    return (yp.reshape(b_p, t_p, D_MODEL), ys.reshape(b_s, t_s, D_MODEL),
            ckv_p.reshape(1, b_p, t_p, KV_LORA), kr_p.reshape(1, b_p, t_p, ROPE_DIM),
            c_p[None], n_p[:, :, 0, :][None], m_p[:, :, 0, 0][None],
            ckv_s.reshape(1, b_s, t_s, KV_LORA), kr_s.reshape(1, b_s, t_s, ROPE_DIM),
            c_s[None], n_s[:, :, 0, :][None], m_s[:, :, 0, 0][None])
```

```python
import functools
import math

import jax
import jax.numpy as jnp
from jax import lax
from jax.experimental import pallas as pl
from jax.experimental.pallas import tpu as pltpu

F32 = jnp.float32
BF16 = jnp.bfloat16

D_MODEL = 1024
H_A = 4
DV_A = D_MODEL // H_A
DK_A = DV_A // 2
V_DIM = 128
H_B = D_MODEL // V_DIM
NOPE_DIM = 128
ROPE_DIM = 64
Q_LORA = 256
KV_LORA = 128
ROPE_THETA = 10000.0
SM_SCALE = (NOPE_DIM + ROPE_DIM) ** -0.5
Q_SCALE = SM_SCALE * math.log2(math.e)
D_FF = 2816
EPS = 1e-6
PAGE_SIZE = 128
QK_PAD = 256

LANE = 128
VMEM_LIMIT = 56 * 1024 * 1024
ROW_TILE = 512
FF_CHUNK = 256
MLSTM_CHUNK = 256
MLSTM_SAMPLE_SEQS = 4
ATTN_TILE = 256
SAMPLE_SUB_BLOCKS = 4
NEG_BIG = -1e30


def _params(*sem):
    return pltpu.CompilerParams(dimension_semantics=sem, vmem_limit_bytes=VMEM_LIMIT)


def _resident(shape):
    return pl.BlockSpec(shape, lambda *_: (0,) * len(shape), pipeline_mode=pl.Buffered(1))


def _rms(x, g):
    return x * lax.rsqrt(jnp.mean(x * x, axis=-1, keepdims=True) + EPS) * g


def _dot(a, b):
    return jnp.dot(a, b, preferred_element_type=F32)


def _dot_nt(a, b):
    return lax.dot_general(a, b, (((1,), (1,)), ((), ())), preferred_element_type=F32)


def _dot_tn(a, b):
    return lax.dot_general(a, b, (((0,), (0,)), ((), ())), preferred_element_type=F32)


def _ffn_body(x, gpre, gpost, wgu_ref, wdn_ref, act_ref):
    h = _rms(x, gpre).astype(BF16)
    for lo in range(0, D_FF, FF_CHUNK):
        g = _dot(h, wgu_ref[:, lo:lo + FF_CHUNK])
        u = _dot(h, wgu_ref[:, D_FF + lo:D_FF + lo + FF_CHUNK])
        act_ref[:, lo:lo + FF_CHUNK] = (g * jax.nn.sigmoid(g) * u).astype(BF16)
    f = _dot(act_ref[...], wdn_ref[...])
    return x + 0.5 * _rms(f, gpost)


def _ffn_kernel(x_ref, gpre_ref, gpost_ref, wgu_ref, wdn_ref, o_ref, act_ref):
    o_ref[...] = _ffn_body(x_ref[...], gpre_ref[...], gpost_ref[...], wgu_ref, wdn_ref, act_ref)


def _ffn(x, gpre, gpost, wgu, wdn):
    m = x.shape[0]
    tm = min(ROW_TILE, m)
    row = pl.BlockSpec((tm, D_MODEL), lambda i: (i, 0))
    return pl.pallas_call(
        _ffn_kernel,
        grid=(m // tm,),
        in_specs=[row, _resident((1, D_MODEL)), _resident((1, D_MODEL)),
                  _resident((D_MODEL, 2 * D_FF)), _resident((D_FF, D_MODEL))],
        out_specs=row,
        out_shape=jax.ShapeDtypeStruct((m, D_MODEL), F32),
        scratch_shapes=[pltpu.VMEM((tm, D_FF), BF16)],
        compiler_params=_params("parallel"),
        name="ffn",
    )(x, gpre, gpost, wgu, wdn)


_O_QA, _O_KA, _O_VA, _O_OA, _O_GA, _O_GB = 0, 512, 1024, 2048, 3072, 4096
_O_CQ, _O_CKV, _O_KR, _O_KRS, _O_GT = 5120, 5376, 5504, 5632, 5760
IN_COLS_PAD = 5888


def _proj_kernel(x_ref, gpre_ref, win_ref, bg_ref, gq_ref, gkv_ref, wuq_ref, wuk_ref, cos_ref, sin_ref,
                 qa_ref, ka_ref, va_ref, ga_ref, gb_ref, gt_ref, ckv_ref, kr_ref, kcat_ref, qcat_ref):
    h = _rms(x_ref[...], gpre_ref[...]).astype(BF16)

    def seg(lo, n):
        return _dot(h, win_ref[:, lo:lo + n])

    qa_ref[...] = seg(_O_QA, H_A * DK_A).astype(BF16)
    ka_ref[...] = (seg(_O_KA, H_A * DK_A) * (DK_A ** -0.5)).astype(BF16)
    va_ref[...] = seg(_O_VA, D_MODEL).astype(BF16)
    ga_ref[...] = (jax.nn.sigmoid(seg(_O_GA, D_MODEL)) * jax.nn.sigmoid(seg(_O_OA, D_MODEL))).astype(BF16)
    gb_ref[...] = jax.nn.sigmoid(seg(_O_GB, D_MODEL)).astype(BF16)

    gt = seg(_O_GT, LANE) + bg_ref[...]
    logsig = jnp.minimum(gt, 0.0) - jnp.log1p(jnp.exp(-jnp.abs(gt)))
    lane = lax.broadcasted_iota(jnp.int32, gt.shape, 1)
    gt_ref[...] = jnp.where(lane < H_A, gt, logsig)

    cos = cos_ref[...]
    sin = sin_ref[...]
    ckv = _rms(seg(_O_CKV, KV_LORA), gkv_ref[...])
    kr = seg(_O_KR, LANE) * cos + seg(_O_KRS, LANE) * sin
    ckv_ref[...] = ckv
    kr_ref[...] = kr[:, :ROPE_DIM]
    kcat_ref[:, :KV_LORA] = ckv.astype(BF16)
    kcat_ref[:, KV_LORA:] = kr.astype(BF16)

    cq = _rms(seg(_O_CQ, Q_LORA), gq_ref[...]).astype(BF16)
    q = _dot(cq, wuq_ref[...])
    q_lat2 = [_dot(q[:, pp * 2 * LANE:(pp + 1) * 2 * LANE].astype(BF16), wuk_ref[pp]) for pp in range(H_B // 2)]
    for hh in range(H_B):
        q_lat = q_lat2[hh // 2][:, (hh % 2) * LANE:(hh % 2 + 1) * LANE]
        lo = H_B * LANE + hh * LANE
        q_rope = q[:, lo:lo + LANE] * cos + q[:, lo + H_B * LANE:lo + (H_B + 1) * LANE] * sin
        qcat_ref[hh, :, :LANE] = (q_lat * Q_SCALE).astype(BF16)
        qcat_ref[hh, :, LANE:] = (q_rope * Q_SCALE).astype(BF16)


def _proj(x, w, cos, sin):
    m = x.shape[0]
    tm = min(ROW_TILE, m)
    ntab = cos.shape[0] // tm

    def row(n):
        return pl.BlockSpec((tm, n), lambda i: (i, 0))

    tab = pl.BlockSpec((tm, LANE), lambda i: (i % ntab, 0))
    outs = [(H_A * DK_A, BF16), (H_A * DK_A, BF16), (D_MODEL, BF16), (D_MODEL, BF16), (D_MODEL, BF16),
            (LANE, F32), (KV_LORA, F32), (ROPE_DIM, F32), (QK_PAD, BF16)]
    out_specs = [row(n) for n, _ in outs] + [pl.BlockSpec((H_B, tm, QK_PAD), lambda i: (0, i, 0))]
    out_shape = ([jax.ShapeDtypeStruct((m, n), dt) for n, dt in outs]
                 + [jax.ShapeDtypeStruct((H_B, m, QK_PAD), BF16)])
    return pl.pallas_call(
        _proj_kernel,
        grid=(m // tm,),
        in_specs=[row(D_MODEL), _resident((1, D_MODEL)), _resident((D_MODEL, IN_COLS_PAD)),
                  _resident((1, LANE)), _resident((1, Q_LORA)), _resident((1, KV_LORA)),
                  _resident((Q_LORA, 3 * H_B * LANE)), _resident((H_B // 2, 2 * NOPE_DIM, 2 * KV_LORA)), tab, tab],
        out_specs=out_specs,
        out_shape=out_shape,
        compiler_params=_params("parallel"),
        name="proj",
    )(x, w["g_mix_pre"], w["w_in"], w["b_gates"], w["g_q"], w["g_kv"], w["w_uq"], w["w_uk"], cos, sin)


def _mlstm_chunks(probs, tri, eye):
    ln = probs[0]["q"].shape[0]
    each = lambda fn, *lists: [fn(*a) for a in zip(*lists)]
    f_row = [p["f_row"] for p in probs]
    i_row = [p["i_row"] for p in probs]
    q = [p["q"] for p in probs]
    k = [p["k"] for p in probs]
    v = [p["v"] for p in probs]
    c_st = [p["c"] for p in probs]
    n_row = [p["n"] for p in probs]
    m = [p["m"] for p in probs]

    b_col = each(lambda f: jnp.sum(jnp.where(tri, f, 0.0), axis=1, keepdims=True), f_row)
    b_row = each(lambda b: jnp.sum(jnp.where(eye, b, 0.0), axis=0, keepdims=True), b_col)
    b_last = each(lambda b: b[:, ln - 1:ln], b_row)
    g_row = each(lambda bl, br, ir: bl - br + ir, b_last, b_row, i_row)
    m_new = each(lambda bl, mm, g: jnp.maximum(bl + mm, jnp.max(g, axis=1, keepdims=True)), b_last, m, g_row)
    a_row = each(lambda g, mn: jnp.exp(g - mn), g_row, m_new)
    decay = each(lambda bl, mm, mn: jnp.exp(bl + mm - mn), b_last, m, m_new)
    a_col = each(lambda a: jnp.sum(jnp.where(eye, a, 0.0), axis=1, keepdims=True), a_row)
    d = each(lambda bc, br, ir: jnp.where(tri, bc - br + ir, -jnp.inf), b_col, b_row, i_row)
    m_inter = each(lambda bc, mm: bc + mm, b_col, m)
    m_t = each(lambda mi, dd: jnp.maximum(mi, jnp.max(dd, axis=1, keepdims=True)), m_inter, d)
    s = each(lambda qq, kk, dd, mt: _dot_nt(qq, kk) * jnp.exp(dd - mt), q, k, d, m_t)
    inter = each(lambda mi, mt: jnp.exp(mi - mt), m_inter, m_t)
    qn = each(lambda qq, nn: jnp.sum(qq.astype(F32) * nn, axis=1, keepdims=True), q, n_row)
    den = each(lambda ss, it, x: jnp.sum(ss, axis=1, keepdims=True) + it * x, s, inter, qn)
    num = each(lambda ss, vv, it, qq, cc: _dot(ss.astype(BF16), vv) + it * _dot_nt(qq, cc.astype(BF16)),
               s, v, inter, q, c_st)
    h = each(lambda nu, de, mt: nu / jnp.maximum(jnp.abs(de), jnp.exp(-mt)), num, den, m_t)
    va = each(lambda vv, a: (vv.astype(F32) * a).astype(BF16), v, a_col)
    c_new = each(lambda dc, cc, x, kk: dc * cc + _dot_tn(x, kk), decay, c_st, va, k)
    n_new = each(lambda dc, nn, a, kk: dc * nn + jnp.sum(a * kk.astype(F32), axis=0, keepdims=True),
                 decay, n_row, a_col, k)
    return list(zip(h, c_new, n_new, m_new))


def _mlstm_kernel(*refs, chunk, n_chunks, n_seq, has_init):
    if has_init:
        q_ref, k_ref, v_ref, gt_ref, gh_ref, c0_ref, n0_ref, m0_ref, h_ref, c_ref, n_ref, m_ref = refs
        c_ref[...] = c0_ref[...]
        n_ref[...] = n0_ref[...]
        m_ref[...] = m0_ref[...]
    else:
        q_ref, k_ref, v_ref, gt_ref, gh_ref, h_ref, c_ref, n_ref, m_ref = refs
        c_ref[...] = jnp.zeros(c_ref.shape, F32)
        n_ref[...] = jnp.zeros(n_ref.shape, F32)
        m_ref[...] = jnp.zeros(m_ref.shape, F32)
    t_idx = lax.broadcasted_iota(jnp.int32, (chunk, chunk), 0)
    s_idx = lax.broadcasted_iota(jnp.int32, (chunk, chunk), 1)
    tri = s_idx <= t_idx
    eye = s_idx == t_idx
    t_len = chunk * n_chunks

    def step(c, carry):
        where, probs = [], []
        for b in range(n_seq):
            start = b * t_len + c * chunk
            rows = pl.ds(start if isinstance(start, int) else pl.multiple_of(start, chunk), chunk)
            for hh in range(H_A):
                qk_cols = slice(hh * DK_A, (hh + 1) * DK_A)
                v_cols = slice(hh * DV_A, (hh + 1) * DV_A)
                where.append((b, hh, rows, v_cols))
                probs.append(dict(q=q_ref[rows, qk_cols], k=k_ref[rows, qk_cols], v=v_ref[rows, v_cols],
                                  i_row=gt_ref[b, hh, c, 0:1, :], f_row=gt_ref[b, hh, c, 1:2, :],
                                  c=c_ref[b, hh], n=n_ref[b, hh], m=m_ref[b, hh][:, :1]))
        results = _mlstm_chunks(probs, tri, eye)
        normed = [_rms(h, gh_ref[:, v_cols]) for (h, _, _, _), (_, _, _, v_cols) in zip(results, where)]
        for (b, hh, rows, v_cols), (_, c_st, n_row, m), hn in zip(where, results, normed):
            h_ref[rows, v_cols] = hn.astype(h_ref.dtype)
            c_ref[b, hh] = c_st
            n_ref[b, hh] = n_row
            m_ref[b, hh] = jnp.broadcast_to(m, (1, LANE))
        return carry

    if n_chunks == 1:
        step(0, 0)
    else:
        lax.fori_loop(0, n_chunks, step, 0)


def _mlstm(qa, ka, va, gates_t, g_h, init, n_batch, t_len, chunk, n_seq):
    has_init = init is not None
    n_chunks = t_len // chunk
    st4 = lambda n: pl.BlockSpec((n_seq, H_A, 1, n), lambda b: (b, 0, 0, 0))
    c_spec = pl.BlockSpec((n_seq, H_A, DV_A, DK_A), lambda b: (b, 0, 0, 0))
    row = lambda n: pl.BlockSpec((n_seq * t_len, n), lambda b: (b, 0))
    in_specs = [row(H_A * DK_A), row(H_A * DK_A), row(D_MODEL),
                pl.BlockSpec((n_seq, H_A, n_chunks, 2, chunk), lambda b: (b, 0, 0, 0, 0)),
                _resident((1, D_MODEL))]
    args = [qa, ka, va, gates_t, g_h]
    if has_init:
        in_specs += [c_spec, st4(DK_A), st4(LANE)]
        args += list(init)
    return pl.pallas_call(
        functools.partial(_mlstm_kernel, chunk=chunk, n_chunks=n_chunks, n_seq=n_seq, has_init=has_init),
        grid=(n_batch // n_seq,),
        in_specs=in_specs,
        out_specs=[row(D_MODEL), c_spec, st4(DK_A), st4(LANE)],
        out_shape=[jax.ShapeDtypeStruct((n_batch * t_len, D_MODEL), BF16),
                   jax.ShapeDtypeStruct((n_batch, H_A, DV_A, DK_A), F32),
                   jax.ShapeDtypeStruct((n_batch, H_A, 1, DK_A), F32),
                   jax.ShapeDtypeStruct((n_batch, H_A, 1, LANE), F32)],
        compiler_params=_params("parallel"),
        name="mlstm_init" if has_init else "mlstm",
    )(*args)


def _attn_kernel(q_ref, k_ref, o_ref, s_ref, mx_ref, acc_ref):
    tq = q_ref.shape[1]
    qi = pl.program_id(1)
    rows = H_B * tq

    def queries():
        return q_ref[...].reshape(rows, QK_PAD)

    def keys(j, n):
        return k_ref[pl.ds(pl.multiple_of(j * tq, tq), n * tq), :]

    def lane_fold(s):
        m = s[:, :LANE]
        for c in range(1, s.shape[1] // LANE):
            m = jnp.maximum(m, s[:, c * LANE:(c + 1) * LANE])
        return m

    def weighted(s, m, kj):
        p = jnp.exp2(s - jnp.concatenate([m] * (s.shape[1] // LANE), axis=1)).astype(BF16)
        return _dot(p, jnp.concatenate([kj[:, :KV_LORA], jnp.ones((kj.shape[0], LANE), BF16)], axis=1))

    def score_blocks(j, n):
        s = _dot_nt(queries(), keys(j, n))
        for c in range(n):
            s_ref[j + c] = s[:, c * tq:(c + 1) * tq]
        mx_ref[...] = jnp.maximum(mx_ref[...], lane_fold(s))

    def weigh_blocks(j, n):
        s = jnp.concatenate([s_ref[j + c] for c in range(n)], axis=1)
        acc_ref[...] += weighted(s, mx_ref[...], keys(j, n))

    k_diag = keys(qi, 1)
    r_idx = lax.broadcasted_iota(jnp.int32, (H_B, tq, tq), 1).reshape(rows, tq)
    c_idx = lax.broadcasted_iota(jnp.int32, (rows, tq), 1)
    s_diag = jnp.where(c_idx <= r_idx, _dot_nt(queries(), k_diag), -jnp.inf)
    odd = qi % 2 == 1

    mx_ref[...] = lane_fold(s_diag)
    pl.loop(0, qi // 2)(lambda jj: score_blocks(2 * jj, 2))
    pl.when(odd)(lambda: score_blocks(qi - 1, 1))

    m = jnp.broadcast_to(jnp.max(mx_ref[...], axis=1, keepdims=True), (rows, LANE))
    mx_ref[...] = m
    acc_ref[...] = weighted(s_diag, m, k_diag)
    pl.loop(0, qi // 2)(lambda jj: weigh_blocks(2 * jj, 2))
    pl.when(odd)(lambda: weigh_blocks(qi - 1, 1))

    o = acc_ref[:, :KV_LORA] / acc_ref[:, KV_LORA:]
    for hh in range(H_B):
        o_ref[:, hh * KV_LORA:(hh + 1) * KV_LORA] = o[hh * tq:(hh + 1) * tq].astype(o_ref.dtype)


def _attn_prompt(qcat, kcat, n_batch, t_len):
    tq = min(ATTN_TILE, t_len)
    assert tq == 2 * LANE
    nq = t_len // tq
    return pl.pallas_call(
        _attn_kernel,
        grid=(n_batch, nq),
        in_specs=[pl.BlockSpec((H_B, tq, QK_PAD), lambda b, i: (0, b * nq + i, 0)),
                  pl.BlockSpec((t_len, QK_PAD), lambda b, i: (b, 0))],
        out_specs=pl.BlockSpec((tq, H_B * KV_LORA), lambda b, i: (b * nq + i, 0)),
        out_shape=jax.ShapeDtypeStruct((n_batch * t_len, H_B * KV_LORA), BF16),
        scratch_shapes=[pltpu.VMEM((max(nq - 1, 1), H_B * tq, tq), F32),
                        pltpu.VMEM((H_B * tq, LANE), F32),
                        pltpu.VMEM((H_B * tq, 2 * KV_LORA), F32)],
        compiler_params=_params("parallel", "parallel"),
        name="attn_prompt",
    )(qcat, kcat)


def _attn_sample_kernel(pt_ref, q_ref, kvn_ref, krn_ref, kv_hbm, kr_hbm, o_ref,
                        kv_a, kr_a, kv_b, kr_b, sem, *, pages, t_new):
    b = pl.program_id(0)
    last = b + 1 == pl.num_programs(0)
    bufs = ((kv_a, kr_a), (kv_b, kr_b))

    def half_copies(seq, half):
        kv_buf, kr_buf = bufs[half]
        copies = []
        for p in range(pages):
            page = pt_ref[seq, half * pages + p]
            span = pl.ds(p * PAGE_SIZE, PAGE_SIZE)
            copies.append(pltpu.make_async_copy(kv_hbm.at[page], kv_buf.at[span, :], sem.at[half, 0]))
            copies.append(pltpu.make_async_copy(kr_hbm.at[page], kr_buf.at[p], sem.at[half, 1]))
        return copies

    @pl.when(b == 0)
    def _():
        for cp in half_copies(b, 0):
            cp.start()

    q = q_ref[0]
    q_lat = q[:, :KV_LORA]
    q_rope = q[:, KV_LORA:KV_LORA + ROPE_DIM]

    def partial_softmax(s, v):
        m_blk = jnp.max(s, axis=1, keepdims=True)
        p = jnp.exp2(s - m_blk)
        return m_blk, jnp.sum(p, axis=1, keepdims=True), _dot(p.astype(BF16), v)

    def half_parts(half):
        kv_buf, kr_buf = bufs[half]
        sub_pages = pages // SAMPLE_SUB_BLOCKS
        sub = sub_pages * PAGE_SIZE
        blocks = range(SAMPLE_SUB_BLOCKS)
        kv = [kv_buf[i * sub:(i + 1) * sub, :].astype(BF16) for i in blocks]
        kr_t = [jnp.concatenate([kr_buf[p].astype(BF16) for p in range(i * sub_pages, (i + 1) * sub_pages)],
                                axis=1) for i in blocks]
        s = [_dot_nt(q_lat, kv[i]) + _dot(q_rope, kr_t[i]) for i in blocks]
        m_blk = [jnp.max(s[i], axis=1, keepdims=True) for i in blocks]
        p = [jnp.exp2(s[i] - m_blk[i]) for i in blocks]
        l_blk = [jnp.sum(p[i], axis=1, keepdims=True) for i in blocks]
        o_blk = [_dot(p[i].astype(BF16), kv[i]) for i in blocks]
        return list(zip(m_blk, l_blk, o_blk))

    for cp in half_copies(b, 0):
        cp.wait()
    for cp in half_copies(b, 1):
        cp.start()
    parts = half_parts(0)

    for cp in half_copies(jnp.where(last, 0, b + 1), 0):
        cp.start()
    for cp in half_copies(b, 1):
        cp.wait()
    parts += half_parts(1)

    kvn = kvn_ref[0].astype(BF16)
    krn = krn_ref[0].astype(BF16)
    s = _dot_nt(q_lat, kvn) + _dot_nt(q_rope, krn)
    r_tok = lax.broadcasted_iota(jnp.int32, s.shape, 0) % t_new
    c_tok = lax.broadcasted_iota(jnp.int32, s.shape, 1)
    parts.append(partial_softmax(jnp.where(c_tok <= r_tok, s, -jnp.inf), kvn))

    m = parts[0][0]
    for m_blk, _, _ in parts[1:]:
        m = jnp.maximum(m, m_blk)
    l = jnp.zeros_like(m)
    acc = jnp.zeros((q.shape[0], KV_LORA), F32)
    for m_blk, l_blk, o_blk in parts:
        w_blk = jnp.exp2(m_blk - m)
        l = l + w_blk * l_blk
        acc = acc + w_blk * o_blk
    o_ref[0] = (acc / l).astype(o_ref.dtype)

    @pl.when(last)
    def _():
        for cp in half_copies(0, 0):
            cp.wait()


def _attn_sample(page_table, q, kv_new, kr_new, cache_kv, cache_kr_t, t_new):
    n_b, n_pages = page_table.shape
    assert n_pages % 2 == 0
    pages = n_pages // 2
    half_rows = pages * PAGE_SIZE
    assert pages % SAMPLE_SUB_BLOCKS == 0
    rows = q.shape[1]
    pad = kv_new.shape[1]
    grid_spec = pltpu.PrefetchScalarGridSpec(
        num_scalar_prefetch=1,
        grid=(n_b,),
        in_specs=[pl.BlockSpec((1, rows, QK_PAD), lambda b, pt: (b, 0, 0)),
                  pl.BlockSpec((1, pad, KV_LORA), lambda b, pt: (b, 0, 0)),
                  pl.BlockSpec((1, pad, ROPE_DIM), lambda b, pt: (b, 0, 0)),
                  pl.BlockSpec(memory_space=pl.ANY),
                  pl.BlockSpec(memory_space=pl.ANY)],
        out_specs=pl.BlockSpec((1, rows, KV_LORA), lambda b, pt: (b, 0, 0)),
        scratch_shapes=[pltpu.VMEM((half_rows, KV_LORA), F32), pltpu.VMEM((pages, ROPE_DIM, PAGE_SIZE), F32),
                        pltpu.VMEM((half_rows, KV_LORA), F32), pltpu.VMEM((pages, ROPE_DIM, PAGE_SIZE), F32),
                        pltpu.SemaphoreType.DMA((2, 2))])
    return pl.pallas_call(
        functools.partial(_attn_sample_kernel, pages=pages, t_new=t_new),
        grid_spec=grid_spec,
        out_shape=jax.ShapeDtypeStruct((n_b, rows, KV_LORA), BF16),
        compiler_params=_params("arbitrary"),
        name="attn_sample",
    )(page_table, q, kv_new, kr_new, cache_kv, cache_kr_t)


def _out_ffn_kernel(x_ref, ha_ref, ga_ref, gb_ref, ol_ref, wuv_ref, wout_ref, gmix_ref,
                    gpre_ref, gpost_ref, wgu_ref, wdn_ref, o_ref, act_ref):
    merged = []
    for pp in range(H_B // 2):
        cols = slice(pp * 2 * V_DIM, (pp + 1) * 2 * V_DIM)
        y_b = _dot(ol_ref[:, cols], wuv_ref[pp])
        y = ga_ref[:, cols].astype(F32) * ha_ref[:, cols].astype(F32) + gb_ref[:, cols].astype(F32) * y_b
        merged.append(y.astype(BF16))
    mix = _dot(jnp.concatenate(merged, axis=1), wout_ref[...])
    x = x_ref[...] + _rms(mix, gmix_ref[...])
    o_ref[...] = _ffn_body(x, gpre_ref[...], gpost_ref[...], wgu_ref, wdn_ref, act_ref)


def _out_ffn(x, ha, ga, gb, o_lat, w):
    m = x.shape[0]
    tm = min(ROW_TILE, m)
    row = pl.BlockSpec((tm, D_MODEL), lambda i: (i, 0))
    vec = _resident((1, D_MODEL))
    return pl.pallas_call(
        _out_ffn_kernel,
        grid=(m // tm,),
        in_specs=[row, row, row, row, row, _resident((H_B // 2, 2 * KV_LORA, 2 * V_DIM)),
                  _resident((D_MODEL, D_MODEL)), vec,
                  vec, vec, _resident((D_MODEL, 2 * D_FF)), _resident((D_FF, D_MODEL))],
        out_specs=row,
        out_shape=jax.ShapeDtypeStruct((m, D_MODEL), F32),
        scratch_shapes=[pltpu.VMEM((tm, D_FF), BF16)],
        compiler_params=_params("parallel"),
        name="out_ffn",
    )(x, ha, ga, gb, o_lat, w["w_uv"], w["w_out"], w["g_mix_post"],
      w["g_ffn2_pre"], w["g_ffn2_post"], w["w_ffn2_gu"], w["w_ffn2_down"])


def _prep_weights(norm_ffn1_pre, norm_ffn1_post, w_ffn1_gu, w_ffn1_down, norm_mix_pre, norm_mix_post,
                  w_in, b_gates, w_uq, norm_q_lat, norm_kv_lat, w_uk, w_uv, norm_mlstm_h, w_out,
                  norm_ffn2_pre, norm_ffn2_post, w_ffn2_gu, w_ffn2_down):
    half = ROPE_DIM // 2
    row = lambda g: g.reshape(1, -1).astype(F32)

    def swap_pad(wr):
        z = jnp.zeros(wr.shape[:-1] + (LANE - ROPE_DIM,), wr.dtype)
        return (jnp.concatenate([wr, z], axis=-1),
                jnp.concatenate([wr[..., half:], wr[..., :half], z], axis=-1))

    sizes = (H_A * DK_A, H_A * DK_A, H_A * DV_A, H_A, H_A, H_A * DV_A, Q_LORA, KV_LORA, ROPE_DIM, D_MODEL, D_MODEL)
    offs = [0]
    for n in sizes:
        offs.append(offs[-1] + n)
    qa, ka, va, ip, fp, oa, cq, ckv, kr, ga, gb = [w_in[:, offs[i]:offs[i + 1]] for i in range(len(sizes))]
    kr_p, kr_s = swap_pad(kr)
    gates = jnp.concatenate([ip, fp, jnp.zeros((D_MODEL, LANE - 2 * H_A), w_in.dtype)], axis=1)
    w_in2 = jnp.concatenate([qa, ka, va, oa, ga, gb, cq, ckv, kr_p, kr_s, gates], axis=1).astype(BF16)
    assert w_in2.shape[1] == IN_COLS_PAD

    uq = w_uq.reshape(Q_LORA, H_B, NOPE_DIM + ROPE_DIM)
    uq_p, uq_s = swap_pad(uq[..., NOPE_DIM:])
    w_uq2 = jnp.concatenate([uq[..., :NOPE_DIM].reshape(Q_LORA, -1), uq_p.reshape(Q_LORA, -1),
                             uq_s.reshape(Q_LORA, -1)], axis=1).astype(BF16)
    return dict(
        g_ffn1_pre=row(norm_ffn1_pre), g_ffn1_post=row(norm_ffn1_post),
        w_ffn1_gu=w_ffn1_gu.astype(BF16), w_ffn1_down=w_ffn1_down.astype(BF16),
        g_ffn2_pre=row(norm_ffn2_pre), g_ffn2_post=row(norm_ffn2_post),
        w_ffn2_gu=w_ffn2_gu.astype(BF16), w_ffn2_down=w_ffn2_down.astype(BF16),
        g_mix_pre=row(norm_mix_pre), g_mix_post=row(norm_mix_post),
        w_in=w_in2, w_uq=w_uq2,
        b_gates=jnp.concatenate([b_gates.astype(F32), jnp.zeros((LANE - 2 * H_A,), F32)]).reshape(1, LANE),
        g_q=row(norm_q_lat), g_kv=row(norm_kv_lat),
        w_uk=_pair_block_diag(jnp.transpose(w_uk, (1, 2, 0)).astype(BF16)),
        w_uv=_pair_block_diag(jnp.transpose(w_uv, (1, 0, 2)).astype(BF16)),
        g_h=row(norm_mlstm_h), w_out=w_out.astype(BF16))


def _pair_block_diag(w):
    h, a, b = w.shape
    w = w.reshape(h // 2, 2, a, b)
    z = jnp.zeros((h // 2, a, b), w.dtype)
    top = jnp.concatenate([w[:, 0], z], axis=2)
    bottom = jnp.concatenate([z, w[:, 1]], axis=2)
    return jnp.concatenate([top, bottom], axis=1)


def _rope_tables(pos):
    half = ROPE_DIM // 2
    inv_freq = ROPE_THETA ** (-jnp.arange(half, dtype=F32) / half)
    ang = pos.astype(F32)[:, None] * inv_freq[None, :]
    cos, sin = jnp.cos(ang), jnp.sin(ang)
    z = jnp.zeros((pos.shape[0], LANE - ROPE_DIM), F32)
    return jnp.concatenate([cos, cos, z], axis=1), jnp.concatenate([-sin, sin, z], axis=1)


def _gates_by_head(gt, n_batch, t_len, chunk):
    g = gt[:, :2 * H_A].reshape(n_batch, t_len // chunk, chunk, 2, H_A)
    return jnp.transpose(g, (0, 4, 1, 3, 2))


def kernel(x_prompt, x_sample, cache_kv_latent, cache_k_rope, state_mlstm_C, state_mlstm_n, state_mlstm_m,
           page_table, norm_ffn1_pre, norm_ffn1_post, w_ffn1_gu, w_ffn1_down, norm_mix_pre, norm_mix_post,
           w_in, b_gates, w_uq, norm_q_lat, norm_kv_lat, w_uk, w_uv, norm_mlstm_h, w_out,
           norm_ffn2_pre, norm_ffn2_post, w_ffn2_gu, w_ffn2_down):
    assert w_in.shape[0] == 1, "single-layer trunk"
    b_p, t_p, _ = x_prompt.shape
    b_s, t_s, _ = x_sample.shape
    past_len = page_table.shape[1] * PAGE_SIZE
    w = _prep_weights(norm_ffn1_pre[0], norm_ffn1_post[0], w_ffn1_gu[0], w_ffn1_down[0], norm_mix_pre[0],
                      norm_mix_post[0], w_in[0], b_gates[0], w_uq[0], norm_q_lat[0], norm_kv_lat[0], w_uk[0],
                      w_uv[0], norm_mlstm_h[0], w_out[0], norm_ffn2_pre[0], norm_ffn2_post[0], w_ffn2_gu[0],
                      w_ffn2_down[0])

    xp = x_prompt.reshape(b_p * t_p, D_MODEL)
    xp = _ffn(xp, w["g_ffn1_pre"], w["g_ffn1_post"], w["w_ffn1_gu"], w["w_ffn1_down"])
    cos_p, sin_p = _rope_tables(jnp.arange(t_p, dtype=jnp.int32))
    qa, ka, va, ga, gb, gt, ckv_p, kr_p, kcat, qcat = _proj(xp, w, cos_p, sin_p)
    chunk = min(MLSTM_CHUNK, t_p)
    ha, c_p, n_p, m_p = _mlstm(qa, ka, va, _gates_by_head(gt, b_p, t_p, chunk), w["g_h"], None, b_p, t_p, chunk, 1)
    o_lat = _attn_prompt(qcat, kcat, b_p, t_p)
    yp = _out_ffn(xp, ha, ga, gb, o_lat, w)

    t_pad = 16
    new_pad = 8
    xs = x_sample.reshape(b_s * t_s, D_MODEL)
    xs = _ffn(xs, w["g_ffn1_pre"], w["g_ffn1_post"], w["w_ffn1_gu"], w["w_ffn1_down"])
    cos_s, sin_s = _rope_tables(past_len + jnp.arange(t_s, dtype=jnp.int32))
    reps = min(ROW_TILE, b_s * t_s) // t_s
    qa, ka, va, ga, gb, gt, ckv_s, kr_s, _, qcat = _proj(xs, w, jnp.tile(cos_s, (reps, 1)), jnp.tile(sin_s, (reps, 1)))

    def pad_t(a, value=0.0):
        a = a.reshape(b_s, t_s, a.shape[-1])
        a = jnp.pad(a, ((0, 0), (0, t_pad - t_s), (0, 0)), constant_values=value)
        return a.reshape(b_s * t_pad, a.shape[-1])

    tok = jnp.arange(b_s * t_pad) % t_pad
    lane = jnp.arange(LANE)
    gt_pad = jnp.where((tok[:, None] >= t_s) & (lane[None, :] < H_A), NEG_BIG, pad_t(gt))
    m0 = jnp.broadcast_to(state_mlstm_m[0][:, :, None, None], (b_s, H_A, 1, LANE))
    ha, c_s, n_s, m_s = _mlstm(pad_t(qa), pad_t(ka), pad_t(va), _gates_by_head(gt_pad, b_s, t_pad, t_pad), w["g_h"],
                               (state_mlstm_C[0], state_mlstm_n[0][:, :, None, :], m0), b_s, t_pad, t_pad,
                               math.gcd(b_s, MLSTM_SAMPLE_SEQS))
    ha = ha.reshape(b_s, t_pad, D_MODEL)[:, :t_s].reshape(b_s * t_s, D_MODEL)

    q_s = qcat.reshape(H_B, b_s, t_s, QK_PAD).transpose(1, 0, 2, 3).reshape(b_s, H_B * t_s, QK_PAD)
    pad_new = lambda a: jnp.pad(a.reshape(b_s, t_s, -1), ((0, 0), (0, new_pad - t_s), (0, 0)))
    o_s = _attn_sample(page_table, q_s, pad_new(ckv_s), pad_new(kr_s), cache_kv_latent[0],
                       jnp.swapaxes(cache_k_rope[0], 1, 2), t_s)
    o_s = o_s.reshape(b_s, H_B, t_s, KV_LORA).transpose(0, 2, 1, 3).reshape(b_s * t_s, H_B * KV_LORA)
    ys = _out_ffn(xs, ha, ga, gb, o_s, w)

    return (yp.reshape(b_p, t_p, D_MODEL), ys.reshape(b_s, t_s, D_MODEL),
            ckv_p.reshape(1, b_p, t_p, KV_LORA), kr_p.reshape(1, b_p, t_p, ROPE_DIM),
            c_p[None], n_p[:, :, 0, :][None], m_p[:, :, 0, 0][None],
            ckv_s.reshape(1, b_s, t_s, KV_LORA), kr_s.reshape(1, b_s, t_s, ROPE_DIM),
            c_s[None], n_s[:, :, 0, :][None], m_s[:, :, 0, 0][None])
```

```python
import functools
import math

import jax
import jax.numpy as jnp
from jax import lax
from jax.experimental import pallas as pl
from jax.experimental.pallas import tpu as pltpu

F32 = jnp.float32
BF16 = jnp.bfloat16

D_MODEL = 1024
H_A = 4
DV_A = D_MODEL // H_A
DK_A = DV_A // 2
V_DIM = 128
H_B = D_MODEL // V_DIM
NOPE_DIM = 128
ROPE_DIM = 64
Q_LORA = 256
KV_LORA = 128
ROPE_THETA = 10000.0
SM_SCALE = (NOPE_DIM + ROPE_DIM) ** -0.5
Q_SCALE = SM_SCALE * math.log2(math.e)
D_FF = 2816
EPS = 1e-6
PAGE_SIZE = 128
QK_PAD = 256

LANE = 128
VMEM_LIMIT = 56 * 1024 * 1024
ROW_TILE = 512
FF_CHUNK = 256
MLSTM_CHUNK = 256
MLSTM_SAMPLE_SEQS = 4
ATTN_TILE = 256
SAMPLE_SUB_BLOCKS = 8
NEG_BIG = -1e30


def _params(*sem):
    return pltpu.CompilerParams(dimension_semantics=sem, vmem_limit_bytes=VMEM_LIMIT)


def _resident(shape):
    return pl.BlockSpec(shape, lambda *_: (0,) * len(shape), pipeline_mode=pl.Buffered(1))


def _rms(x, g):
    return x * lax.rsqrt(jnp.mean(x * x, axis=-1, keepdims=True) + EPS) * g


def _dot(a, b):
    return jnp.dot(a, b, preferred_element_type=F32)


def _dot_nt(a, b):
    return lax.dot_general(a, b, (((1,), (1,)), ((), ())), preferred_element_type=F32)


def _dot_tn(a, b):
    return lax.dot_general(a, b, (((0,), (0,)), ((), ())), preferred_element_type=F32)


def _ffn_body(x, gpre, gpost, wgu_ref, wdn_ref, act_ref):
    h = _rms(x, gpre).astype(BF16)
    for lo in range(0, D_FF, FF_CHUNK):
        g = _dot(h, wgu_ref[:, lo:lo + FF_CHUNK])
        u = _dot(h, wgu_ref[:, D_FF + lo:D_FF + lo + FF_CHUNK])
        act_ref[:, lo:lo + FF_CHUNK] = (g * jax.nn.sigmoid(g) * u).astype(BF16)
    f = _dot(act_ref[...], wdn_ref[...])
    return x + 0.5 * _rms(f, gpost)


def _ffn_kernel(x_ref, gpre_ref, gpost_ref, wgu_ref, wdn_ref, o_ref, act_ref):
    o_ref[...] = _ffn_body(x_ref[...], gpre_ref[...], gpost_ref[...], wgu_ref, wdn_ref, act_ref)


def _ffn(x, gpre, gpost, wgu, wdn):
    m = x.shape[0]
    tm = min(ROW_TILE, m)
    row = pl.BlockSpec((tm, D_MODEL), lambda i: (i, 0))
    return pl.pallas_call(
        _ffn_kernel,
        grid=(m // tm,),
        in_specs=[row, _resident((1, D_MODEL)), _resident((1, D_MODEL)),
                  _resident((D_MODEL, 2 * D_FF)), _resident((D_FF, D_MODEL))],
        out_specs=row,
        out_shape=jax.ShapeDtypeStruct((m, D_MODEL), F32),
        scratch_shapes=[pltpu.VMEM((tm, D_FF), BF16)],
        compiler_params=_params("parallel"),
        name="ffn",
    )(x, gpre, gpost, wgu, wdn)


_O_QA, _O_KA, _O_VA, _O_OA, _O_GA, _O_GB = 0, 512, 1024, 2048, 3072, 4096
_O_CQ, _O_CKV, _O_KR, _O_KRS, _O_GT = 5120, 5376, 5504, 5632, 5760
IN_COLS_PAD = 5888


def _proj_kernel(x_ref, gpre_ref, win_ref, bg_ref, gq_ref, gkv_ref, wuq_ref, wuk_ref, cos_ref, sin_ref,
                 qa_ref, ka_ref, va_ref, ga_ref, gb_ref, gt_ref, ckv_ref, kr_ref, kcat_ref, qcat_ref):
    h = _rms(x_ref[...], gpre_ref[...]).astype(BF16)

    def seg(lo, n):
        return _dot(h, win_ref[:, lo:lo + n])

    qa_ref[...] = seg(_O_QA, H_A * DK_A).astype(BF16)
    ka_ref[...] = (seg(_O_KA, H_A * DK_A) * (DK_A ** -0.5)).astype(BF16)
    va_ref[...] = seg(_O_VA, D_MODEL).astype(BF16)
    ga_ref[...] = (jax.nn.sigmoid(seg(_O_GA, D_MODEL)) * jax.nn.sigmoid(seg(_O_OA, D_MODEL))).astype(BF16)
    gb_ref[...] = jax.nn.sigmoid(seg(_O_GB, D_MODEL)).astype(BF16)

    gt = seg(_O_GT, LANE) + bg_ref[...]
    logsig = jnp.minimum(gt, 0.0) - jnp.log1p(jnp.exp(-jnp.abs(gt)))
    lane = lax.broadcasted_iota(jnp.int32, gt.shape, 1)
    gt_ref[...] = jnp.where(lane < H_A, gt, logsig)

    cos = cos_ref[...]
    sin = sin_ref[...]
    ckv = _rms(seg(_O_CKV, KV_LORA), gkv_ref[...])
    kr = seg(_O_KR, LANE) * cos + seg(_O_KRS, LANE) * sin
    ckv_ref[...] = ckv
    kr_ref[...] = kr[:, :ROPE_DIM]
    kcat_ref[:, :KV_LORA] = ckv.astype(BF16)
    kcat_ref[:, KV_LORA:] = kr.astype(BF16)

    cq = _rms(seg(_O_CQ, Q_LORA), gq_ref[...]).astype(BF16)
    q = _dot(cq, wuq_ref[...])
    q_lat2 = [_dot(q[:, pp * 2 * LANE:(pp + 1) * 2 * LANE].astype(BF16), wuk_ref[pp]) for pp in range(H_B // 2)]
    for hh in range(H_B):
        q_lat = q_lat2[hh // 2][:, (hh % 2) * LANE:(hh % 2 + 1) * LANE]
        lo = H_B * LANE + hh * LANE
        q_rope = q[:, lo:lo + LANE] * cos + q[:, lo + H_B * LANE:lo + (H_B + 1) * LANE] * sin
        qcat_ref[hh, :, :LANE] = (q_lat * Q_SCALE).astype(BF16)
        qcat_ref[hh, :, LANE:] = (q_rope * Q_SCALE).astype(BF16)


def _proj(x, w, cos, sin):
    m = x.shape[0]
    tm = min(ROW_TILE, m)
    ntab = cos.shape[0] // tm

    def row(n):
        return pl.BlockSpec((tm, n), lambda i: (i, 0))

    tab = pl.BlockSpec((tm, LANE), lambda i: (i % ntab, 0))
    outs = [(H_A * DK_A, BF16), (H_A * DK_A, BF16), (D_MODEL, BF16), (D_MODEL, BF16), (D_MODEL, BF16),
            (LANE, F32), (KV_LORA, F32), (ROPE_DIM, F32), (QK_PAD, BF16)]
    out_specs = [row(n) for n, _ in outs] + [pl.BlockSpec((H_B, tm, QK_PAD), lambda i: (0, i, 0))]
    out_shape = ([jax.ShapeDtypeStruct((m, n), dt) for n, dt in outs]
                 + [jax.ShapeDtypeStruct((H_B, m, QK_PAD), BF16)])
    return pl.pallas_call(
        _proj_kernel,
        grid=(m // tm,),
        in_specs=[row(D_MODEL), _resident((1, D_MODEL)), _resident((D_MODEL, IN_COLS_PAD)),
                  _resident((1, LANE)), _resident((1, Q_LORA)), _resident((1, KV_LORA)),
                  _resident((Q_LORA, 3 * H_B * LANE)), _resident((H_B // 2, 2 * NOPE_DIM, 2 * KV_LORA)), tab, tab],
        out_specs=out_specs,
        out_shape=out_shape,
        compiler_params=_params("parallel"),
        name="proj",
    )(x, w["g_mix_pre"], w["w_in"], w["b_gates"], w["g_q"], w["g_kv"], w["w_uq"], w["w_uk"], cos, sin)


def _mlstm_chunks(probs, tri, eye):
    ln = probs[0]["q"].shape[0]
    each = lambda fn, *lists: [fn(*a) for a in zip(*lists)]
    f_row = [p["f_row"] for p in probs]
    i_row = [p["i_row"] for p in probs]
    q = [p["q"] for p in probs]
    k = [p["k"] for p in probs]
    v = [p["v"] for p in probs]
    c_st = [p["c"] for p in probs]
    n_row = [p["n"] for p in probs]
    m = [p["m"] for p in probs]

    b_col = each(lambda f: jnp.sum(jnp.where(tri, f, 0.0), axis=1, keepdims=True), f_row)
    b_row = each(lambda b: jnp.sum(jnp.where(eye, b, 0.0), axis=0, keepdims=True), b_col)
    b_last = each(lambda b: b[:, ln - 1:ln], b_row)
    g_row = each(lambda bl, br, ir: bl - br + ir, b_last, b_row, i_row)
    m_new = each(lambda bl, mm, g: jnp.maximum(bl + mm, jnp.max(g, axis=1, keepdims=True)), b_last, m, g_row)
    a_row = each(lambda g, mn: jnp.exp(g - mn), g_row, m_new)
    decay = each(lambda bl, mm, mn: jnp.exp(bl + mm - mn), b_last, m, m_new)
    a_col = each(lambda a: jnp.sum(jnp.where(eye, a, 0.0), axis=1, keepdims=True), a_row)
    d = each(lambda bc, br, ir: jnp.where(tri, bc - br + ir, -jnp.inf), b_col, b_row, i_row)
    m_inter = each(lambda bc, mm: bc + mm, b_col, m)
    m_t = each(lambda mi, dd: jnp.maximum(mi, jnp.max(dd, axis=1, keepdims=True)), m_inter, d)
    s = each(lambda qq, kk, dd, mt: _dot_nt(qq, kk) * jnp.exp(dd - mt), q, k, d, m_t)
    inter = each(lambda mi, mt: jnp.exp(mi - mt), m_inter, m_t)
    qn = each(lambda qq, nn: jnp.sum(qq.astype(F32) * nn, axis=1, keepdims=True), q, n_row)
    den = each(lambda ss, it, x: jnp.sum(ss, axis=1, keepdims=True) + it * x, s, inter, qn)
    num = each(lambda ss, vv, it, qq, cc: _dot(ss.astype(BF16), vv) + it * _dot_nt(qq, cc.astype(BF16)),
               s, v, inter, q, c_st)
    h = each(lambda nu, de, mt: nu / jnp.maximum(jnp.abs(de), jnp.exp(-mt)), num, den, m_t)
    va = each(lambda vv, a: (vv.astype(F32) * a).astype(BF16), v, a_col)
    c_new = each(lambda dc, cc, x, kk: dc * cc + _dot_tn(x, kk), decay, c_st, va, k)
    n_new = each(lambda dc, nn, a, kk: dc * nn + jnp.sum(a * kk.astype(F32), axis=0, keepdims=True),
                 decay, n_row, a_col, k)
    return list(zip(h, c_new, n_new, m_new))


def _mlstm_kernel(*refs, chunk, n_chunks, n_seq, has_init):
    if has_init:
        q_ref, k_ref, v_ref, gt_ref, gh_ref, c0_ref, n0_ref, m0_ref, h_ref, c_ref, n_ref, m_ref = refs
        c_ref[...] = c0_ref[...]
        n_ref[...] = n0_ref[...]
        m_ref[...] = m0_ref[...]
    else:
        q_ref, k_ref, v_ref, gt_ref, gh_ref, h_ref, c_ref, n_ref, m_ref = refs
        c_ref[...] = jnp.zeros(c_ref.shape, F32)
        n_ref[...] = jnp.zeros(n_ref.shape, F32)
        m_ref[...] = jnp.zeros(m_ref.shape, F32)
    t_idx = lax.broadcasted_iota(jnp.int32, (chunk, chunk), 0)
    s_idx = lax.broadcasted_iota(jnp.int32, (chunk, chunk), 1)
    tri = s_idx <= t_idx
    eye = s_idx == t_idx
    t_len = chunk * n_chunks

    def step(c, carry):
        where, probs = [], []
        for b in range(n_seq):
            start = b * t_len + c * chunk
            rows = pl.ds(start if isinstance(start, int) else pl.multiple_of(start, chunk), chunk)
            for hh in range(H_A):
                qk_cols = slice(hh * DK_A, (hh + 1) * DK_A)
                v_cols = slice(hh * DV_A, (hh + 1) * DV_A)
                where.append((b, hh, rows, v_cols))
                probs.append(dict(q=q_ref[rows, qk_cols], k=k_ref[rows, qk_cols], v=v_ref[rows, v_cols],
                                  i_row=gt_ref[b, hh, c, 0:1, :], f_row=gt_ref[b, hh, c, 1:2, :],
                                  c=c_ref[b, hh], n=n_ref[b, hh], m=m_ref[b, hh][:, :1]))
        results = _mlstm_chunks(probs, tri, eye)
        normed = [_rms(h, gh_ref[:, v_cols]) for (h, _, _, _), (_, _, _, v_cols) in zip(results, where)]
        for (b, hh, rows, v_cols), (_, c_st, n_row, m), hn in zip(where, results, normed):
            h_ref[rows, v_cols] = hn.astype(h_ref.dtype)
            c_ref[b, hh] = c_st
            n_ref[b, hh] = n_row
            m_ref[b, hh] = jnp.broadcast_to(m, (1, LANE))
        return carry

    if n_chunks == 1:
        step(0, 0)
    else:
        lax.fori_loop(0, n_chunks, step, 0)


def _mlstm(qa, ka, va, gates_t, g_h, init, n_batch, t_len, chunk, n_seq):
    has_init = init is not None
    n_chunks = t_len // chunk
    st4 = lambda n: pl.BlockSpec((n_seq, H_A, 1, n), lambda b: (b, 0, 0, 0))
    c_spec = pl.BlockSpec((n_seq, H_A, DV_A, DK_A), lambda b: (b, 0, 0, 0))
    row = lambda n: pl.BlockSpec((n_seq * t_len, n), lambda b: (b, 0))
    in_specs = [row(H_A * DK_A), row(H_A * DK_A), row(D_MODEL),
                pl.BlockSpec((n_seq, H_A, n_chunks, 2, chunk), lambda b: (b, 0, 0, 0, 0)),
                _resident((1, D_MODEL))]
    args = [qa, ka, va, gates_t, g_h]
    if has_init:
        in_specs += [c_spec, st4(DK_A), st4(LANE)]
        args += list(init)
    return pl.pallas_call(
        functools.partial(_mlstm_kernel, chunk=chunk, n_chunks=n_chunks, n_seq=n_seq, has_init=has_init),
        grid=(n_batch // n_seq,),
        in_specs=in_specs,
        out_specs=[row(D_MODEL), c_spec, st4(DK_A), st4(LANE)],
        out_shape=[jax.ShapeDtypeStruct((n_batch * t_len, D_MODEL), BF16),
                   jax.ShapeDtypeStruct((n_batch, H_A, DV_A, DK_A), F32),
                   jax.ShapeDtypeStruct((n_batch, H_A, 1, DK_A), F32),
                   jax.ShapeDtypeStruct((n_batch, H_A, 1, LANE), F32)],
        compiler_params=_params("parallel"),
        name="mlstm_init" if has_init else "mlstm",
    )(*args)


def _attn_kernel(q_ref, k_ref, o_ref, s_ref, mx_ref, acc_ref):
    tq = q_ref.shape[1]
    qi = pl.program_id(1)
    rows = H_B * tq

    def queries():
        return q_ref[...].reshape(rows, QK_PAD)

    def keys(j, n):
        return k_ref[pl.ds(pl.multiple_of(j * tq, tq), n * tq), :]

    def lane_fold(s):
        m = s[:, :LANE]
        for c in range(1, s.shape[1] // LANE):
            m = jnp.maximum(m, s[:, c * LANE:(c + 1) * LANE])
        return m

    def weighted(s, m, kj):
        p = jnp.exp2(s - jnp.concatenate([m] * (s.shape[1] // LANE), axis=1)).astype(BF16)
        return _dot(p, jnp.concatenate([kj[:, :KV_LORA], jnp.ones((kj.shape[0], LANE), BF16)], axis=1))

    def score_blocks(j, n):
        s = _dot_nt(queries(), keys(j, n))
        for c in range(n):
            s_ref[j + c] = s[:, c * tq:(c + 1) * tq]
        mx_ref[...] = jnp.maximum(mx_ref[...], lane_fold(s))

    def weigh_blocks(j, n):
        s = jnp.concatenate([s_ref[j + c] for c in range(n)], axis=1)
        acc_ref[...] += weighted(s, mx_ref[...], keys(j, n))

    k_diag = keys(qi, 1)
    r_idx = lax.broadcasted_iota(jnp.int32, (H_B, tq, tq), 1).reshape(rows, tq)
    c_idx = lax.broadcasted_iota(jnp.int32, (rows, tq), 1)
    s_diag = jnp.where(c_idx <= r_idx, _dot_nt(queries(), k_diag), -jnp.inf)
    odd = qi % 2 == 1

    mx_ref[...] = lane_fold(s_diag)
    pl.loop(0, qi // 2)(lambda jj: score_blocks(2 * jj, 2))
    pl.when(odd)(lambda: score_blocks(qi - 1, 1))

    m = jnp.broadcast_to(jnp.max(mx_ref[...], axis=1, keepdims=True), (rows, LANE))
    mx_ref[...] = m
    acc_ref[...] = weighted(s_diag, m, k_diag)
    pl.loop(0, qi // 2)(lambda jj: weigh_blocks(2 * jj, 2))
    pl.when(odd)(lambda: weigh_blocks(qi - 1, 1))

    o = acc_ref[:, :KV_LORA] / acc_ref[:, KV_LORA:]
    for hh in range(H_B):
        o_ref[:, hh * KV_LORA:(hh + 1) * KV_LORA] = o[hh * tq:(hh + 1) * tq].astype(o_ref.dtype)


def _attn_prompt(qcat, kcat, n_batch, t_len):
    tq = min(ATTN_TILE, t_len)
    assert tq == 2 * LANE
    nq = t_len // tq
    return pl.pallas_call(
        _attn_kernel,
        grid=(n_batch, nq),
        in_specs=[pl.BlockSpec((H_B, tq, QK_PAD), lambda b, i: (0, b * nq + i, 0)),
                  pl.BlockSpec((t_len, QK_PAD), lambda b, i: (b, 0))],
        out_specs=pl.BlockSpec((tq, H_B * KV_LORA), lambda b, i: (b * nq + i, 0)),
        out_shape=jax.ShapeDtypeStruct((n_batch * t_len, H_B * KV_LORA), BF16),
        scratch_shapes=[pltpu.VMEM((max(nq - 1, 1), H_B * tq, tq), F32),
                        pltpu.VMEM((H_B * tq, LANE), F32),
                        pltpu.VMEM((H_B * tq, 2 * KV_LORA), F32)],
        compiler_params=_params("parallel", "parallel"),
        name="attn_prompt",
    )(qcat, kcat)


def _attn_sample_kernel(pt_ref, q_ref, kvn_ref, krn_ref, kv_hbm, kr_hbm, o_ref,
                        kv_0, kr_0, kv_1, kr_1, sem, *, pages, t_new):
    g = pl.program_id(0)
    last = g + 1 == pl.num_programs(0)
    bufs = ((kv_0, kr_0), (kv_1, kr_1))

    def seq_copies(seq, slot):
        kv_buf, kr_buf = bufs[slot]
        kv_cp, kr_cp = [], []
        for p in range(pages):
            page = pt_ref[seq, p]
            kv_cp.append(pltpu.make_async_copy(kv_hbm.at[page], kv_buf.at[pl.ds(p * PAGE_SIZE, PAGE_SIZE), :],
                                               sem.at[slot, 0]))
            kr_cp.append(pltpu.make_async_copy(kr_hbm.at[page], kr_buf.at[p], sem.at[slot, 1]))
        return kv_cp, kr_cp

    def start(seq, slot):
        for cp in sum(seq_copies(seq, slot), []):
            cp.start()

    def attend(i, slot):
        q = q_ref[i]
        q_lat = q[:, :KV_LORA]
        q_rope = q[:, KV_LORA:KV_LORA + ROPE_DIM]
        kv_buf, kr_buf = bufs[slot]
        sub_pages = pages // SAMPLE_SUB_BLOCKS
        sub = sub_pages * PAGE_SIZE
        blocks = range(SAMPLE_SUB_BLOCKS)
        kv = [kv_buf[j * sub:(j + 1) * sub, :].astype(BF16) for j in blocks]
        kr_t = [jnp.concatenate([kr_buf[p].astype(BF16) for p in range(j * sub_pages, (j + 1) * sub_pages)],
                                axis=1) for j in blocks]
        s = [_dot_nt(q_lat, kv[j]) + _dot(q_rope, kr_t[j]) for j in blocks]
        kvn = kvn_ref[i].astype(BF16)
        s_new = _dot_nt(q_lat, kvn) + _dot_nt(q_rope, krn_ref[i].astype(BF16))
        r_tok = lax.broadcasted_iota(jnp.int32, s_new.shape, 0) % t_new
        c_tok = lax.broadcasted_iota(jnp.int32, s_new.shape, 1)
        s.append(jnp.where(c_tok <= r_tok, s_new, -jnp.inf))
        kv.append(kvn)
        m_blk = [jnp.max(x, axis=1, keepdims=True) for x in s]
        p = [jnp.exp2(x - mb) for x, mb in zip(s, m_blk)]
        l_blk = [jnp.sum(x, axis=1, keepdims=True) for x in p]
        o_blk = [_dot(x.astype(BF16), v) for x, v in zip(p, kv)]
        m = m_blk[0]
        for mb in m_blk[1:]:
            m = jnp.maximum(m, mb)
        l = jnp.zeros_like(m)
        acc = jnp.zeros((q.shape[0], KV_LORA), F32)
        for mb, lb, ob in zip(m_blk, l_blk, o_blk):
            w_blk = jnp.exp2(mb - m)
            l = l + w_blk * lb
            acc = acc + w_blk * ob
        o_ref[i] = (acc / l).astype(o_ref.dtype)

    @pl.when(g == 0)
    def _():
        start(0, 0)

    start(2 * g + 1, 1)
    for cp in sum(seq_copies(2 * g, 0), []):
        cp.wait()
    attend(0, 0)
    start(jnp.where(last, 0, 2 * g + 2), 0)
    for cp in sum(seq_copies(2 * g + 1, 1), []):
        cp.wait()
    attend(1, 1)

    @pl.when(last)
    def _():
        for cp in sum(seq_copies(0, 0), []):
            cp.wait()


def _attn_sample(page_table, q, kv_new, kr_new, cache_kv, cache_kr_t, t_new):
    n_b, n_pages = page_table.shape
    assert n_b % 2 == 0 and n_pages % SAMPLE_SUB_BLOCKS == 0
    past = n_pages * PAGE_SIZE
    rows = q.shape[1]
    pad = kv_new.shape[1]
    grid_spec = pltpu.PrefetchScalarGridSpec(
        num_scalar_prefetch=1,
        grid=(n_b // 2,),
        in_specs=[pl.BlockSpec((2, rows, QK_PAD), lambda g, pt: (g, 0, 0)),
                  pl.BlockSpec((2, pad, KV_LORA), lambda g, pt: (g, 0, 0)),
                  pl.BlockSpec((2, pad, ROPE_DIM), lambda g, pt: (g, 0, 0)),
                  pl.BlockSpec(memory_space=pl.ANY),
                  pl.BlockSpec(memory_space=pl.ANY)],
        out_specs=pl.BlockSpec((2, rows, KV_LORA), lambda g, pt: (g, 0, 0)),
        scratch_shapes=[pltpu.VMEM((past, KV_LORA), F32), pltpu.VMEM((n_pages, ROPE_DIM, PAGE_SIZE), F32),
                        pltpu.VMEM((past, KV_LORA), F32), pltpu.VMEM((n_pages, ROPE_DIM, PAGE_SIZE), F32),
                        pltpu.SemaphoreType.DMA((2, 2))])
    return pl.pallas_call(
        functools.partial(_attn_sample_kernel, pages=n_pages, t_new=t_new),
        grid_spec=grid_spec,
        out_shape=jax.ShapeDtypeStruct((n_b, rows, KV_LORA), BF16),
        compiler_params=_params("arbitrary"),
        name="attn_sample",
    )(page_table, q, kv_new, kr_new, cache_kv, cache_kr_t)


def _out_ffn_kernel(x_ref, ha_ref, ga_ref, gb_ref, ol_ref, wuv_ref, wout_ref, gmix_ref,
                    gpre_ref, gpost_ref, wgu_ref, wdn_ref, o_ref, act_ref):
    merged = []
    for pp in range(H_B // 2):
        cols = slice(pp * 2 * V_DIM, (pp + 1) * 2 * V_DIM)
        y_b = _dot(ol_ref[:, cols], wuv_ref[pp])
        y = ga_ref[:, cols].astype(F32) * ha_ref[:, cols].astype(F32) + gb_ref[:, cols].astype(F32) * y_b
        merged.append(y.astype(BF16))
    mix = _dot(jnp.concatenate(merged, axis=1), wout_ref[...])
    x = x_ref[...] + _rms(mix, gmix_ref[...])
    o_ref[...] = _ffn_body(x, gpre_ref[...], gpost_ref[...], wgu_ref, wdn_ref, act_ref)


def _out_ffn(x, ha, ga, gb, o_lat, w):
    m = x.shape[0]
    tm = min(ROW_TILE, m)
    row = pl.BlockSpec((tm, D_MODEL), lambda i: (i, 0))
    vec = _resident((1, D_MODEL))
    return pl.pallas_call(
        _out_ffn_kernel,
        grid=(m // tm,),
        in_specs=[row, row, row, row, row, _resident((H_B // 2, 2 * KV_LORA, 2 * V_DIM)),
                  _resident((D_MODEL, D_MODEL)), vec,
                  vec, vec, _resident((D_MODEL, 2 * D_FF)), _resident((D_FF, D_MODEL))],
        out_specs=row,
        out_shape=jax.ShapeDtypeStruct((m, D_MODEL), F32),
        scratch_shapes=[pltpu.VMEM((tm, D_FF), BF16)],
        compiler_params=_params("parallel"),
        name="out_ffn",
    )(x, ha, ga, gb, o_lat, w["w_uv"], w["w_out"], w["g_mix_post"],
      w["g_ffn2_pre"], w["g_ffn2_post"], w["w_ffn2_gu"], w["w_ffn2_down"])


def _prep_weights(norm_ffn1_pre, norm_ffn1_post, w_ffn1_gu, w_ffn1_down, norm_mix_pre, norm_mix_post,
                  w_in, b_gates, w_uq, norm_q_lat, norm_kv_lat, w_uk, w_uv, norm_mlstm_h, w_out,
                  norm_ffn2_pre, norm_ffn2_post, w_ffn2_gu, w_ffn2_down):
    half = ROPE_DIM // 2
    row = lambda g: g.reshape(1, -1).astype(F32)

    def swap_pad(wr):
        z = jnp.zeros(wr.shape[:-1] + (LANE - ROPE_DIM,), wr.dtype)
        return (jnp.concatenate([wr, z], axis=-1),
                jnp.concatenate([wr[..., half:], wr[..., :half], z], axis=-1))

    sizes = (H_A * DK_A, H_A * DK_A, H_A * DV_A, H_A, H_A, H_A * DV_A, Q_LORA, KV_LORA, ROPE_DIM, D_MODEL, D_MODEL)
    offs = [0]
    for n in sizes:
        offs.append(offs[-1] + n)
    qa, ka, va, ip, fp, oa, cq, ckv, kr, ga, gb = [w_in[:, offs[i]:offs[i + 1]] for i in range(len(sizes))]
    kr_p, kr_s = swap_pad(kr)
    gates = jnp.concatenate([ip, fp, jnp.zeros((D_MODEL, LANE - 2 * H_A), w_in.dtype)], axis=1)
    w_in2 = jnp.concatenate([qa, ka, va, oa, ga, gb, cq, ckv, kr_p, kr_s, gates], axis=1).astype(BF16)
    assert w_in2.shape[1] == IN_COLS_PAD

    uq = w_uq.reshape(Q_LORA, H_B, NOPE_DIM + ROPE_DIM)
    uq_p, uq_s = swap_pad(uq[..., NOPE_DIM:])
    w_uq2 = jnp.concatenate([uq[..., :NOPE_DIM].reshape(Q_LORA, -1), uq_p.reshape(Q_LORA, -1),
                             uq_s.reshape(Q_LORA, -1)], axis=1).astype(BF16)
    return dict(
        g_ffn1_pre=row(norm_ffn1_pre), g_ffn1_post=row(norm_ffn1_post),
        w_ffn1_gu=w_ffn1_gu.astype(BF16), w_ffn1_down=w_ffn1_down.astype(BF16),
        g_ffn2_pre=row(norm_ffn2_pre), g_ffn2_post=row(norm_ffn2_post),
        w_ffn2_gu=w_ffn2_gu.astype(BF16), w_ffn2_down=w_ffn2_down.astype(BF16),
        g_mix_pre=row(norm_mix_pre), g_mix_post=row(norm_mix_post),
        w_in=w_in2, w_uq=w_uq2,
        b_gates=jnp.concatenate([b_gates.astype(F32), jnp.zeros((LANE - 2 * H_A,), F32)]).reshape(1, LANE),
        g_q=row(norm_q_lat), g_kv=row(norm_kv_lat),
        w_uk=_pair_block_diag(jnp.transpose(w_uk, (1, 2, 0)).astype(BF16)),
        w_uv=_pair_block_diag(jnp.transpose(w_uv, (1, 0, 2)).astype(BF16)),
        g_h=row(norm_mlstm_h), w_out=w_out.astype(BF16))


def _pair_block_diag(w):
    h, a, b = w.shape
    w = w.reshape(h // 2, 2, a, b)
    z = jnp.zeros((h // 2, a, b), w.dtype)
    top = jnp.concatenate([w[:, 0], z], axis=2)
    bottom = jnp.concatenate([z, w[:, 1]], axis=2)
    return jnp.concatenate([top, bottom], axis=1)


def _rope_tables(pos):
    half = ROPE_DIM // 2
    inv_freq = ROPE_THETA ** (-jnp.arange(half, dtype=F32) / half)
    ang = pos.astype(F32)[:, None] * inv_freq[None, :]
    cos, sin = jnp.cos(ang), jnp.sin(ang)
    z = jnp.zeros((pos.shape[0], LANE - ROPE_DIM), F32)
    return jnp.concatenate([cos, cos, z], axis=1), jnp.concatenate([-sin, sin, z], axis=1)


def _gates_by_head(gt, n_batch, t_len, chunk):
    g = gt[:, :2 * H_A].reshape(n_batch, t_len // chunk, chunk, 2, H_A)
    return jnp.transpose(g, (0, 4, 1, 3, 2))


def kernel(x_prompt, x_sample, cache_kv_latent, cache_k_rope, state_mlstm_C, state_mlstm_n, state_mlstm_m,
           page_table, norm_ffn1_pre, norm_ffn1_post, w_ffn1_gu, w_ffn1_down, norm_mix_pre, norm_mix_post,
           w_in, b_gates, w_uq, norm_q_lat, norm_kv_lat, w_uk, w_uv, norm_mlstm_h, w_out,
           norm_ffn2_pre, norm_ffn2_post, w_ffn2_gu, w_ffn2_down):
    assert w_in.shape[0] == 1, "single-layer trunk"
    b_p, t_p, _ = x_prompt.shape
    b_s, t_s, _ = x_sample.shape
    past_len = page_table.shape[1] * PAGE_SIZE
    w = _prep_weights(norm_ffn1_pre[0], norm_ffn1_post[0], w_ffn1_gu[0], w_ffn1_down[0], norm_mix_pre[0],
                      norm_mix_post[0], w_in[0], b_gates[0], w_uq[0], norm_q_lat[0], norm_kv_lat[0], w_uk[0],
                      w_uv[0], norm_mlstm_h[0], w_out[0], norm_ffn2_pre[0], norm_ffn2_post[0], w_ffn2_gu[0],
                      w_ffn2_down[0])

    xp = x_prompt.reshape(b_p * t_p, D_MODEL)
    xp = _ffn(xp, w["g_ffn1_pre"], w["g_ffn1_post"], w["w_ffn1_gu"], w["w_ffn1_down"])
    cos_p, sin_p = _rope_tables(jnp.arange(t_p, dtype=jnp.int32))
    qa, ka, va, ga, gb, gt, ckv_p, kr_p, kcat, qcat = _proj(xp, w, cos_p, sin_p)
    chunk = min(MLSTM_CHUNK, t_p)
    ha, c_p, n_p, m_p = _mlstm(qa, ka, va, _gates_by_head(gt, b_p, t_p, chunk), w["g_h"], None, b_p, t_p, chunk, 1)
    o_lat = _attn_prompt(qcat, kcat, b_p, t_p)
    yp = _out_ffn(xp, ha, ga, gb, o_lat, w)

    t_pad = 16
    new_pad = 8
    xs = x_sample.reshape(b_s * t_s, D_MODEL)
    xs = _ffn(xs, w["g_ffn1_pre"], w["g_ffn1_post"], w["w_ffn1_gu"], w["w_ffn1_down"])
    cos_s, sin_s = _rope_tables(past_len + jnp.arange(t_s, dtype=jnp.int32))
    reps = min(ROW_TILE, b_s * t_s) // t_s
    qa, ka, va, ga, gb, gt, ckv_s, kr_s, _, qcat = _proj(xs, w, jnp.tile(cos_s, (reps, 1)), jnp.tile(sin_s, (reps, 1)))

    def pad_t(a, value=0.0):
        a = a.reshape(b_s, t_s, a.shape[-1])
        a = jnp.pad(a, ((0, 0), (0, t_pad - t_s), (0, 0)), constant_values=value)
        return a.reshape(b_s * t_pad, a.shape[-1])

    tok = jnp.arange(b_s * t_pad) % t_pad
    lane = jnp.arange(LANE)
    gt_pad = jnp.where((tok[:, None] >= t_s) & (lane[None, :] < H_A), NEG_BIG, pad_t(gt))
    m0 = jnp.broadcast_to(state_mlstm_m[0][:, :, None, None], (b_s, H_A, 1, LANE))
    ha, c_s, n_s, m_s = _mlstm(pad_t(qa), pad_t(ka), pad_t(va), _gates_by_head(gt_pad, b_s, t_pad, t_pad), w["g_h"],
                               (state_mlstm_C[0], state_mlstm_n[0][:, :, None, :], m0), b_s, t_pad, t_pad,
                               math.gcd(b_s, MLSTM_SAMPLE_SEQS))
    ha = ha.reshape(b_s, t_pad, D_MODEL)[:, :t_s].reshape(b_s * t_s, D_MODEL)

    q_s = qcat.reshape(H_B, b_s, t_s, QK_PAD).transpose(1, 0, 2, 3).reshape(b_s, H_B * t_s, QK_PAD)
    pad_new = lambda a: jnp.pad(a.reshape(b_s, t_s, -1), ((0, 0), (0, new_pad - t_s), (0, 0)))
    o_s = _attn_sample(page_table, q_s, pad_new(ckv_s), pad_new(kr_s), cache_kv_latent[0],
                       jnp.swapaxes(cache_k_rope[0], 1, 2), t_s)
    o_s = o_s.reshape(b_s, H_B, t_s, KV_LORA).transpose(0, 2, 1, 3).reshape(b_s * t_s, H_B * KV_LORA)
    ys = _out_ffn(xs, ha, ga, gb, o_s, w)

    return (yp.reshape(b_p, t_p, D_MODEL), ys.reshape(b_s, t_s, D_MODEL),
            ckv_p.reshape(1, b_p, t_p, KV_LORA), kr_p.reshape(1, b_p, t_p, ROPE_DIM),
            c_p[None], n_p[:, :, 0, :][None], m_p[:, :, 0, 0][None],
            ckv_s.reshape(1, b_s, t_s, KV_LORA), kr_s.reshape(1, b_s, t_s, ROPE_DIM),
            c_s[None], n_s[:, :, 0, :][None], m_s[:, :, 0, 0][None])
```

```python
import functools
import math

import jax
import jax.numpy as jnp
from jax import lax
from jax.experimental import pallas as pl
from jax.experimental.pallas import tpu as pltpu

F32 = jnp.float32
BF16 = jnp.bfloat16

D_MODEL = 1024
H_A = 4
DV_A = D_MODEL // H_A
DK_A = DV_A // 2
V_DIM = 128
H_B = D_MODEL // V_DIM
NOPE_DIM = 128
ROPE_DIM = 64
Q_LORA = 256
KV_LORA = 128
ROPE_THETA = 10000.0
SM_SCALE = (NOPE_DIM + ROPE_DIM) ** -0.5
Q_SCALE = SM_SCALE * math.log2(math.e)
D_FF = 2816
EPS = 1e-6
PAGE_SIZE = 128
QK_PAD = 256

LANE = 128
VMEM_LIMIT = 56 * 1024 * 1024
ROW_TILE = 512
FF_CHUNK = 256
MLSTM_CHUNK = 256
MLSTM_SAMPLE_SEQS = 4
ATTN_TILE = 256
SAMPLE_SUB_BLOCKS = 8
NEG_BIG = -1e30


def _params(*sem):
    return pltpu.CompilerParams(dimension_semantics=sem, vmem_limit_bytes=VMEM_LIMIT)


def _resident(shape):
    return pl.BlockSpec(shape, lambda *_: (0,) * len(shape), pipeline_mode=pl.Buffered(1))


def _rms(x, g):
    return x * lax.rsqrt(jnp.mean(x * x, axis=-1, keepdims=True) + EPS) * g


def _dot(a, b):
    return jnp.dot(a, b, preferred_element_type=F32)


def _dot_nt(a, b):
    return lax.dot_general(a, b, (((1,), (1,)), ((), ())), preferred_element_type=F32)


def _dot_tn(a, b):
    return lax.dot_general(a, b, (((0,), (0,)), ((), ())), preferred_element_type=F32)


def _ffn_body(x, gpre, gpost, wgu_ref, wdn_ref, act_ref):
    h = _rms(x, gpre).astype(BF16)
    for lo in range(0, D_FF, FF_CHUNK):
        g = _dot(h, wgu_ref[:, lo:lo + FF_CHUNK])
        u = _dot(h, wgu_ref[:, D_FF + lo:D_FF + lo + FF_CHUNK])
        act_ref[:, lo:lo + FF_CHUNK] = (g * jax.nn.sigmoid(g) * u).astype(BF16)
    f = _dot(act_ref[...], wdn_ref[...])
    return x + 0.5 * _rms(f, gpost)


def _ffn_kernel(x_ref, gpre_ref, gpost_ref, wgu_ref, wdn_ref, o_ref, act_ref):
    o_ref[...] = _ffn_body(x_ref[...], gpre_ref[...], gpost_ref[...], wgu_ref, wdn_ref, act_ref)


def _ffn(x, gpre, gpost, wgu, wdn):
    m = x.shape[0]
    tm = min(ROW_TILE, m)
    row = pl.BlockSpec((tm, D_MODEL), lambda i: (i, 0))
    return pl.pallas_call(
        _ffn_kernel,
        grid=(m // tm,),
        in_specs=[row, _resident((1, D_MODEL)), _resident((1, D_MODEL)),
                  _resident((D_MODEL, 2 * D_FF)), _resident((D_FF, D_MODEL))],
        out_specs=row,
        out_shape=jax.ShapeDtypeStruct((m, D_MODEL), F32),
        scratch_shapes=[pltpu.VMEM((tm, D_FF), BF16)],
        compiler_params=_params("parallel"),
        name="ffn",
    )(x, gpre, gpost, wgu, wdn)


_O_QA, _O_KA, _O_VA, _O_OA, _O_GA, _O_GB = 0, 512, 1024, 2048, 3072, 4096
_O_CQ, _O_CKV = 5120, 5376
IN_COLS_PAD = 5888


def _proj_kernel(x_ref, gpre_ref, win_ref, bg_ref, gq_ref, gkv_ref, wuq_ref, wuk_ref, cos_ref, sin_ref,
                 qa_ref, ka_ref, va_ref, ga_ref, gb_ref, gt_ref, ckv_ref, kr_ref, kcat_ref, qcat_ref):
    h = _rms(x_ref[...], gpre_ref[...]).astype(BF16)

    def seg(lo, n):
        return _dot(h, win_ref[:, lo:lo + n])

    qa_ref[...] = seg(_O_QA, H_A * DK_A).astype(BF16)
    ka_ref[...] = (seg(_O_KA, H_A * DK_A) * (DK_A ** -0.5)).astype(BF16)
    va_ref[...] = seg(_O_VA, D_MODEL).astype(BF16)
    ga_ref[...] = (jax.nn.sigmoid(seg(_O_GA, D_MODEL)) * jax.nn.sigmoid(seg(_O_OA, D_MODEL))).astype(BF16)
    gb_ref[...] = jax.nn.sigmoid(seg(_O_GB, D_MODEL)).astype(BF16)

    tail = seg(_O_CKV, 4 * LANE)
    ckv_raw, kr_raw, krs_raw, gt_raw = (tail[:, c * LANE:(c + 1) * LANE] for c in range(4))

    gt = gt_raw + bg_ref[...]
    logsig = jnp.minimum(gt, 0.0) - jnp.log1p(jnp.exp(-jnp.abs(gt)))
    lane = lax.broadcasted_iota(jnp.int32, gt.shape, 1)
    gt_ref[...] = jnp.where(lane < H_A, gt, logsig)

    cos = cos_ref[...]
    sin = sin_ref[...]
    ckv = _rms(ckv_raw, gkv_ref[...])
    kr = kr_raw * cos + krs_raw * sin
    ckv_ref[...] = ckv
    kr_ref[...] = kr[:, :ROPE_DIM]
    kcat_ref[:, :KV_LORA] = ckv.astype(BF16)
    kcat_ref[:, KV_LORA:] = kr.astype(BF16)

    cq = _rms(seg(_O_CQ, Q_LORA), gq_ref[...]).astype(BF16)
    q = _dot(cq, wuq_ref[...])
    q_lat2 = [_dot(q[:, pp * 2 * LANE:(pp + 1) * 2 * LANE].astype(BF16), wuk_ref[pp]) for pp in range(H_B // 2)]
    for hh in range(H_B):
        q_lat = q_lat2[hh // 2][:, (hh % 2) * LANE:(hh % 2 + 1) * LANE]
        lo = H_B * LANE + hh * LANE
        q_rope = q[:, lo:lo + LANE] * cos + q[:, lo + H_B * LANE:lo + (H_B + 1) * LANE] * sin
        qcat_ref[hh, :, :LANE] = (q_lat * Q_SCALE).astype(BF16)
        qcat_ref[hh, :, LANE:] = (q_rope * Q_SCALE).astype(BF16)


def _proj(x, w, cos, sin):
    m = x.shape[0]
    tm = min(ROW_TILE, m)
    ntab = cos.shape[0] // tm

    def row(n):
        return pl.BlockSpec((tm, n), lambda i: (i, 0))

    tab = pl.BlockSpec((tm, LANE), lambda i: (i % ntab, 0))
    outs = [(H_A * DK_A, BF16), (H_A * DK_A, BF16), (D_MODEL, BF16), (D_MODEL, BF16), (D_MODEL, BF16),
            (LANE, F32), (KV_LORA, F32), (ROPE_DIM, F32), (QK_PAD, BF16)]
    out_specs = [row(n) for n, _ in outs] + [pl.BlockSpec((H_B, tm, QK_PAD), lambda i: (0, i, 0))]
    out_shape = ([jax.ShapeDtypeStruct((m, n), dt) for n, dt in outs]
                 + [jax.ShapeDtypeStruct((H_B, m, QK_PAD), BF16)])
    return pl.pallas_call(
        _proj_kernel,
        grid=(m // tm,),
        in_specs=[row(D_MODEL), _resident((1, D_MODEL)), _resident((D_MODEL, IN_COLS_PAD)),
                  _resident((1, LANE)), _resident((1, Q_LORA)), _resident((1, KV_LORA)),
                  _resident((Q_LORA, 3 * H_B * LANE)), _resident((H_B // 2, 2 * NOPE_DIM, 2 * KV_LORA)), tab, tab],
        out_specs=out_specs,
        out_shape=out_shape,
        compiler_params=_params("parallel"),
        name="proj",
    )(x, w["g_mix_pre"], w["w_in"], w["b_gates"], w["g_q"], w["g_kv"], w["w_uq"], w["w_uk"], cos, sin)


def _mlstm_chunks(probs, tri, eye):
    ln = probs[0]["q"].shape[0]
    each = lambda fn, *lists: [fn(*a) for a in zip(*lists)]
    f_row = [p["f_row"] for p in probs]
    i_row = [p["i_row"] for p in probs]
    q = [p["q"] for p in probs]
    k = [p["k"] for p in probs]
    v = [p["v"] for p in probs]
    c_st = [p["c"] for p in probs]
    n_row = [p["n"] for p in probs]
    m = [p["m"] for p in probs]

    b_col = each(lambda f: jnp.sum(jnp.where(tri, f, 0.0), axis=1, keepdims=True), f_row)
    b_row = each(lambda b: jnp.sum(jnp.where(eye, b, 0.0), axis=0, keepdims=True), b_col)
    b_last = each(lambda b: b[:, ln - 1:ln], b_row)
    g_row = each(lambda bl, br, ir: bl - br + ir, b_last, b_row, i_row)
    m_new = each(lambda bl, mm, g: jnp.maximum(bl + mm, jnp.max(g, axis=1, keepdims=True)), b_last, m, g_row)
    a_row = each(lambda g, mn: jnp.exp(g - mn), g_row, m_new)
    decay = each(lambda bl, mm, mn: jnp.exp(bl + mm - mn), b_last, m, m_new)
    a_col = each(lambda a: jnp.sum(jnp.where(eye, a, 0.0), axis=1, keepdims=True), a_row)
    d = each(lambda bc, br, ir: jnp.where(tri, bc - br + ir, -jnp.inf), b_col, b_row, i_row)
    m_inter = each(lambda bc, mm: bc + mm, b_col, m)
    m_t = each(lambda mi, dd: jnp.maximum(mi, jnp.max(dd, axis=1, keepdims=True)), m_inter, d)
    s = each(lambda qq, kk, dd, mt: _dot_nt(qq, kk) * jnp.exp(dd - mt), q, k, d, m_t)
    inter = each(lambda mi, mt: jnp.exp(mi - mt), m_inter, m_t)
    qn = each(lambda qq, nn: jnp.sum(qq.astype(F32) * nn, axis=1, keepdims=True), q, n_row)
    den = each(lambda ss, it, x: jnp.sum(ss, axis=1, keepdims=True) + it * x, s, inter, qn)
    num = each(lambda ss, vv, it, qq, cc: _dot(ss.astype(BF16), vv) + it * _dot_nt(qq, cc.astype(BF16)),
               s, v, inter, q, c_st)
    h = each(lambda nu, de, mt: nu / jnp.maximum(jnp.abs(de), jnp.exp(-mt)), num, den, m_t)
    va = each(lambda vv, a: (vv.astype(F32) * a).astype(BF16), v, a_col)
    c_new = each(lambda dc, cc, x, kk: dc * cc + _dot_tn(x, kk), decay, c_st, va, k)
    n_new = each(lambda dc, nn, a, kk: dc * nn + jnp.sum(a * kk.astype(F32), axis=0, keepdims=True),
                 decay, n_row, a_col, k)
    return list(zip(h, c_new, n_new, m_new))


def _mlstm_kernel(*refs, chunk, n_chunks, n_seq, has_init):
    if has_init:
        q_ref, k_ref, v_ref, gt_ref, gh_ref, c0_ref, n0_ref, m0_ref, h_ref, c_ref, n_ref, m_ref = refs
        c_ref[...] = c0_ref[...]
        n_ref[...] = n0_ref[...]
        m_ref[...] = m0_ref[...]
    else:
        q_ref, k_ref, v_ref, gt_ref, gh_ref, h_ref, c_ref, n_ref, m_ref = refs
        c_ref[...] = jnp.zeros(c_ref.shape, F32)
        n_ref[...] = jnp.zeros(n_ref.shape, F32)
        m_ref[...] = jnp.zeros(m_ref.shape, F32)
    t_idx = lax.broadcasted_iota(jnp.int32, (chunk, chunk), 0)
    s_idx = lax.broadcasted_iota(jnp.int32, (chunk, chunk), 1)
    tri = s_idx <= t_idx
    eye = s_idx == t_idx
    t_len = chunk * n_chunks

    def step(c, carry):
        where, probs = [], []
        for b in range(n_seq):
            start = b * t_len + c * chunk
            rows = pl.ds(start if isinstance(start, int) else pl.multiple_of(start, chunk), chunk)
            for hh in range(H_A):
                qk_cols = slice(hh * DK_A, (hh + 1) * DK_A)
                v_cols = slice(hh * DV_A, (hh + 1) * DV_A)
                where.append((b, hh, rows, v_cols))
                probs.append(dict(q=q_ref[rows, qk_cols], k=k_ref[rows, qk_cols], v=v_ref[rows, v_cols],
                                  i_row=gt_ref[b, hh, c, 0:1, :], f_row=gt_ref[b, hh, c, 1:2, :],
                                  c=c_ref[b, hh], n=n_ref[b, hh], m=m_ref[b, hh][:, :1]))
        results = _mlstm_chunks(probs, tri, eye)
        normed = [_rms(h, gh_ref[:, v_cols]) for (h, _, _, _), (_, _, _, v_cols) in zip(results, where)]
        for (b, hh, rows, v_cols), (_, c_st, n_row, m), hn in zip(where, results, normed):
            h_ref[rows, v_cols] = hn.astype(h_ref.dtype)
            c_ref[b, hh] = c_st
            n_ref[b, hh] = n_row
            m_ref[b, hh] = jnp.broadcast_to(m, (1, LANE))
        return carry

    if n_chunks == 1:
        step(0, 0)
    else:
        lax.fori_loop(0, n_chunks, step, 0)


def _mlstm(qa, ka, va, gates_t, g_h, init, n_batch, t_len, chunk, n_seq):
    has_init = init is not None
    n_chunks = t_len // chunk
    st4 = lambda n: pl.BlockSpec((n_seq, H_A, 1, n), lambda b: (b, 0, 0, 0))
    c_spec = pl.BlockSpec((n_seq, H_A, DV_A, DK_A), lambda b: (b, 0, 0, 0))
    row = lambda n: pl.BlockSpec((n_seq * t_len, n), lambda b: (b, 0))
    in_specs = [row(H_A * DK_A), row(H_A * DK_A), row(D_MODEL),
                pl.BlockSpec((n_seq, H_A, n_chunks, 2, chunk), lambda b: (b, 0, 0, 0, 0)),
                _resident((1, D_MODEL))]
    args = [qa, ka, va, gates_t, g_h]
    if has_init:
        in_specs += [c_spec, st4(DK_A), st4(LANE)]
        args += list(init)
    return pl.pallas_call(
        functools.partial(_mlstm_kernel, chunk=chunk, n_chunks=n_chunks, n_seq=n_seq, has_init=has_init),
        grid=(n_batch // n_seq,),
        in_specs=in_specs,
        out_specs=[row(D_MODEL), c_spec, st4(DK_A), st4(LANE)],
        out_shape=[jax.ShapeDtypeStruct((n_batch * t_len, D_MODEL), BF16),
                   jax.ShapeDtypeStruct((n_batch, H_A, DV_A, DK_A), F32),
                   jax.ShapeDtypeStruct((n_batch, H_A, 1, DK_A), F32),
                   jax.ShapeDtypeStruct((n_batch, H_A, 1, LANE), F32)],
        compiler_params=_params("parallel"),
        name="mlstm_init" if has_init else "mlstm",
    )(*args)


def _attn_kernel(q_ref, k_ref, o_ref, s_ref, mx_ref, acc_ref):
    tq = q_ref.shape[1]
    qi = pl.program_id(1)
    rows = H_B * tq

    def queries():
        return q_ref[...].reshape(rows, QK_PAD)

    def keys(j, n):
        return k_ref[pl.ds(pl.multiple_of(j * tq, tq), n * tq), :]

    def lane_fold(s):
        m = s[:, :LANE]
        for c in range(1, s.shape[1] // LANE):
            m = jnp.maximum(m, s[:, c * LANE:(c + 1) * LANE])
        return m

    def weighted(s, m, kj):
        p = jnp.exp2(s - jnp.concatenate([m] * (s.shape[1] // LANE), axis=1)).astype(BF16)
        return _dot(p, jnp.concatenate([kj[:, :KV_LORA], jnp.ones((kj.shape[0], LANE), BF16)], axis=1))

    def score_blocks(j, n):
        s = _dot_nt(queries(), keys(j, n))
        for c in range(n):
            s_ref[j + c] = s[:, c * tq:(c + 1) * tq]
        mx_ref[...] = jnp.maximum(mx_ref[...], lane_fold(s))

    def weigh_blocks(j, n):
        s = jnp.concatenate([s_ref[j + c] for c in range(n)], axis=1)
        acc_ref[...] += weighted(s, mx_ref[...], keys(j, n))

    k_diag = keys(qi, 1)
    r_idx = lax.broadcasted_iota(jnp.int32, (H_B, tq, tq), 1).reshape(rows, tq)
    c_idx = lax.broadcasted_iota(jnp.int32, (rows, tq), 1)
    s_diag = jnp.where(c_idx <= r_idx, _dot_nt(queries(), k_diag), -jnp.inf)
    odd = qi % 2 == 1

    mx_ref[...] = lane_fold(s_diag)
    pl.loop(0, qi // 2)(lambda jj: score_blocks(2 * jj, 2))
    pl.when(odd)(lambda: score_blocks(qi - 1, 1))

    m = jnp.broadcast_to(jnp.max(mx_ref[...], axis=1, keepdims=True), (rows, LANE))
    mx_ref[...] = m
    acc_ref[...] = weighted(s_diag, m, k_diag)
    pl.loop(0, qi // 2)(lambda jj: weigh_blocks(2 * jj, 2))
    pl.when(odd)(lambda: weigh_blocks(qi - 1, 1))

    o = acc_ref[:, :KV_LORA] / acc_ref[:, KV_LORA:]
    for hh in range(H_B):
        o_ref[:, hh * KV_LORA:(hh + 1) * KV_LORA] = o[hh * tq:(hh + 1) * tq].astype(o_ref.dtype)


def _attn_prompt(qcat, kcat, n_batch, t_len):
    tq = min(ATTN_TILE, t_len)
    assert tq == 2 * LANE
    nq = t_len // tq
    return pl.pallas_call(
        _attn_kernel,
        grid=(n_batch, nq),
        in_specs=[pl.BlockSpec((H_B, tq, QK_PAD), lambda b, i: (0, b * nq + i, 0)),
                  pl.BlockSpec((t_len, QK_PAD), lambda b, i: (b, 0))],
        out_specs=pl.BlockSpec((tq, H_B * KV_LORA), lambda b, i: (b * nq + i, 0)),
        out_shape=jax.ShapeDtypeStruct((n_batch * t_len, H_B * KV_LORA), BF16),
        scratch_shapes=[pltpu.VMEM((max(nq - 1, 1), H_B * tq, tq), F32),
                        pltpu.VMEM((H_B * tq, LANE), F32),
                        pltpu.VMEM((H_B * tq, 2 * KV_LORA), F32)],
        compiler_params=_params("parallel", "parallel"),
        name="attn_prompt",
    )(qcat, kcat)


def _attn_sample_kernel(pt_ref, q_ref, kvn_ref, krn_ref, kv_hbm, kr_hbm, o_ref,
                        kv_0, kr_0, kv_1, kr_1, sem, *, pages, t_new):
    g = pl.program_id(0)
    last = g + 1 == pl.num_programs(0)
    bufs = ((kv_0, kr_0), (kv_1, kr_1))

    def seq_copies(seq, slot):
        kv_buf, kr_buf = bufs[slot]
        kv_cp, kr_cp = [], []
        for p in range(pages):
            page = pt_ref[seq, p]
            kv_cp.append(pltpu.make_async_copy(kv_hbm.at[page], kv_buf.at[pl.ds(p * PAGE_SIZE, PAGE_SIZE), :],
                                               sem.at[slot, 0]))
            kr_cp.append(pltpu.make_async_copy(kr_hbm.at[page], kr_buf.at[p], sem.at[slot, 1]))
        return kv_cp, kr_cp

    def start(seq, slot):
        for cp in sum(seq_copies(seq, slot), []):
            cp.start()

    def attend(i, slot):
        q = q_ref[i]
        q_lat = q[:, :KV_LORA]
        q_rope = q[:, KV_LORA:KV_LORA + ROPE_DIM]
        kv_buf, kr_buf = bufs[slot]
        sub_pages = pages // SAMPLE_SUB_BLOCKS
        sub = sub_pages * PAGE_SIZE
        blocks = range(SAMPLE_SUB_BLOCKS)
        kv = [kv_buf[j * sub:(j + 1) * sub, :].astype(BF16) for j in blocks]
        kr_t = [jnp.concatenate([kr_buf[p].astype(BF16) for p in range(j * sub_pages, (j + 1) * sub_pages)],
                                axis=1) for j in blocks]
        s = [_dot_nt(q_lat, kv[j]) + _dot(q_rope, kr_t[j]) for j in blocks]
        kvn = kvn_ref[i].astype(BF16)
        s_new = _dot_nt(q_lat, kvn) + _dot_nt(q_rope, krn_ref[i].astype(BF16))
        r_tok = lax.broadcasted_iota(jnp.int32, s_new.shape, 0) % t_new
        c_tok = lax.broadcasted_iota(jnp.int32, s_new.shape, 1)
        s.append(jnp.where(c_tok <= r_tok, s_new, -jnp.inf))
        kv.append(kvn)
        m_blk = [jnp.max(x, axis=1, keepdims=True) for x in s]
        p = [jnp.exp2(x - mb) for x, mb in zip(s, m_blk)]
        l_blk = [jnp.sum(x, axis=1, keepdims=True) for x in p]
        o_blk = [_dot(x.astype(BF16), v) for x, v in zip(p, kv)]
        m = m_blk[0]
        for mb in m_blk[1:]:
            m = jnp.maximum(m, mb)
        l = jnp.zeros_like(m)
        acc = jnp.zeros((q.shape[0], KV_LORA), F32)
        for mb, lb, ob in zip(m_blk, l_blk, o_blk):
            w_blk = jnp.exp2(mb - m)
            l = l + w_blk * lb
            acc = acc + w_blk * ob
        o_ref[i] = (acc / l).astype(o_ref.dtype)

    @pl.when(g == 0)
    def _():
        start(0, 0)

    start(2 * g + 1, 1)
    for cp in sum(seq_copies(2 * g, 0), []):
        cp.wait()
    attend(0, 0)
    start(jnp.where(last, 0, 2 * g + 2), 0)
    for cp in sum(seq_copies(2 * g + 1, 1), []):
        cp.wait()
    attend(1, 1)

    @pl.when(last)
    def _():
        for cp in sum(seq_copies(0, 0), []):
            cp.wait()


def _attn_sample(page_table, q, kv_new, kr_new, cache_kv, cache_kr_t, t_new):
    n_b, n_pages = page_table.shape
    assert n_b % 2 == 0 and n_pages % SAMPLE_SUB_BLOCKS == 0
    past = n_pages * PAGE_SIZE
    rows = q.shape[1]
    pad = kv_new.shape[1]
    grid_spec = pltpu.PrefetchScalarGridSpec(
        num_scalar_prefetch=1,
        grid=(n_b // 2,),
        in_specs=[pl.BlockSpec((2, rows, QK_PAD), lambda g, pt: (g, 0, 0)),
                  pl.BlockSpec((2, pad, KV_LORA), lambda g, pt: (g, 0, 0)),
                  pl.BlockSpec((2, pad, ROPE_DIM), lambda g, pt: (g, 0, 0)),
                  pl.BlockSpec(memory_space=pl.ANY),
                  pl.BlockSpec(memory_space=pl.ANY)],
        out_specs=pl.BlockSpec((2, rows, KV_LORA), lambda g, pt: (g, 0, 0)),
        scratch_shapes=[pltpu.VMEM((past, KV_LORA), F32), pltpu.VMEM((n_pages, ROPE_DIM, PAGE_SIZE), F32),
                        pltpu.VMEM((past, KV_LORA), F32), pltpu.VMEM((n_pages, ROPE_DIM, PAGE_SIZE), F32),
                        pltpu.SemaphoreType.DMA((2, 2))])
    return pl.pallas_call(
        functools.partial(_attn_sample_kernel, pages=n_pages, t_new=t_new),
        grid_spec=grid_spec,
        out_shape=jax.ShapeDtypeStruct((n_b, rows, KV_LORA), BF16),
        compiler_params=_params("arbitrary"),
        name="attn_sample",
    )(page_table, q, kv_new, kr_new, cache_kv, cache_kr_t)


def _out_ffn_kernel(x_ref, ha_ref, ga_ref, gb_ref, ol_ref, wuv_ref, wout_ref, gmix_ref,
                    gpre_ref, gpost_ref, wgu_ref, wdn_ref, o_ref, act_ref):
    merged = []
    for pp in range(H_B // 2):
        cols = slice(pp * 2 * V_DIM, (pp + 1) * 2 * V_DIM)
        y_b = _dot(ol_ref[:, cols], wuv_ref[pp])
        y = ga_ref[:, cols].astype(F32) * ha_ref[:, cols].astype(F32) + gb_ref[:, cols].astype(F32) * y_b
        merged.append(y.astype(BF16))
    mix = _dot(jnp.concatenate(merged, axis=1), wout_ref[...])
    x = x_ref[...] + _rms(mix, gmix_ref[...])
    o_ref[...] = _ffn_body(x, gpre_ref[...], gpost_ref[...], wgu_ref, wdn_ref, act_ref)


def _out_ffn(x, ha, ga, gb, o_lat, w):
    m = x.shape[0]
    tm = min(ROW_TILE, m)
    row = pl.BlockSpec((tm, D_MODEL), lambda i: (i, 0))
    vec = _resident((1, D_MODEL))
    return pl.pallas_call(
        _out_ffn_kernel,
        grid=(m // tm,),
        in_specs=[row, row, row, row, row, _resident((H_B // 2, 2 * KV_LORA, 2 * V_DIM)),
                  _resident((D_MODEL, D_MODEL)), vec,
                  vec, vec, _resident((D_MODEL, 2 * D_FF)), _resident((D_FF, D_MODEL))],
        out_specs=row,
        out_shape=jax.ShapeDtypeStruct((m, D_MODEL), F32),
        scratch_shapes=[pltpu.VMEM((tm, D_FF), BF16)],
        compiler_params=_params("parallel"),
        name="out_ffn",
    )(x, ha, ga, gb, o_lat, w["w_uv"], w["w_out"], w["g_mix_post"],
      w["g_ffn2_pre"], w["g_ffn2_post"], w["w_ffn2_gu"], w["w_ffn2_down"])


def _prep_weights(norm_ffn1_pre, norm_ffn1_post, w_ffn1_gu, w_ffn1_down, norm_mix_pre, norm_mix_post,
                  w_in, b_gates, w_uq, norm_q_lat, norm_kv_lat, w_uk, w_uv, norm_mlstm_h, w_out,
                  norm_ffn2_pre, norm_ffn2_post, w_ffn2_gu, w_ffn2_down):
    half = ROPE_DIM // 2
    row = lambda g: g.reshape(1, -1).astype(F32)

    def swap_pad(wr):
        z = jnp.zeros(wr.shape[:-1] + (LANE - ROPE_DIM,), wr.dtype)
        return (jnp.concatenate([wr, z], axis=-1),
                jnp.concatenate([wr[..., half:], wr[..., :half], z], axis=-1))

    sizes = (H_A * DK_A, H_A * DK_A, H_A * DV_A, H_A, H_A, H_A * DV_A, Q_LORA, KV_LORA, ROPE_DIM, D_MODEL, D_MODEL)
    offs = [0]
    for n in sizes:
        offs.append(offs[-1] + n)
    qa, ka, va, ip, fp, oa, cq, ckv, kr, ga, gb = [w_in[:, offs[i]:offs[i + 1]] for i in range(len(sizes))]
    kr_p, kr_s = swap_pad(kr)
    gates = jnp.concatenate([ip, fp, jnp.zeros((D_MODEL, LANE - 2 * H_A), w_in.dtype)], axis=1)
    w_in2 = jnp.concatenate([qa, ka, va, oa, ga, gb, cq, ckv, kr_p, kr_s, gates], axis=1).astype(BF16)
    assert w_in2.shape[1] == IN_COLS_PAD

    uq = w_uq.reshape(Q_LORA, H_B, NOPE_DIM + ROPE_DIM)
    uq_p, uq_s = swap_pad(uq[..., NOPE_DIM:])
    w_uq2 = jnp.concatenate([uq[..., :NOPE_DIM].reshape(Q_LORA, -1), uq_p.reshape(Q_LORA, -1),
                             uq_s.reshape(Q_LORA, -1)], axis=1).astype(BF16)
    return dict(
        g_ffn1_pre=row(norm_ffn1_pre), g_ffn1_post=row(norm_ffn1_post),
        w_ffn1_gu=w_ffn1_gu.astype(BF16), w_ffn1_down=w_ffn1_down.astype(BF16),
        g_ffn2_pre=row(norm_ffn2_pre), g_ffn2_post=row(norm_ffn2_post),
        w_ffn2_gu=w_ffn2_gu.astype(BF16), w_ffn2_down=w_ffn2_down.astype(BF16),
        g_mix_pre=row(norm_mix_pre), g_mix_post=row(norm_mix_post),
        w_in=w_in2, w_uq=w_uq2,
        b_gates=jnp.concatenate([b_gates.astype(F32), jnp.zeros((LANE - 2 * H_A,), F32)]).reshape(1, LANE),
        g_q=row(norm_q_lat), g_kv=row(norm_kv_lat),
        w_uk=_pair_block_diag(jnp.transpose(w_uk, (1, 2, 0)).astype(BF16)),
        w_uv=_pair_block_diag(jnp.transpose(w_uv, (1, 0, 2)).astype(BF16)),
        g_h=row(norm_mlstm_h), w_out=w_out.astype(BF16))


def _pair_block_diag(w):
    h, a, b = w.shape
    w = w.reshape(h // 2, 2, a, b)
    z = jnp.zeros((h // 2, a, b), w.dtype)
    top = jnp.concatenate([w[:, 0], z], axis=2)
    bottom = jnp.concatenate([z, w[:, 1]], axis=2)
    return jnp.concatenate([top, bottom], axis=1)


def _rope_tables(pos):
    half = ROPE_DIM // 2
    inv_freq = ROPE_THETA ** (-jnp.arange(half, dtype=F32) / half)
    ang = pos.astype(F32)[:, None] * inv_freq[None, :]
    cos, sin = jnp.cos(ang), jnp.sin(ang)
    z = jnp.zeros((pos.shape[0], LANE - ROPE_DIM), F32)
    return jnp.concatenate([cos, cos, z], axis=1), jnp.concatenate([-sin, sin, z], axis=1)


def _gates_by_head(gt, n_batch, t_len, chunk):
    g = gt[:, :2 * H_A].reshape(n_batch, t_len // chunk, chunk, 2, H_A)
    return jnp.transpose(g, (0, 4, 1, 3, 2))


def kernel(x_prompt, x_sample, cache_kv_latent, cache_k_rope, state_mlstm_C, state_mlstm_n, state_mlstm_m,
           page_table, norm_ffn1_pre, norm_ffn1_post, w_ffn1_gu, w_ffn1_down, norm_mix_pre, norm_mix_post,
           w_in, b_gates, w_uq, norm_q_lat, norm_kv_lat, w_uk, w_uv, norm_mlstm_h, w_out,
           norm_ffn2_pre, norm_ffn2_post, w_ffn2_gu, w_ffn2_down):
    assert w_in.shape[0] == 1, "single-layer trunk"
    b_p, t_p, _ = x_prompt.shape
    b_s, t_s, _ = x_sample.shape
    past_len = page_table.shape[1] * PAGE_SIZE
    w = _prep_weights(norm_ffn1_pre[0], norm_ffn1_post[0], w_ffn1_gu[0], w_ffn1_down[0], norm_mix_pre[0],
                      norm_mix_post[0], w_in[0], b_gates[0], w_uq[0], norm_q_lat[0], norm_kv_lat[0], w_uk[0],
                      w_uv[0], norm_mlstm_h[0], w_out[0], norm_ffn2_pre[0], norm_ffn2_post[0], w_ffn2_gu[0],
                      w_ffn2_down[0])

    xp = x_prompt.reshape(b_p * t_p, D_MODEL)
    xp = _ffn(xp, w["g_ffn1_pre"], w["g_ffn1_post"], w["w_ffn1_gu"], w["w_ffn1_down"])
    cos_p, sin_p = _rope_tables(jnp.arange(t_p, dtype=jnp.int32))
    qa, ka, va, ga, gb, gt, ckv_p, kr_p, kcat, qcat = _proj(xp, w, cos_p, sin_p)
    chunk = min(MLSTM_CHUNK, t_p)
    ha, c_p, n_p, m_p = _mlstm(qa, ka, va, _gates_by_head(gt, b_p, t_p, chunk), w["g_h"], None, b_p, t_p, chunk, 1)
    o_lat = _attn_prompt(qcat, kcat, b_p, t_p)
    yp = _out_ffn(xp, ha, ga, gb, o_lat, w)

    t_pad = 16
    new_pad = 8
    xs = x_sample.reshape(b_s * t_s, D_MODEL)
    xs = _ffn(xs, w["g_ffn1_pre"], w["g_ffn1_post"], w["w_ffn1_gu"], w["w_ffn1_down"])
    cos_s, sin_s = _rope_tables(past_len + jnp.arange(t_s, dtype=jnp.int32))
    reps = min(ROW_TILE, b_s * t_s) // t_s
    qa, ka, va, ga, gb, gt, ckv_s, kr_s, _, qcat = _proj(xs, w, jnp.tile(cos_s, (reps, 1)), jnp.tile(sin_s, (reps, 1)))

    def pad_t(a, value=0.0):
        a = a.reshape(b_s, t_s, a.shape[-1])
        a = jnp.pad(a, ((0, 0), (0, t_pad - t_s), (0, 0)), constant_values=value)
        return a.reshape(b_s * t_pad, a.shape[-1])

    tok = jnp.arange(b_s * t_pad) % t_pad
    lane = jnp.arange(LANE)
    gt_pad = jnp.where((tok[:, None] >= t_s) & (lane[None, :] < H_A), NEG_BIG, pad_t(gt))
    m0 = jnp.broadcast_to(state_mlstm_m[0][:, :, None, None], (b_s, H_A, 1, LANE))
    ha, c_s, n_s, m_s = _mlstm(pad_t(qa), pad_t(ka), pad_t(va), _gates_by_head(gt_pad, b_s, t_pad, t_pad), w["g_h"],
                               (state_mlstm_C[0], state_mlstm_n[0][:, :, None, :], m0), b_s, t_pad, t_pad,
                               math.gcd(b_s, MLSTM_SAMPLE_SEQS))
    ha = ha.reshape(b_s, t_pad, D_MODEL)[:, :t_s].reshape(b_s * t_s, D_MODEL)

    q_s = qcat.reshape(H_B, b_s, t_s, QK_PAD).transpose(1, 0, 2, 3).reshape(b_s, H_B * t_s, QK_PAD)
    pad_new = lambda a: jnp.pad(a.reshape(b_s, t_s, -1), ((0, 0), (0, new_pad - t_s), (0, 0)))
    o_s = _attn_sample(page_table, q_s, pad_new(ckv_s), pad_new(kr_s), cache_kv_latent[0],
                       jnp.swapaxes(cache_k_rope[0], 1, 2), t_s)
    o_s = o_s.reshape(b_s, H_B, t_s, KV_LORA).transpose(0, 2, 1, 3).reshape(b_s * t_s, H_B * KV_LORA)
    ys = _out_ffn(xs, ha, ga, gb, o_s, w)

    return (yp.reshape(b_p, t_p, D_MODEL), ys.reshape(b_s, t_s, D_MODEL),
            ckv_p.reshape(1, b_p, t_p, KV_LORA), kr_p.reshape(1, b_p, t_p, ROPE_DIM),
            c_p[None], n_p[:, :, 0, :][None], m_p[:, :, 0, 0][None],
            ckv_s.reshape(1, b_s, t_s, KV_LORA), kr_s.reshape(1, b_s, t_s, ROPE_DIM),
            c_s[None], n_s[:, :, 0, :][None], m_s[:, :, 0, 0][None])
```

```python
import functools
import math

import jax
import jax.numpy as jnp
from jax import lax
from jax.experimental import pallas as pl
from jax.experimental.pallas import tpu as pltpu

F32 = jnp.float32
BF16 = jnp.bfloat16

D_MODEL = 1024
H_A = 4
DV_A = D_MODEL // H_A
DK_A = DV_A // 2
V_DIM = 128
H_B = D_MODEL // V_DIM
NOPE_DIM = 128
ROPE_DIM = 64
Q_LORA = 256
KV_LORA = 128
ROPE_THETA = 10000.0
SM_SCALE = (NOPE_DIM + ROPE_DIM) ** -0.5
Q_SCALE = SM_SCALE * math.log2(math.e)
D_FF = 2816
EPS = 1e-6
PAGE_SIZE = 128
QK_PAD = 256

LANE = 128
VMEM_LIMIT = 56 * 1024 * 1024
ROW_TILE = 512
FF_CHUNK = 256
MLSTM_CHUNK = 256
MLSTM_SAMPLE_SEQS = 4
ATTN_TILE = 256
SAMPLE_SUB_BLOCKS = 8
NEG_BIG = -1e30


def _params(*sem):
    return pltpu.CompilerParams(dimension_semantics=sem, vmem_limit_bytes=VMEM_LIMIT)


def _resident(shape):
    return pl.BlockSpec(shape, lambda *_: (0,) * len(shape), pipeline_mode=pl.Buffered(1))


def _rms(x, g):
    return x * lax.rsqrt(jnp.mean(x * x, axis=-1, keepdims=True) + EPS) * g


def _dot(a, b):
    return jnp.dot(a, b, preferred_element_type=F32)


def _dot_nt(a, b):
    return lax.dot_general(a, b, (((1,), (1,)), ((), ())), preferred_element_type=F32)


def _dot_tn(a, b):
    return lax.dot_general(a, b, (((0,), (0,)), ((), ())), preferred_element_type=F32)


def _ffn_body(x, gpre, gpost, wgu_ref, wdn_ref, act_ref):
    h = _rms(x, gpre).astype(BF16)
    for lo in range(0, D_FF, FF_CHUNK):
        g = _dot(h, wgu_ref[:, lo:lo + FF_CHUNK])
        u = _dot(h, wgu_ref[:, D_FF + lo:D_FF + lo + FF_CHUNK])
        act_ref[:, lo:lo + FF_CHUNK] = (g * jax.nn.sigmoid(g) * u).astype(BF16)
    f = _dot(act_ref[...], wdn_ref[...])
    return x + 0.5 * _rms(f, gpost)


def _ffn_kernel(x_ref, gpre_ref, gpost_ref, wgu_ref, wdn_ref, o_ref, act_ref):
    o_ref[...] = _ffn_body(x_ref[...], gpre_ref[...], gpost_ref[...], wgu_ref, wdn_ref, act_ref)


def _ffn(x, gpre, gpost, wgu, wdn):
    m = x.shape[0]
    tm = min(ROW_TILE, m)
    row = pl.BlockSpec((tm, D_MODEL), lambda i: (i, 0))
    return pl.pallas_call(
        _ffn_kernel,
        grid=(m // tm,),
        in_specs=[row, _resident((1, D_MODEL)), _resident((1, D_MODEL)),
                  _resident((D_MODEL, 2 * D_FF)), _resident((D_FF, D_MODEL))],
        out_specs=row,
        out_shape=jax.ShapeDtypeStruct((m, D_MODEL), F32),
        scratch_shapes=[pltpu.VMEM((tm, D_FF), BF16)],
        compiler_params=_params("parallel"),
        name="ffn",
    )(x, gpre, gpost, wgu, wdn)


_O_QA, _O_KA, _O_VA, _O_OA, _O_GA, _O_GB = 0, 512, 1024, 2048, 3072, 4096
_O_CQ, _O_CKV = 5120, 5376
IN_COLS_PAD = 5888


def _proj_kernel(x_ref, gpre_ref, win_ref, bg_ref, gq_ref, gkv_ref, wuq_ref, wuk_ref, cos_ref, sin_ref,
                 qa_ref, ka_ref, va_ref, ga_ref, gb_ref, gt_ref, ckv_ref, kr_ref, kcat_ref, qcat_ref):
    h = _rms(x_ref[...], gpre_ref[...]).astype(BF16)

    def seg(lo, n):
        return _dot(h, win_ref[:, lo:lo + n])

    qa_ref[...] = seg(_O_QA, H_A * DK_A).astype(BF16)
    ka_ref[...] = (seg(_O_KA, H_A * DK_A) * (DK_A ** -0.5)).astype(BF16)
    va_ref[...] = seg(_O_VA, D_MODEL).astype(BF16)
    ga_ref[...] = (jax.nn.sigmoid(seg(_O_GA, D_MODEL)) * jax.nn.sigmoid(seg(_O_OA, D_MODEL))).astype(BF16)
    gb_ref[...] = jax.nn.sigmoid(seg(_O_GB, D_MODEL)).astype(BF16)

    tail = seg(_O_CKV, 4 * LANE)
    ckv_raw, kr_raw, krs_raw, gt_raw = (tail[:, c * LANE:(c + 1) * LANE] for c in range(4))

    gt = gt_raw + bg_ref[...]
    logsig = jnp.minimum(gt, 0.0) - jnp.log1p(jnp.exp(-jnp.abs(gt)))
    lane = lax.broadcasted_iota(jnp.int32, gt.shape, 1)
    gt_ref[...] = jnp.where(lane < H_A, gt, logsig)

    cos = cos_ref[...]
    sin = sin_ref[...]
    ckv = _rms(ckv_raw, gkv_ref[...])
    kr = kr_raw * cos + krs_raw * sin
    ckv_ref[...] = ckv
    kr_ref[...] = kr[:, :ROPE_DIM]
    kcat_ref[:, :KV_LORA] = ckv.astype(BF16)
    kcat_ref[:, KV_LORA:] = kr.astype(BF16)

    cq = _rms(seg(_O_CQ, Q_LORA), gq_ref[...]).astype(BF16)
    q = _dot(cq, wuq_ref[...])
    q_lat2 = [_dot(q[:, pp * 2 * LANE:(pp + 1) * 2 * LANE].astype(BF16), wuk_ref[pp]) for pp in range(H_B // 2)]
    for hh in range(H_B):
        q_lat = q_lat2[hh // 2][:, (hh % 2) * LANE:(hh % 2 + 1) * LANE]
        lo = H_B * LANE + hh * LANE
        q_rope = q[:, lo:lo + LANE] * cos + q[:, lo + H_B * LANE:lo + (H_B + 1) * LANE] * sin
        qcat_ref[hh, :, :LANE] = (q_lat * Q_SCALE).astype(BF16)
        qcat_ref[hh, :, LANE:] = (q_rope * Q_SCALE).astype(BF16)


def _proj(x, w, cos, sin):
    m = x.shape[0]
    tm = min(ROW_TILE, m)
    ntab = cos.shape[0] // tm

    def row(n):
        return pl.BlockSpec((tm, n), lambda i: (i, 0))

    tab = pl.BlockSpec((tm, LANE), lambda i: (i % ntab, 0))
    outs = [(H_A * DK_A, BF16), (H_A * DK_A, BF16), (D_MODEL, BF16), (D_MODEL, BF16), (D_MODEL, BF16),
            (LANE, F32), (KV_LORA, F32), (ROPE_DIM, F32), (QK_PAD, BF16)]
    out_specs = [row(n) for n, _ in outs] + [pl.BlockSpec((H_B, tm, QK_PAD), lambda i: (0, i, 0))]
    out_shape = ([jax.ShapeDtypeStruct((m, n), dt) for n, dt in outs]
                 + [jax.ShapeDtypeStruct((H_B, m, QK_PAD), BF16)])
    return pl.pallas_call(
        _proj_kernel,
        grid=(m // tm,),
        in_specs=[row(D_MODEL), _resident((1, D_MODEL)), _resident((D_MODEL, IN_COLS_PAD)),
                  _resident((1, LANE)), _resident((1, Q_LORA)), _resident((1, KV_LORA)),
                  _resident((Q_LORA, 3 * H_B * LANE)), _resident((H_B // 2, 2 * NOPE_DIM, 2 * KV_LORA)), tab, tab],
        out_specs=out_specs,
        out_shape=out_shape,
        compiler_params=_params("parallel"),
        name="proj",
    )(x, w["g_mix_pre"], w["w_in"], w["b_gates"], w["g_q"], w["g_kv"], w["w_uq"], w["w_uk"], cos, sin)


def _mlstm_chunks(probs, tri, eye):
    ln = probs[0]["q"].shape[0]
    each = lambda fn, *lists: [fn(*a) for a in zip(*lists)]
    f_row = [p["f_row"] for p in probs]
    i_row = [p["i_row"] for p in probs]
    q = [p["q"] for p in probs]
    k = [p["k"] for p in probs]
    v = [p["v"] for p in probs]
    c_st = [p["c"] for p in probs]
    n_row = [p["n"] for p in probs]
    m = [p["m"] for p in probs]

    b_col = each(lambda f: jnp.sum(jnp.where(tri, f, 0.0), axis=1, keepdims=True), f_row)
    b_row = each(lambda b: jnp.sum(jnp.where(eye, b, 0.0), axis=0, keepdims=True), b_col)
    b_last = each(lambda b: b[:, ln - 1:ln], b_row)
    g_row = each(lambda bl, br, ir: bl - br + ir, b_last, b_row, i_row)
    m_new = each(lambda bl, mm, g: jnp.maximum(bl + mm, jnp.max(g, axis=1, keepdims=True)), b_last, m, g_row)
    a_row = each(lambda g, mn: jnp.exp(g - mn), g_row, m_new)
    decay = each(lambda bl, mm, mn: jnp.exp(bl + mm - mn), b_last, m, m_new)
    a_col = each(lambda a: jnp.sum(jnp.where(eye, a, 0.0), axis=1, keepdims=True), a_row)
    d = each(lambda bc, br, ir: jnp.where(tri, bc - br + ir, -jnp.inf), b_col, b_row, i_row)
    m_inter = each(lambda bc, mm: bc + mm, b_col, m)
    m_t = each(lambda mi, dd: jnp.maximum(mi, jnp.max(dd, axis=1, keepdims=True)), m_inter, d)
    s = each(lambda qq, kk, dd, mt: _dot_nt(qq, kk) * jnp.exp(dd - mt), q, k, d, m_t)
    inter = each(lambda mi, mt: jnp.exp(mi - mt), m_inter, m_t)
    qn = each(lambda qq, nn: jnp.sum(qq.astype(F32) * nn, axis=1, keepdims=True), q, n_row)
    den = each(lambda ss, it, x: jnp.sum(ss, axis=1, keepdims=True) + it * x, s, inter, qn)
    num = each(lambda ss, vv, it, qq, cc: _dot(ss.astype(BF16), vv) + it * _dot_nt(qq, cc.astype(BF16)),
               s, v, inter, q, c_st)
    h = each(lambda nu, de, mt: nu / jnp.maximum(jnp.abs(de), jnp.exp(-mt)), num, den, m_t)
    va = each(lambda vv, a: (vv.astype(F32) * a).astype(BF16), v, a_col)
    c_new = each(lambda dc, cc, x, kk: dc * cc + _dot_tn(x, kk), decay, c_st, va, k)
    n_new = each(lambda dc, nn, a, kk: dc * nn + jnp.sum(a * kk.astype(F32), axis=0, keepdims=True),
                 decay, n_row, a_col, k)
    return list(zip(h, c_new, n_new, m_new))


def _mlstm_kernel(*refs, chunk, n_chunks, n_seq, has_init):
    if has_init:
        q_ref, k_ref, v_ref, gt_ref, gh_ref, c0_ref, n0_ref, m0_ref, h_ref, c_ref, n_ref, m_ref = refs
        c_ref[...] = c0_ref[...]
        n_ref[...] = n0_ref[...]
        m_ref[...] = m0_ref[...]
    else:
        q_ref, k_ref, v_ref, gt_ref, gh_ref, h_ref, c_ref, n_ref, m_ref = refs
        c_ref[...] = jnp.zeros(c_ref.shape, F32)
        n_ref[...] = jnp.zeros(n_ref.shape, F32)
        m_ref[...] = jnp.zeros(m_ref.shape, F32)
    t_idx = lax.broadcasted_iota(jnp.int32, (chunk, chunk), 0)
    s_idx = lax.broadcasted_iota(jnp.int32, (chunk, chunk), 1)
    tri = s_idx <= t_idx
    eye = s_idx == t_idx
    t_len = chunk * n_chunks

    def step(c, carry):
        where, probs = [], []
        for b in range(n_seq):
            start = b * t_len + c * chunk
            rows = pl.ds(start if isinstance(start, int) else pl.multiple_of(start, chunk), chunk)
            for hh in range(H_A):
                qk_cols = slice(hh * DK_A, (hh + 1) * DK_A)
                v_cols = slice(hh * DV_A, (hh + 1) * DV_A)
                where.append((b, hh, rows, v_cols))
                probs.append(dict(q=q_ref[rows, qk_cols], k=k_ref[rows, qk_cols], v=v_ref[rows, v_cols],
                                  i_row=gt_ref[b, hh, c, 0:1, :], f_row=gt_ref[b, hh, c, 1:2, :],
                                  c=c_ref[b, hh], n=n_ref[b, hh], m=m_ref[b, hh][:, :1]))
        results = _mlstm_chunks(probs, tri, eye)
        normed = [_rms(h, gh_ref[:, v_cols]) for (h, _, _, _), (_, _, _, v_cols) in zip(results, where)]
        for (b, hh, rows, v_cols), (_, c_st, n_row, m), hn in zip(where, results, normed):
            h_ref[rows, v_cols] = hn.astype(h_ref.dtype)
            c_ref[b, hh] = c_st
            n_ref[b, hh] = n_row
            m_ref[b, hh] = jnp.broadcast_to(m, (1, LANE))
        return carry

    if n_chunks == 1:
        step(0, 0)
    else:
        lax.fori_loop(0, n_chunks, step, 0)


def _mlstm(qa, ka, va, gates_t, g_h, init, n_batch, t_len, chunk, n_seq):
    has_init = init is not None
    n_chunks = t_len // chunk
    st4 = lambda n: pl.BlockSpec((n_seq, H_A, 1, n), lambda b: (b, 0, 0, 0))
    c_spec = pl.BlockSpec((n_seq, H_A, DV_A, DK_A), lambda b: (b, 0, 0, 0))
    row = lambda n: pl.BlockSpec((n_seq * t_len, n), lambda b: (b, 0))
    in_specs = [row(H_A * DK_A), row(H_A * DK_A), row(D_MODEL),
                pl.BlockSpec((n_seq, H_A, n_chunks, 2, chunk), lambda b: (b, 0, 0, 0, 0)),
                _resident((1, D_MODEL))]
    args = [qa, ka, va, gates_t, g_h]
    if has_init:
        in_specs += [c_spec, st4(DK_A), st4(LANE)]
        args += list(init)
    return pl.pallas_call(
        functools.partial(_mlstm_kernel, chunk=chunk, n_chunks=n_chunks, n_seq=n_seq, has_init=has_init),
        grid=(n_batch // n_seq,),
        in_specs=in_specs,
        out_specs=[row(D_MODEL), c_spec, st4(DK_A), st4(LANE)],
        out_shape=[jax.ShapeDtypeStruct((n_batch * t_len, D_MODEL), BF16),
                   jax.ShapeDtypeStruct((n_batch, H_A, DV_A, DK_A), F32),
                   jax.ShapeDtypeStruct((n_batch, H_A, 1, DK_A), F32),
                   jax.ShapeDtypeStruct((n_batch, H_A, 1, LANE), F32)],
        compiler_params=_params("parallel"),
        name="mlstm_init" if has_init else "mlstm",
    )(*args)


def _attn_kernel(q_ref, k_ref, o_ref, s_ref, mx_ref, acc_ref):
    tq = q_ref.shape[1]
    qi = pl.program_id(1)
    rows = H_B * tq

    def queries():
        return q_ref[...].reshape(rows, QK_PAD)

    def keys(j, n):
        return k_ref[pl.ds(pl.multiple_of(j * tq, tq), n * tq), :]

    def lane_fold(s):
        m = s[:, :LANE]
        for c in range(1, s.shape[1] // LANE):
            m = jnp.maximum(m, s[:, c * LANE:(c + 1) * LANE])
        return m

    def weighted(s, m, kj):
        p = jnp.exp2(s - jnp.concatenate([m] * (s.shape[1] // LANE), axis=1)).astype(BF16)
        return _dot(p, jnp.concatenate([kj[:, :KV_LORA], jnp.ones((kj.shape[0], LANE), BF16)], axis=1))

    def score_blocks(j, n):
        s = _dot_nt(queries(), keys(j, n))
        for c in range(n):
            s_ref[j + c] = s[:, c * tq:(c + 1) * tq]
        mx_ref[...] = jnp.maximum(mx_ref[...], lane_fold(s))

    def weigh_blocks(j, n):
        s = jnp.concatenate([s_ref[j + c] for c in range(n)], axis=1)
        acc_ref[...] += weighted(s, mx_ref[...], keys(j, n))

    k_diag = keys(qi, 1)
    r_idx = lax.broadcasted_iota(jnp.int32, (H_B, tq, tq), 1).reshape(rows, tq)
    c_idx = lax.broadcasted_iota(jnp.int32, (rows, tq), 1)
    s_diag = jnp.where(c_idx <= r_idx, _dot_nt(queries(), k_diag), -jnp.inf)
    odd = qi % 2 == 1

    mx_ref[...] = lane_fold(s_diag)
    pl.loop(0, qi // 4)(lambda jj: score_blocks(4 * jj, 4))
    pl.when((qi // 2) % 2 == 1)(lambda: score_blocks((qi // 4) * 4, 2))
    pl.when(odd)(lambda: score_blocks(qi - 1, 1))

    m = jnp.broadcast_to(jnp.max(mx_ref[...], axis=1, keepdims=True), (rows, LANE))
    mx_ref[...] = m
    acc_ref[...] = weighted(s_diag, m, k_diag)
    pl.loop(0, qi // 4)(lambda jj: weigh_blocks(4 * jj, 4))
    pl.when((qi // 2) % 2 == 1)(lambda: weigh_blocks((qi // 4) * 4, 2))
    pl.when(odd)(lambda: weigh_blocks(qi - 1, 1))

    o = acc_ref[:, :KV_LORA] / acc_ref[:, KV_LORA:]
    for hh in range(H_B):
        o_ref[:, hh * KV_LORA:(hh + 1) * KV_LORA] = o[hh * tq:(hh + 1) * tq].astype(o_ref.dtype)


def _attn_prompt(qcat, kcat, n_batch, t_len):
    tq = min(ATTN_TILE, t_len)
    assert tq == 2 * LANE
    nq = t_len // tq
    return pl.pallas_call(
        _attn_kernel,
        grid=(n_batch, nq),
        in_specs=[pl.BlockSpec((H_B, tq, QK_PAD), lambda b, i: (0, b * nq + i, 0)),
                  pl.BlockSpec((t_len, QK_PAD), lambda b, i: (b, 0))],
        out_specs=pl.BlockSpec((tq, H_B * KV_LORA), lambda b, i: (b * nq + i, 0)),
        out_shape=jax.ShapeDtypeStruct((n_batch * t_len, H_B * KV_LORA), BF16),
        scratch_shapes=[pltpu.VMEM((max(nq - 1, 1), H_B * tq, tq), F32),
                        pltpu.VMEM((H_B * tq, LANE), F32),
                        pltpu.VMEM((H_B * tq, 2 * KV_LORA), F32)],
        compiler_params=_params("parallel", "parallel"),
        name="attn_prompt",
    )(qcat, kcat)


def _attn_sample_kernel(pt_ref, q_ref, kvn_ref, krn_ref, kv_hbm, kr_hbm, o_ref,
                        kv_0, kr_0, kv_1, kr_1, sem, *, pages, t_new):
    g = pl.program_id(0)
    last = g + 1 == pl.num_programs(0)
    bufs = ((kv_0, kr_0), (kv_1, kr_1))

    def seq_copies(seq, slot):
        kv_buf, kr_buf = bufs[slot]
        kv_cp, kr_cp = [], []
        for p in range(pages):
            page = pt_ref[seq, p]
            kv_cp.append(pltpu.make_async_copy(kv_hbm.at[page], kv_buf.at[pl.ds(p * PAGE_SIZE, PAGE_SIZE), :],
                                               sem.at[slot, 0]))
            kr_cp.append(pltpu.make_async_copy(kr_hbm.at[page], kr_buf.at[p], sem.at[slot, 1]))
        return kv_cp, kr_cp

    def start(seq, slot):
        for cp in sum(seq_copies(seq, slot), []):
            cp.start()

    def attend(i, slot):
        q = q_ref[i]
        q_lat = q[:, :KV_LORA]
        q_rope = q[:, KV_LORA:KV_LORA + ROPE_DIM]
        kv_buf, kr_buf = bufs[slot]
        sub_pages = pages // SAMPLE_SUB_BLOCKS
        sub = sub_pages * PAGE_SIZE
        blocks = range(SAMPLE_SUB_BLOCKS)
        kv = [kv_buf[j * sub:(j + 1) * sub, :].astype(BF16) for j in blocks]
        kr_t = [jnp.concatenate([kr_buf[p].astype(BF16) for p in range(j * sub_pages, (j + 1) * sub_pages)],
                                axis=1) for j in blocks]
        s = [_dot_nt(q_lat, kv[j]) + _dot(q_rope, kr_t[j]) for j in blocks]
        kvn = kvn_ref[i].astype(BF16)
        s_new = _dot_nt(q_lat, kvn) + _dot_nt(q_rope, krn_ref[i].astype(BF16))
        r_tok = lax.broadcasted_iota(jnp.int32, s_new.shape, 0) % t_new
        c_tok = lax.broadcasted_iota(jnp.int32, s_new.shape, 1)
        s.append(jnp.where(c_tok <= r_tok, s_new, -jnp.inf))
        kv.append(kvn)
        m_blk = [jnp.max(x, axis=1, keepdims=True) for x in s]
        p = [jnp.exp2(x - mb) for x, mb in zip(s, m_blk)]
        l_blk = [jnp.sum(x, axis=1, keepdims=True) for x in p]
        o_blk = [_dot(x.astype(BF16), v) for x, v in zip(p, kv)]
        m = m_blk[0]
        for mb in m_blk[1:]:
            m = jnp.maximum(m, mb)
        l = jnp.zeros_like(m)
        acc = jnp.zeros((q.shape[0], KV_LORA), F32)
        for mb, lb, ob in zip(m_blk, l_blk, o_blk):
            w_blk = jnp.exp2(mb - m)
            l = l + w_blk * lb
            acc = acc + w_blk * ob
        o_ref[i] = (acc / l).astype(o_ref.dtype)

    @pl.when(g == 0)
    def _():
        start(0, 0)

    start(2 * g + 1, 1)
    for cp in sum(seq_copies(2 * g, 0), []):
        cp.wait()
    attend(0, 0)
    start(jnp.where(last, 0, 2 * g + 2), 0)
    for cp in sum(seq_copies(2 * g + 1, 1), []):
        cp.wait()
    attend(1, 1)

    @pl.when(last)
    def _():
        for cp in sum(seq_copies(0, 0), []):
            cp.wait()


def _attn_sample(page_table, q, kv_new, kr_new, cache_kv, cache_kr_t, t_new):
    n_b, n_pages = page_table.shape
    assert n_b % 2 == 0 and n_pages % SAMPLE_SUB_BLOCKS == 0
    past = n_pages * PAGE_SIZE
    rows = q.shape[1]
    pad = kv_new.shape[1]
    grid_spec = pltpu.PrefetchScalarGridSpec(
        num_scalar_prefetch=1,
        grid=(n_b // 2,),
        in_specs=[pl.BlockSpec((2, rows, QK_PAD), lambda g, pt: (g, 0, 0)),
                  pl.BlockSpec((2, pad, KV_LORA), lambda g, pt: (g, 0, 0)),
                  pl.BlockSpec((2, pad, ROPE_DIM), lambda g, pt: (g, 0, 0)),
                  pl.BlockSpec(memory_space=pl.ANY),
                  pl.BlockSpec(memory_space=pl.ANY)],
        out_specs=pl.BlockSpec((2, rows, KV_LORA), lambda g, pt: (g, 0, 0)),
        scratch_shapes=[pltpu.VMEM((past, KV_LORA), F32), pltpu.VMEM((n_pages, ROPE_DIM, PAGE_SIZE), F32),
                        pltpu.VMEM((past, KV_LORA), F32), pltpu.VMEM((n_pages, ROPE_DIM, PAGE_SIZE), F32),
                        pltpu.SemaphoreType.DMA((2, 2))])
    return pl.pallas_call(
        functools.partial(_attn_sample_kernel, pages=n_pages, t_new=t_new),
        grid_spec=grid_spec,
        out_shape=jax.ShapeDtypeStruct((n_b, rows, KV_LORA), BF16),
        compiler_params=_params("arbitrary"),
        name="attn_sample",
    )(page_table, q, kv_new, kr_new, cache_kv, cache_kr_t)


def _out_ffn_kernel(x_ref, ha_ref, ga_ref, gb_ref, ol_ref, wuv_ref, wout_ref, gmix_ref,
                    gpre_ref, gpost_ref, wgu_ref, wdn_ref, o_ref, act_ref):
    merged = []
    for pp in range(H_B // 2):
        cols = slice(pp * 2 * V_DIM, (pp + 1) * 2 * V_DIM)
        y_b = _dot(ol_ref[:, cols], wuv_ref[pp])
        y = ga_ref[:, cols].astype(F32) * ha_ref[:, cols].astype(F32) + gb_ref[:, cols].astype(F32) * y_b
        merged.append(y.astype(BF16))
    mix = _dot(jnp.concatenate(merged, axis=1), wout_ref[...])
    x = x_ref[...] + _rms(mix, gmix_ref[...])
    o_ref[...] = _ffn_body(x, gpre_ref[...], gpost_ref[...], wgu_ref, wdn_ref, act_ref)


def _out_ffn(x, ha, ga, gb, o_lat, w):
    m = x.shape[0]
    tm = min(ROW_TILE, m)
    row = pl.BlockSpec((tm, D_MODEL), lambda i: (i, 0))
    vec = _resident((1, D_MODEL))
    return pl.pallas_call(
        _out_ffn_kernel,
        grid=(m // tm,),
        in_specs=[row, row, row, row, row, _resident((H_B // 2, 2 * KV_LORA, 2 * V_DIM)),
                  _resident((D_MODEL, D_MODEL)), vec,
                  vec, vec, _resident((D_MODEL, 2 * D_FF)), _resident((D_FF, D_MODEL))],
        out_specs=row,
        out_shape=jax.ShapeDtypeStruct((m, D_MODEL), F32),
        scratch_shapes=[pltpu.VMEM((tm, D_FF), BF16)],
        compiler_params=_params("parallel"),
        name="out_ffn",
    )(x, ha, ga, gb, o_lat, w["w_uv"], w["w_out"], w["g_mix_post"],
      w["g_ffn2_pre"], w["g_ffn2_post"], w["w_ffn2_gu"], w["w_ffn2_down"])


def _prep_weights(norm_ffn1_pre, norm_ffn1_post, w_ffn1_gu, w_ffn1_down, norm_mix_pre, norm_mix_post,
                  w_in, b_gates, w_uq, norm_q_lat, norm_kv_lat, w_uk, w_uv, norm_mlstm_h, w_out,
                  norm_ffn2_pre, norm_ffn2_post, w_ffn2_gu, w_ffn2_down):
    half = ROPE_DIM // 2
    row = lambda g: g.reshape(1, -1).astype(F32)

    def swap_pad(wr):
        z = jnp.zeros(wr.shape[:-1] + (LANE - ROPE_DIM,), wr.dtype)
        return (jnp.concatenate([wr, z], axis=-1),
                jnp.concatenate([wr[..., half:], wr[..., :half], z], axis=-1))

    sizes = (H_A * DK_A, H_A * DK_A, H_A * DV_A, H_A, H_A, H_A * DV_A, Q_LORA, KV_LORA, ROPE_DIM, D_MODEL, D_MODEL)
    offs = [0]
    for n in sizes:
        offs.append(offs[-1] + n)
    qa, ka, va, ip, fp, oa, cq, ckv, kr, ga, gb = [w_in[:, offs[i]:offs[i + 1]] for i in range(len(sizes))]
    kr_p, kr_s = swap_pad(kr)
    gates = jnp.concatenate([ip, fp, jnp.zeros((D_MODEL, LANE - 2 * H_A), w_in.dtype)], axis=1)
    w_in2 = jnp.concatenate([qa, ka, va, oa, ga, gb, cq, ckv, kr_p, kr_s, gates], axis=1).astype(BF16)
    assert w_in2.shape[1] == IN_COLS_PAD

    uq = w_uq.reshape(Q_LORA, H_B, NOPE_DIM + ROPE_DIM)
    uq_p, uq_s = swap_pad(uq[..., NOPE_DIM:])
    w_uq2 = jnp.concatenate([uq[..., :NOPE_DIM].reshape(Q_LORA, -1), uq_p.reshape(Q_LORA, -1),
                             uq_s.reshape(Q_LORA, -1)], axis=1).astype(BF16)
    return dict(
        g_ffn1_pre=row(norm_ffn1_pre), g_ffn1_post=row(norm_ffn1_post),
        w_ffn1_gu=w_ffn1_gu.astype(BF16), w_ffn1_down=w_ffn1_down.astype(BF16),
        g_ffn2_pre=row(norm_ffn2_pre), g_ffn2_post=row(norm_ffn2_post),
        w_ffn2_gu=w_ffn2_gu.astype(BF16), w_ffn2_down=w_ffn2_down.astype(BF16),
        g_mix_pre=row(norm_mix_pre), g_mix_post=row(norm_mix_post),
        w_in=w_in2, w_uq=w_uq2,
        b_gates=jnp.concatenate([b_gates.astype(F32), jnp.zeros((LANE - 2 * H_A,), F32)]).reshape(1, LANE),
        g_q=row(norm_q_lat), g_kv=row(norm_kv_lat),
        w_uk=_pair_block_diag(jnp.transpose(w_uk, (1, 2, 0)).astype(BF16)),
        w_uv=_pair_block_diag(jnp.transpose(w_uv, (1, 0, 2)).astype(BF16)),
        g_h=row(norm_mlstm_h), w_out=w_out.astype(BF16))


def _pair_block_diag(w):
    h, a, b = w.shape
    w = w.reshape(h // 2, 2, a, b)
    z = jnp.zeros((h // 2, a, b), w.dtype)
    top = jnp.concatenate([w[:, 0], z], axis=2)
    bottom = jnp.concatenate([z, w[:, 1]], axis=2)
    return jnp.concatenate([top, bottom], axis=1)


def _rope_tables(pos):
    half = ROPE_DIM // 2
    inv_freq = ROPE_THETA ** (-jnp.arange(half, dtype=F32) / half)
    ang = pos.astype(F32)[:, None] * inv_freq[None, :]
    cos, sin = jnp.cos(ang), jnp.sin(ang)
    z = jnp.zeros((pos.shape[0], LANE - ROPE_DIM), F32)
    return jnp.concatenate([cos, cos, z], axis=1), jnp.concatenate([-sin, sin, z], axis=1)


def _gates_by_head(gt, n_batch, t_len, chunk):
    g = gt[:, :2 * H_A].reshape(n_batch, t_len // chunk, chunk, 2, H_A)
    return jnp.transpose(g, (0, 4, 1, 3, 2))


def kernel(x_prompt, x_sample, cache_kv_latent, cache_k_rope, state_mlstm_C, state_mlstm_n, state_mlstm_m,
           page_table, norm_ffn1_pre, norm_ffn1_post, w_ffn1_gu, w_ffn1_down, norm_mix_pre, norm_mix_post,
           w_in, b_gates, w_uq, norm_q_lat, norm_kv_lat, w_uk, w_uv, norm_mlstm_h, w_out,
           norm_ffn2_pre, norm_ffn2_post, w_ffn2_gu, w_ffn2_down):
    assert w_in.shape[0] == 1, "single-layer trunk"
    b_p, t_p, _ = x_prompt.shape
    b_s, t_s, _ = x_sample.shape
    past_len = page_table.shape[1] * PAGE_SIZE
    w = _prep_weights(norm_ffn1_pre[0], norm_ffn1_post[0], w_ffn1_gu[0], w_ffn1_down[0], norm_mix_pre[0],
                      norm_mix_post[0], w_in[0], b_gates[0], w_uq[0], norm_q_lat[0], norm_kv_lat[0], w_uk[0],
                      w_uv[0], norm_mlstm_h[0], w_out[0], norm_ffn2_pre[0], norm_ffn2_post[0], w_ffn2_gu[0],
                      w_ffn2_down[0])

    xp = x_prompt.reshape(b_p * t_p, D_MODEL)
    xp = _ffn(xp, w["g_ffn1_pre"], w["g_ffn1_post"], w["w_ffn1_gu"], w["w_ffn1_down"])
    cos_p, sin_p = _rope_tables(jnp.arange(t_p, dtype=jnp.int32))
    qa, ka, va, ga, gb, gt, ckv_p, kr_p, kcat, qcat = _proj(xp, w, cos_p, sin_p)
    chunk = min(MLSTM_CHUNK, t_p)
    ha, c_p, n_p, m_p = _mlstm(qa, ka, va, _gates_by_head(gt, b_p, t_p, chunk), w["g_h"], None, b_p, t_p, chunk, 1)
    o_lat = _attn_prompt(qcat, kcat, b_p, t_p)
    yp = _out_ffn(xp, ha, ga, gb, o_lat, w)

    t_pad = 16
    new_pad = 8
    xs = x_sample.reshape(b_s * t_s, D_MODEL)
    xs = _ffn(xs, w["g_ffn1_pre"], w["g_ffn1_post"], w["w_ffn1_gu"], w["w_ffn1_down"])
    cos_s, sin_s = _rope_tables(past_len + jnp.arange(t_s, dtype=jnp.int32))
    reps = min(ROW_TILE, b_s * t_s) // t_s
    qa, ka, va, ga, gb, gt, ckv_s, kr_s, _, qcat = _proj(xs, w, jnp.tile(cos_s, (reps, 1)), jnp.tile(sin_s, (reps, 1)))

    def pad_t(a, value=0.0):
        a = a.reshape(b_s, t_s, a.shape[-1])
        a = jnp.pad(a, ((0, 0), (0, t_pad - t_s), (0, 0)), constant_values=value)
        return a.reshape(b_s * t_pad, a.shape[-1])

    tok = jnp.arange(b_s * t_pad) % t_pad
    lane = jnp.arange(LANE)
    gt_pad = jnp.where((tok[:, None] >= t_s) & (lane[None, :] < H_A), NEG_BIG, pad_t(gt))
    m0 = jnp.broadcast_to(state_mlstm_m[0][:, :, None, None], (b_s, H_A, 1, LANE))
    ha, c_s, n_s, m_s = _mlstm(pad_t(qa), pad_t(ka), pad_t(va), _gates_by_head(gt_pad, b_s, t_pad, t_pad), w["g_h"],
                               (state_mlstm_C[0], state_mlstm_n[0][:, :, None, :], m0), b_s, t_pad, t_pad,
                               math.gcd(b_s, MLSTM_SAMPLE_SEQS))
    ha = ha.reshape(b_s, t_pad, D_MODEL)[:, :t_s].reshape(b_s * t_s, D_MODEL)

    q_s = qcat.reshape(H_B, b_s, t_s, QK_PAD).transpose(1, 0, 2, 3).reshape(b_s, H_B * t_s, QK_PAD)
    pad_new = lambda a: jnp.pad(a.reshape(b_s, t_s, -1), ((0, 0), (0, new_pad - t_s), (0, 0)))
    o_s = _attn_sample(page_table, q_s, pad_new(ckv_s), pad_new(kr_s), cache_kv_latent[0],
                       jnp.swapaxes(cache_k_rope[0], 1, 2), t_s)
    o_s = o_s.reshape(b_s, H_B, t_s, KV_LORA).transpose(0, 2, 1, 3).reshape(b_s * t_s, H_B * KV_LORA)
    ys = _out_ffn(xs, ha, ga, gb, o_s, w)

    return (yp.reshape(b_p, t_p, D_MODEL), ys.reshape(b_s, t_s, D_MODEL),
            ckv_p.reshape(1, b_p, t_p, KV_LORA), kr_p.reshape(1, b_p, t_p, ROPE_DIM),
            c_p[None], n_p[:, :, 0, :][None], m_p[:, :, 0, 0][None],
            ckv_s.reshape(1, b_s, t_s, KV_LORA), kr_s.reshape(1, b_s, t_s, ROPE_DIM),
            c_s[None], n_s[:, :, 0, :][None], m_s[:, :, 0, 0][None])
```
